```python
import jax, jax.numpy as jnp
from jax import lax
import numpy as np

D_MODEL = 1024
BATCH = 8
SEQ = 2048
DEPTH = 2

D_MIX = D_MODEL
CONV_CH = D_MIX // 2
SGU_CH = D_MIX - CONV_CH
CONV_GROUPS = 8
SGU_HEADS = 4
SGU_HEAD_DIM = SGU_CH // SGU_HEADS
CONV_WIDTH = 31
CHUNK = 128
D_IN = 2 * CONV_CH + 2 * SGU_CH
D_FF = 2816
N_EXPERTS = 8
TOP_K = 2
N_DENSE = (DEPTH + 1) // 2
N_MOE = DEPTH // 2
EPS = 1e-6

kernel_name = "hybrid_conv_gmlp_moe_block"


def rms_norm(x, g):
    xf = x.astype(jnp.float32)
    y = xf * lax.rsqrt(jnp.mean(xf * xf, axis=-1, keepdims=True) + EPS)
    return (y * g.astype(jnp.float32)).astype(x.dtype)


def group_layer_norm(x, n_groups, g, b):
    shp = x.shape
    xf = x.astype(jnp.float32).reshape(shp[:-1] + (n_groups, shp[-1] // n_groups))
    mu = jnp.mean(xf, axis=-1, keepdims=True)
    var = jnp.mean(jnp.square(xf - mu), axis=-1, keepdims=True)
    y = ((xf - mu) * lax.rsqrt(var + EPS)).reshape(shp)
    return (y * g.astype(jnp.float32) + b.astype(jnp.float32)).astype(x.dtype)


def causal_depthwise_conv(c, w, b):
    y = lax.conv_general_dilated(
        c, w, window_strides=(1,), padding=[(CONV_WIDTH - 1, 0)],
        dimension_numbers=('NWC', 'WIO', 'NWC'), feature_group_count=c.shape[-1])
    return y + b


def hybrid_mixer(h, w_in, conv_w, conv_b, conv_ng, conv_nb, sgu_ng, sgu_nb, sgu_w, sgu_b, w_out):
    B, S, _ = h.shape
    z = jnp.einsum('bsd,de->bse', h, w_in)
    a, gate, u, v = jnp.split(z, [CONV_CH, 2 * CONV_CH, 2 * CONV_CH + SGU_CH], axis=-1)
    c = a * jax.nn.sigmoid(gate)
    c = causal_depthwise_conv(c, conv_w, conv_b)
    c = jax.nn.silu(group_layer_norm(c, CONV_GROUPS, conv_ng, conv_nb))
    u = jax.nn.gelu(u, approximate=False)
    v = group_layer_norm(jax.nn.gelu(v, approximate=False), SGU_HEADS, sgu_ng, sgu_nb)
    n_chunks = S // CHUNK
    vc = v.reshape(B, n_chunks, CHUNK, SGU_HEADS, SGU_HEAD_DIM)
    mask = jnp.tril(jnp.ones((CHUNK, CHUNK), dtype=bool))
    ws = jnp.where(mask[None], sgu_w, jnp.zeros((), sgu_w.dtype))
    sp = jnp.einsum('hts,bnshd->bnthd', ws, vc) + jnp.transpose(sgu_b)[None, None, :, :, None]
    g_out = u * sp.reshape(B, S, SGU_CH)
    y = jnp.concatenate([c, g_out], axis=-1)
    return jnp.einsum('bse,ed->bsd', y, w_out)


def swiglu(h, wg, wu, wd):
    return jnp.einsum('...f,fd->...d', jax.nn.silu(h @ wg) * (h @ wu), wd)


def moe_swiglu(h, router, wg, wu, wd):
    B, S, D = h.shape
    ht = h.reshape(B * S, D)
    logits = jnp.einsum('td,de->te', ht, router).astype(jnp.float32)
    top_vals, top_idx = lax.top_k(logits, TOP_K)
    gates = jax.nn.softmax(top_vals, axis=-1)
    dense_gate = jnp.sum(jax.nn.one_hot(top_idx, N_EXPERTS, dtype=jnp.float32) * gates[..., None], axis=1)
    dense_gate = dense_gate.astype(h.dtype)
    out = jnp.zeros_like(ht)
    for e in range(N_EXPERTS):
        out = out + dense_gate[:, e:e + 1] * swiglu(ht, wg[e], wu[e], wd[e])
    return out.reshape(B, S, D)


def setup_inputs(seed: int = 0) -> dict:
    key = jax.random.key(seed)
    ks = jax.random.split(key, 24)
    nrm = lambda k, shp, s: jax.random.normal(k, shp, jnp.float32) * s
    L = DEPTH
    return {
        "x": nrm(ks[0], (BATCH, SEQ, D_MODEL), 1.0),
        "norm_mix": 1.0 + nrm(ks[1], (L, D_MODEL), 0.05),
        "w_in": nrm(ks[2], (L, D_MODEL, D_IN), D_MODEL ** -0.5),
        "conv_w": nrm(ks[3], (L, CONV_WIDTH, 1, CONV_CH), CONV_WIDTH ** -0.5),
        "conv_b": nrm(ks[4], (L, CONV_CH), 0.02),
        "conv_ng": 1.0 + nrm(ks[5], (L, CONV_CH), 0.05),
        "conv_nb": nrm(ks[6], (L, CONV_CH), 0.02),
        "sgu_ng": 1.0 + nrm(ks[7], (L, SGU_CH), 0.05),
        "sgu_nb": nrm(ks[8], (L, SGU_CH), 0.02),
        "sgu_w": nrm(ks[9], (L, SGU_HEADS, CHUNK, CHUNK), CHUNK ** -0.5),
        "sgu_b": 1.0 + nrm(ks[10], (L, SGU_HEADS, CHUNK), 0.1),
        "w_out": nrm(ks[11], (L, D_MIX, D_MODEL), D_MIX ** -0.5),
        "norm_ffn": 1.0 + nrm(ks[12], (L, D_MODEL), 0.05),
        "ffn_wg": nrm(ks[13], (N_DENSE, D_MODEL, D_FF), D_MODEL ** -0.5),
        "ffn_wu": nrm(ks[14], (N_DENSE, D_MODEL, D_FF), D_MODEL ** -0.5),
        "ffn_wd": nrm(ks[15], (N_DENSE, D_FF, D_MODEL), D_FF ** -0.5),
        "moe_router": nrm(ks[16], (N_MOE, D_MODEL, N_EXPERTS), D_MODEL ** -0.5),
        "moe_wg": nrm(ks[17], (N_MOE, N_EXPERTS, D_MODEL, D_FF), D_MODEL ** -0.5),
        "moe_wu": nrm(ks[18], (N_MOE, N_EXPERTS, D_MODEL, D_FF), D_MODEL ** -0.5),
        "moe_wd": nrm(ks[19], (N_MOE, N_EXPERTS, D_FF, D_MODEL), D_FF ** -0.5),
        "norm_final": 1.0 + nrm(ks[20], (D_MODEL,), 0.05),
    }


def reference(x, norm_mix, w_in, conv_w, conv_b, conv_ng, conv_nb, sgu_ng, sgu_nb, sgu_w, sgu_b,
              w_out, norm_ffn, ffn_wg, ffn_wu, ffn_wd, moe_router, moe_wg, moe_wu, moe_wd, norm_final):
    for l in range(DEPTH):
        h = rms_norm(x, norm_mix[l])
        x = x + hybrid_mixer(h, w_in[l], conv_w[l], conv_b[l], conv_ng[l], conv_nb[l],
                             sgu_ng[l], sgu_nb[l], sgu_w[l], sgu_b[l], w_out[l])
        h = rms_norm(x, norm_ffn[l])
        i = l // 2
        if l % 2 == 0:
            x = x + swiglu(h, ffn_wg[i], ffn_wu[i], ffn_wd[i])
        else:
            x = x + moe_swiglu(h, moe_router[i], moe_wg[i], moe_wu[i], moe_wd[i])
    return rms_norm(x, norm_final)
```

```python
import functools

import jax
import jax.numpy as jnp
from jax import lax
from jax.experimental import pallas as pl
from jax.experimental.pallas import tpu as pltpu

F32 = jnp.float32
BF16 = jnp.bfloat16

EPS = 1e-6
CONV_WIDTH = 31
CONV_GROUPS = 8
SGU_HEADS = 4
CHUNK = 128
N_EXPERTS = 8

SUBLANES = 8
LANES = 128
CONV_HALO = 32
VMEM_LIMIT_BYTES = 56 * 1024 * 1024


def _rms_norm(x, g):
    ms = jnp.mean(x * x, axis=-1, keepdims=True)
    return x * lax.rsqrt(ms + EPS) * g


def _gelu(x):
    return 0.5 * x * (1.0 + lax.erf(x * (2.0 ** -0.5)))


def _silu(x):
    return x * jax.nn.sigmoid(x)


def _mixer_kernel(x_ref, nw_ref, win_ref, cw_ref, cb_ref, cng_ref, cnb_ref, sng_ref, snb_ref,
                  sw_ref, sb_ref, wout_ref, gavg_ref, o_ref, cbuf_ref, *, ts, cc, sc):
    s = pl.program_id(1)
    x = x_ref[0]
    h = _rms_norm(x, nw_ref[...]).astype(BF16)
    z = jnp.dot(h, win_ref[...], preferred_element_type=F32)
    a = z[:, :cc]
    gate = z[:, cc:2 * cc]
    u = z[:, 2 * cc:2 * cc + sc]
    v = z[:, 2 * cc + sc:]

    @pl.when(s == 0)
    def _():
        cbuf_ref[0:CONV_HALO, :] = jnp.zeros((CONV_HALO, cc), F32)

    cbuf_ref[CONV_HALO:CONV_HALO + ts, :] = a * jax.nn.sigmoid(gate)
    acc = jnp.broadcast_to(cb_ref[...], (ts, cc))
    lead = CONV_HALO - SUBLANES
    for b in range(SUBLANES):
        xb = cbuf_ref[pl.ds(CONV_HALO - lead - b, ts + lead), :]
        for q in range(lead // SUBLANES + 1):
            d = SUBLANES * q + b
            if d >= CONV_WIDTH:
                continue
            k = CONV_WIDTH - 1 - d
            acc = acc + cw_ref[k:k + 1, :] * xb[lead - SUBLANES * q:lead - SUBLANES * q + ts, :]
    cbuf_ref[0:CONV_HALO, :] = cbuf_ref[ts:ts + CONV_HALO, :]

    gavg = gavg_ref[...]
    acc_hi = acc.astype(BF16)
    acc_lo = (acc - acc_hi.astype(F32)).astype(BF16)
    mu = (jnp.dot(acc_hi, gavg, preferred_element_type=F32)
          + jnp.dot(acc_lo, gavg, preferred_element_type=F32))
    dev = acc - mu
    var = jnp.dot((dev * dev).astype(BF16), gavg, preferred_element_type=F32)
    cn = dev * lax.rsqrt(var + EPS) * cng_ref[...] + cnb_ref[...]
    c_out = _silu(cn).astype(BF16)

    u = _gelu(u)
    v = _gelu(v)
    hd = sc // SGU_HEADS
    row = lax.broadcasted_iota(jnp.int32, (CHUNK, CHUNK), 0)
    col = lax.broadcasted_iota(jnp.int32, (CHUNK, CHUNK), 1)
    tril = row >= col
    g_cols = []
    for hh in range(SGU_HEADS):
        vh = v[:, hh * hd:(hh + 1) * hd]
        mu_h = jnp.mean(vh, axis=-1, keepdims=True)
        dh = vh - mu_h
        var_h = jnp.mean(dh * dh, axis=-1, keepdims=True)
        vn = (dh * lax.rsqrt(var_h + EPS) * sng_ref[:, hh * hd:(hh + 1) * hd]
              + snb_ref[:, hh * hd:(hh + 1) * hd]).astype(BF16)
        ws = jnp.where(tril, sw_ref[hh], 0.0).astype(BF16)
        rows = []
        for ci in range(ts // CHUNK):
            sp = jnp.dot(ws, vn[ci * CHUNK:(ci + 1) * CHUNK, :], preferred_element_type=F32)
            rows.append(sp + sb_ref[hh])
        sp_h = rows[0] if len(rows) == 1 else jnp.concatenate(rows, axis=0)
        g_cols.append((u[:, hh * hd:(hh + 1) * hd] * sp_h).astype(BF16))
    g_out = jnp.concatenate(g_cols, axis=1)

    y = (jnp.dot(c_out, wout_ref[0:cc, :], preferred_element_type=F32)
         + jnp.dot(g_out, wout_ref[cc:cc + sc, :], preferred_element_type=F32))
    o_ref[0] = x + y


def _mixer(x, nw, w_in, conv_w, conv_b, conv_ng, conv_nb, sgu_ng, sgu_nb, sgu_w, sgu_b, w_out,
           *, ts):
    B, S, D = x.shape
    cc = conv_w.shape[-1]
    sc = sgu_ng.shape[-1]
    d_in = w_in.shape[-1]
    assert S % ts == 0 and ts % CHUNK == 0 and d_in == 2 * cc + 2 * sc
    gs = cc // CONV_GROUPS
    gid = jnp.arange(cc) // gs
    gavg = jnp.where(gid[:, None] == gid[None, :], 1.0 / gs, 0.0).astype(BF16)
    sb_full = jnp.broadcast_to(sgu_b[:, :, None], (SGU_HEADS, CHUNK, sc // SGU_HEADS)).astype(F32)
    row = lambda p: p.reshape(1, -1).astype(F32)
    const2 = lambda shape: pl.BlockSpec(shape, lambda b, s: (0, 0))
    const3 = lambda shape: pl.BlockSpec(shape, lambda b, s: (0, 0, 0))
    kern = functools.partial(_mixer_kernel, ts=ts, cc=cc, sc=sc)
    return pl.pallas_call(
        kern,
        grid=(B, S // ts),
        in_specs=[
            pl.BlockSpec((1, ts, D), lambda b, s: (b, s, 0)),
            const2((1, D)),
            const2((D, d_in)),
            const2((CONV_WIDTH, cc)),
            const2((1, cc)), const2((1, cc)), const2((1, cc)),
            const2((1, sc)), const2((1, sc)),
            const3((SGU_HEADS, CHUNK, CHUNK)),
            const3((SGU_HEADS, CHUNK, sc // SGU_HEADS)),
            const2((cc + sc, D)),
            const2((cc, cc)),
        ],
        out_specs=pl.BlockSpec((1, ts, D), lambda b, s: (b, s, 0)),
        out_shape=jax.ShapeDtypeStruct((B, S, D), F32),
        scratch_shapes=[pltpu.VMEM((CONV_HALO + ts, cc), F32)],
        compiler_params=pltpu.CompilerParams(
            dimension_semantics=("arbitrary", "arbitrary"),
            vmem_limit_bytes=VMEM_LIMIT_BYTES),
        name="mixer",
    )(x, row(nw), w_in.astype(BF16), conv_w.reshape(CONV_WIDTH, cc).astype(F32), row(conv_b),
      row(conv_ng), row(conv_nb), row(sgu_ng), row(sgu_nb), sgu_w.astype(F32), sb_full,
      w_out.astype(BF16), gavg)


def _swiglu_tile(h, wg_ref, wu_ref, wd_ref, ff_chunk):
    ff = wg_ref.shape[-1]
    out = None
    for c0 in range(0, ff, ff_chunk):
        c1 = min(ff, c0 + ff_chunk)
        g = jnp.dot(h, wg_ref[:, c0:c1], preferred_element_type=F32)
        u = jnp.dot(h, wu_ref[:, c0:c1], preferred_element_type=F32)
        a = (_silu(g) * u).astype(BF16)
        y = jnp.dot(a, wd_ref[c0:c1, :], preferred_element_type=F32)
        out = y if out is None else out + y
    return out


def _ffn_kernel(x_ref, nw_ref, wg_ref, wu_ref, wd_ref, o_ref, *, ff_chunk):
    x = x_ref[...]
    h = _rms_norm(x, nw_ref[...]).astype(BF16)
    o_ref[...] = x + _swiglu_tile(h, wg_ref, wu_ref, wd_ref, ff_chunk)


def _ffn(x2, nw, wg, wu, wd, *, tm, ff_chunk):
    T, D = x2.shape
    FF = wg.shape[-1]
    assert T % tm == 0
    kern = functools.partial(_ffn_kernel, ff_chunk=ff_chunk)
    return pl.pallas_call(
        kern,
        grid=(T // tm,),
        in_specs=[
            pl.BlockSpec((tm, D), lambda i: (i, 0)),
            pl.BlockSpec((1, D), lambda i: (0, 0)),
            pl.BlockSpec((D, FF), lambda i: (0, 0)),
            pl.BlockSpec((D, FF), lambda i: (0, 0)),
            pl.BlockSpec((FF, D), lambda i: (0, 0)),
        ],
        out_specs=pl.BlockSpec((tm, D), lambda i: (i, 0)),
        out_shape=jax.ShapeDtypeStruct((T, D), F32),
        compiler_params=pltpu.CompilerParams(
            dimension_semantics=("arbitrary",), vmem_limit_bytes=VMEM_LIMIT_BYTES),
        name="ffn",
    )(x2, nw.reshape(1, D).astype(F32), wg.astype(BF16), wu.astype(BF16), wd.astype(BF16))


def _top2_gates(logits, n_experts):
    lane = lax.broadcasted_iota(jnp.int32, logits.shape, 1)
    m1 = jnp.max(logits, axis=-1, keepdims=True)
    i1 = jnp.min(jnp.where(logits == m1, lane, LANES), axis=-1, keepdims=True)
    rest = jnp.where(lane == i1, -jnp.inf, logits)
    m2 = jnp.max(rest, axis=-1, keepdims=True)
    i2 = jnp.min(jnp.where(rest == m2, lane, LANES), axis=-1, keepdims=True)
    e2 = jnp.exp(m2 - m1)
    den = 1.0 + e2
    g1 = 1.0 / den
    g2 = e2 / den
    return jnp.where(lane == i1, g1, 0.0) + jnp.where(lane == i2, g2, 0.0)


def _moe_dense_kernel(x_ref, nw_ref, rt_ref, wg_ref, wu_ref, wd_ref, fw_ref, o_ref,
                      h_ref, gate_ref, acc_ref, *, ff_chunk, n_experts):
    e = pl.program_id(1)

    @pl.when(e == 0)
    def _():
        x = x_ref[...]
        hf = _rms_norm(x, nw_ref[...])
        h_ref[...] = hf.astype(BF16)
        logits = jnp.dot(hf.astype(BF16), rt_ref[...], preferred_element_type=F32)
        lane = lax.broadcasted_iota(jnp.int32, logits.shape, 1)
        logits = jnp.where(lane < n_experts, logits, -jnp.inf)
        gate_ref[...] = _top2_gates(logits, n_experts)
        acc_ref[...] = x

    lane = lax.broadcasted_iota(jnp.int32, gate_ref.shape, 1)
    ge = jnp.sum(jnp.where(lane == e, gate_ref[...], 0.0), axis=-1, keepdims=True)
    y = _swiglu_tile(h_ref[...], wg_ref.at[0], wu_ref.at[0], wd_ref.at[0], ff_chunk)
    acc_ref[...] += ge * y

    @pl.when(e == n_experts - 1)
    def _():
        o_ref[...] = _rms_norm(acc_ref[...], fw_ref[...])


def _moe(x2, nw, router, wg, wu, wd, fw):
    return _moe_dense(x2, nw, router, wg, wu, wd, fw, tm=512, ff_chunk=1408)


def _moe_dense(x2, nw, router, wg, wu, wd, fw, *, tm, ff_chunk):
    T, D = x2.shape
    E, _, FF = wg.shape
    rt = jnp.zeros((D, LANES), F32).at[:, :E].set(router).astype(BF16)
    kern = functools.partial(_moe_dense_kernel, ff_chunk=ff_chunk, n_experts=E)
    return pl.pallas_call(
        kern,
        grid=(T // tm, E),
        in_specs=[
            pl.BlockSpec((tm, D), lambda i, e: (i, 0)),
            pl.BlockSpec((1, D), lambda i, e: (0, 0)),
            pl.BlockSpec((D, LANES), lambda i, e: (0, 0)),
            pl.BlockSpec((1, D, FF), lambda i, e: (e, 0, 0)),
            pl.BlockSpec((1, D, FF), lambda i, e: (e, 0, 0)),
            pl.BlockSpec((1, FF, D), lambda i, e: (e, 0, 0)),
            pl.BlockSpec((1, D), lambda i, e: (0, 0)),
        ],
        out_specs=pl.BlockSpec((tm, D), lambda i, e: (i, 0)),
        out_shape=jax.ShapeDtypeStruct((T, D), F32),
        scratch_shapes=[pltpu.VMEM((tm, D), BF16), pltpu.VMEM((tm, LANES), F32),
                        pltpu.VMEM((tm, D), F32)],
        compiler_params=pltpu.CompilerParams(
            dimension_semantics=("arbitrary", "arbitrary"), vmem_limit_bytes=VMEM_LIMIT_BYTES),
        name="moe_dense",
    )(x2, nw.reshape(1, D).astype(F32), rt, wg.astype(BF16), wu.astype(BF16), wd.astype(BF16),
      fw.reshape(1, D).astype(F32))


def kernel(x, norm_mix, w_in, conv_w, conv_b, conv_ng, conv_nb, sgu_ng, sgu_nb, sgu_w, sgu_b, w_out,
           norm_ffn, ffn_wg, ffn_wu, ffn_wd, moe_router, moe_wg, moe_wu, moe_wd, norm_final):
    B, S, D = x.shape
    depth = norm_mix.shape[0]
    assert depth == 2, "trunk is one dense layer followed by one MoE layer"
    mix = functools.partial(_mixer, ts=256)
    x = mix(x, norm_mix[0], w_in[0], conv_w[0], conv_b[0], conv_ng[0], conv_nb[0],
            sgu_ng[0], sgu_nb[0], sgu_w[0], sgu_b[0], w_out[0])
    x2 = _ffn(x.reshape(B * S, D), norm_ffn[0], ffn_wg[0], ffn_wu[0], ffn_wd[0], tm=512, ff_chunk=1408)
    x = mix(x2.reshape(B, S, D), norm_mix[1], w_in[1], conv_w[1], conv_b[1], conv_ng[1], conv_nb[1],
            sgu_ng[1], sgu_nb[1], sgu_w[1], sgu_b[1], w_out[1])
    y2 = _moe(x.reshape(B * S, D), norm_ffn[1], moe_router[0], moe_wg[0], moe_wu[0], moe_wd[0],
              norm_final)
    return y2.reshape(B, S, D)
```

```python
import functools

import jax
import jax.numpy as jnp
from jax import lax
from jax.experimental import pallas as pl
from jax.experimental.pallas import tpu as pltpu

F32 = jnp.float32
BF16 = jnp.bfloat16

EPS = 1e-6
CONV_WIDTH = 31
CONV_GROUPS = 8
SGU_HEADS = 4
CHUNK = 128
TOP_K = 2

SUBLANES = 8
LANES = 128
BF16_ROWS = 16
CONV_HALO = 32
VMEM_LIMIT_BYTES = 56 * 1024 * 1024

MOE_WINDOW = 256
MOE_TILE = 512


def _rms_norm(x, g):
    ms = jnp.mean(x * x, axis=-1, keepdims=True)
    return x * lax.rsqrt(ms + EPS) * g


def _gelu(x):
    return 0.5 * x * (1.0 + lax.erf(x * (2.0 ** -0.5)))


def _silu(x):
    return x * jax.nn.sigmoid(x)


def _mixer_kernel(x_ref, nw_ref, win_ref, cw_ref, cb_ref, cng_ref, cnb_ref, sng_ref, snb_ref,
                  sw_ref, sb_ref, wout_ref, gavg_ref, o_ref, cbuf_ref, *, ts, cc, sc):
    s = pl.program_id(1)
    x = x_ref[0]
    h = _rms_norm(x, nw_ref[...]).astype(BF16)
    z = jnp.dot(h, win_ref[...], preferred_element_type=F32)
    a = z[:, :cc]
    gate = z[:, cc:2 * cc]
    u = z[:, 2 * cc:2 * cc + sc]
    v = z[:, 2 * cc + sc:]

    @pl.when(s == 0)
    def _():
        cbuf_ref[0:CONV_HALO, :] = jnp.zeros((CONV_HALO, cc), F32)

    cbuf_ref[CONV_HALO:CONV_HALO + ts, :] = a * jax.nn.sigmoid(gate)
    acc = jnp.broadcast_to(cb_ref[...], (ts, cc))
    lead = CONV_HALO - SUBLANES
    for b in range(SUBLANES):
        xb = cbuf_ref[pl.ds(CONV_HALO - lead - b, ts + lead), :]
        for q in range(lead // SUBLANES + 1):
            d = SUBLANES * q + b
            if d >= CONV_WIDTH:
                continue
            k = CONV_WIDTH - 1 - d
            acc = acc + cw_ref[k:k + 1, :] * xb[lead - SUBLANES * q:lead - SUBLANES * q + ts, :]
    cbuf_ref[0:CONV_HALO, :] = cbuf_ref[ts:ts + CONV_HALO, :]

    gavg = gavg_ref[...]
    acc_hi = acc.astype(BF16)
    acc_lo = (acc - acc_hi.astype(F32)).astype(BF16)
    mu = (jnp.dot(acc_hi, gavg, preferred_element_type=F32)
          + jnp.dot(acc_lo, gavg, preferred_element_type=F32))
    dev = acc - mu
    var = jnp.dot((dev * dev).astype(BF16), gavg, preferred_element_type=F32)
    cn = dev * lax.rsqrt(var + EPS) * cng_ref[...] + cnb_ref[...]
    c_out = _silu(cn).astype(BF16)

    u = _gelu(u)
    v = _gelu(v)
    hd = sc // SGU_HEADS
    row = lax.broadcasted_iota(jnp.int32, (CHUNK, CHUNK), 0)
    col = lax.broadcasted_iota(jnp.int32, (CHUNK, CHUNK), 1)
    tril = row >= col
    g_cols = []
    for hh in range(SGU_HEADS):
        vh = v[:, hh * hd:(hh + 1) * hd]
        mu_h = jnp.mean(vh, axis=-1, keepdims=True)
        dh = vh - mu_h
        var_h = jnp.mean(dh * dh, axis=-1, keepdims=True)
        vn = (dh * lax.rsqrt(var_h + EPS) * sng_ref[:, hh * hd:(hh + 1) * hd]
              + snb_ref[:, hh * hd:(hh + 1) * hd]).astype(BF16)
        ws = jnp.where(tril, sw_ref[hh], 0.0).astype(BF16)
        rows = []
        for ci in range(ts // CHUNK):
            sp = jnp.dot(ws, vn[ci * CHUNK:(ci + 1) * CHUNK, :], preferred_element_type=F32)
            rows.append(sp + sb_ref[hh])
        sp_h = rows[0] if len(rows) == 1 else jnp.concatenate(rows, axis=0)
        g_cols.append((u[:, hh * hd:(hh + 1) * hd] * sp_h).astype(BF16))
    g_out = jnp.concatenate(g_cols, axis=1)

    y = (jnp.dot(c_out, wout_ref[0:cc, :], preferred_element_type=F32)
         + jnp.dot(g_out, wout_ref[cc:cc + sc, :], preferred_element_type=F32))
    o_ref[0] = x + y


def _mixer(x, nw, w_in, conv_w, conv_b, conv_ng, conv_nb, sgu_ng, sgu_nb, sgu_w, sgu_b, w_out,
           *, ts):
    B, S, D = x.shape
    cc = conv_w.shape[-1]
    sc = sgu_ng.shape[-1]
    d_in = w_in.shape[-1]
    assert S % ts == 0 and ts % CHUNK == 0 and d_in == 2 * cc + 2 * sc
    gs = cc // CONV_GROUPS
    gid = jnp.arange(cc) // gs
    gavg = jnp.where(gid[:, None] == gid[None, :], 1.0 / gs, 0.0).astype(BF16)
    sb_full = jnp.broadcast_to(sgu_b[:, :, None], (SGU_HEADS, CHUNK, sc // SGU_HEADS)).astype(F32)
    row = lambda p: p.reshape(1, -1).astype(F32)
    const2 = lambda shape: pl.BlockSpec(shape, lambda b, s: (0, 0))
    const3 = lambda shape: pl.BlockSpec(shape, lambda b, s: (0, 0, 0))
    kern = functools.partial(_mixer_kernel, ts=ts, cc=cc, sc=sc)
    return pl.pallas_call(
        kern,
        grid=(B, S // ts),
        in_specs=[
            pl.BlockSpec((1, ts, D), lambda b, s: (b, s, 0)),
            const2((1, D)),
            const2((D, d_in)),
            const2((CONV_WIDTH, cc)),
            const2((1, cc)), const2((1, cc)), const2((1, cc)),
            const2((1, sc)), const2((1, sc)),
            const3((SGU_HEADS, CHUNK, CHUNK)),
            const3((SGU_HEADS, CHUNK, sc // SGU_HEADS)),
            const2((cc + sc, D)),
            const2((cc, cc)),
        ],
        out_specs=pl.BlockSpec((1, ts, D), lambda b, s: (b, s, 0)),
        out_shape=jax.ShapeDtypeStruct((B, S, D), F32),
        scratch_shapes=[pltpu.VMEM((CONV_HALO + ts, cc), F32)],
        compiler_params=pltpu.CompilerParams(
            dimension_semantics=("arbitrary", "arbitrary"),
            vmem_limit_bytes=VMEM_LIMIT_BYTES),
        name="mixer",
    )(x, row(nw), w_in.astype(BF16), conv_w.reshape(CONV_WIDTH, cc).astype(F32), row(conv_b),
      row(conv_ng), row(conv_nb), row(sgu_ng), row(sgu_nb), sgu_w.astype(F32), sb_full,
      w_out.astype(BF16), gavg)


def _swiglu_tile(h, wg_ref, wu_ref, wd_ref, ff_chunk):
    ff = wg_ref.shape[-1]
    out = None
    for c0 in range(0, ff, ff_chunk):
        c1 = min(ff, c0 + ff_chunk)
        g = jnp.dot(h, wg_ref[:, c0:c1], preferred_element_type=F32)
        u = jnp.dot(h, wu_ref[:, c0:c1], preferred_element_type=F32)
        a = (_silu(g) * u).astype(BF16)
        y = jnp.dot(a, wd_ref[c0:c1, :], preferred_element_type=F32)
        out = y if out is None else out + y
    return out


def _ffn_kernel(x_ref, nw_ref, wg_ref, wu_ref, wd_ref, o_ref, *, ff_chunk):
    x = x_ref[...]
    h = _rms_norm(x, nw_ref[...]).astype(BF16)
    o_ref[...] = x + _swiglu_tile(h, wg_ref, wu_ref, wd_ref, ff_chunk)


def _ffn(x2, nw, wg, wu, wd, *, tm, ff_chunk):
    T, D = x2.shape
    FF = wg.shape[-1]
    assert T % tm == 0
    kern = functools.partial(_ffn_kernel, ff_chunk=ff_chunk)
    return pl.pallas_call(
        kern,
        grid=(T // tm,),
        in_specs=[
            pl.BlockSpec((tm, D), lambda i: (i, 0)),
            pl.BlockSpec((1, D), lambda i: (0, 0)),
            pl.BlockSpec((D, FF), lambda i: (0, 0)),
            pl.BlockSpec((D, FF), lambda i: (0, 0)),
            pl.BlockSpec((FF, D), lambda i: (0, 0)),
        ],
        out_specs=pl.BlockSpec((tm, D), lambda i: (i, 0)),
        out_shape=jax.ShapeDtypeStruct((T, D), F32),
        compiler_params=pltpu.CompilerParams(
            dimension_semantics=("arbitrary",), vmem_limit_bytes=VMEM_LIMIT_BYTES),
        name="ffn",
    )(x2, nw.reshape(1, D).astype(F32), wg.astype(BF16), wu.astype(BF16), wd.astype(BF16))


def _round_up(n, m):
    return (n + m - 1) // m * m


def _route_kernel(x_ref, nw_ref, rt_ref, ltri_ref, utri_ref,
                  hn_ref, gates_ref, rankm_ref, dest_ref, cnt_ref, *, n_experts):
    x = x_ref[...]
    h = _rms_norm(x, nw_ref[...]).astype(BF16)
    hn_ref[...] = h
    logits = jnp.dot(h, rt_ref[...], preferred_element_type=F32)
    lane = lax.broadcasted_iota(jnp.int32, logits.shape, 1)
    logits = jnp.where(lane < n_experts, logits, -jnp.inf)
    m1 = jnp.max(logits, axis=-1, keepdims=True)
    i1 = jnp.min(jnp.where(logits == m1, lane, LANES), axis=-1, keepdims=True)
    rest = jnp.where(lane == i1, -jnp.inf, logits)
    m2 = jnp.max(rest, axis=-1, keepdims=True)
    i2 = jnp.min(jnp.where(rest == m2, lane, LANES), axis=-1, keepdims=True)
    e2 = jnp.exp(m2 - m1)
    den = 1.0 + e2
    sel1 = lane == i1
    sel2 = lane == i2
    gates_ref[...] = jnp.where(sel1, 1.0 / den, 0.0) + jnp.where(sel2, e2 / den, 0.0)

    sel = jnp.logical_or(sel1, sel2)
    self = sel.astype(F32)
    rank = jnp.dot(ltri_ref[...], self.astype(BF16), preferred_element_type=F32)
    rankm_ref[...] = jnp.where(sel, rank, -1.0)
    cnt = jnp.sum(self, axis=0, keepdims=True)
    cnt_pad = jnp.floor((cnt + (BF16_ROWS - 1.0)) * (1.0 / BF16_ROWS)) * BF16_ROWS
    seg = jnp.dot(jnp.broadcast_to(cnt_pad, (SUBLANES, LANES)).astype(BF16), utri_ref[...],
                  preferred_element_type=F32)[0:1, :]
    pos = seg + rank
    dest1 = jnp.sum(jnp.where(sel1, pos, 0.0), axis=-1, keepdims=True)
    dest2 = jnp.sum(jnp.where(sel2, pos, 0.0), axis=-1, keepdims=True)
    dest = jnp.where(lane == 0, dest1, jnp.where(lane == 1, dest2, -1.0))
    dest_ref[0] = jnp.transpose(dest)[0:SUBLANES, :]
    cnt_ref[0] = cnt.astype(jnp.int32)


def _route(x2, nw, router, *, win):
    T, D = x2.shape
    E = router.shape[-1]
    assert T % win == 0 and E <= SUBLANES
    nwin = T // win
    rt = jnp.zeros((D, LANES), F32).at[:, :E].set(router).astype(BF16)
    r = jnp.arange(win)
    ltri = (r[:, None] > r[None, :]).astype(BF16)
    l = jnp.arange(LANES)
    utri = (l[:, None] < l[None, :]).astype(BF16)
    kern = functools.partial(_route_kernel, n_experts=E)
    return pl.pallas_call(
        kern,
        grid=(nwin,),
        in_specs=[
            pl.BlockSpec((win, D), lambda w: (w, 0)),
            pl.BlockSpec((1, D), lambda w: (0, 0)),
            pl.BlockSpec((D, LANES), lambda w: (0, 0)),
            pl.BlockSpec((win, win), lambda w: (0, 0)),
            pl.BlockSpec((LANES, LANES), lambda w: (0, 0)),
        ],
        out_specs=[
            pl.BlockSpec((win, D), lambda w: (w, 0)),
            pl.BlockSpec((win, LANES), lambda w: (w, 0)),
            pl.BlockSpec((win, LANES), lambda w: (w, 0)),
            pl.BlockSpec((1, SUBLANES, win), lambda w: (w, 0, 0)),
            pl.BlockSpec((1, 1, LANES), lambda w: (w, 0, 0)),
        ],
        out_shape=[
            jax.ShapeDtypeStruct((T, D), BF16),
            jax.ShapeDtypeStruct((T, LANES), F32),
            jax.ShapeDtypeStruct((T, LANES), F32),
            jax.ShapeDtypeStruct((nwin, SUBLANES, win), F32),
            jax.ShapeDtypeStruct((nwin, 1, LANES), jnp.int32),
        ],
        compiler_params=pltpu.CompilerParams(
            dimension_semantics=("arbitrary",), vmem_limit_bytes=VMEM_LIMIT_BYTES),
        name="moe_route",
    )(x2, nw.reshape(1, D).astype(F32), rt, ltri, utri)


def _compact_kernel(seg_ref, off_ref, fill_ref, hn_ref, dest_ref, xs_ref, stage_ref, sem,
                    *, win, n_experts, rows, zero_slabs):
    w = pl.program_id(0)
    nwin = pl.num_programs(0)
    slot = w % 2

    def slab_copy(ww, e, sl):
        src = pl.multiple_of(seg_ref[ww * n_experts + e], BF16_ROWS)
        dst = pl.multiple_of(off_ref[ww * n_experts + e], BF16_ROWS)
        return pltpu.make_async_copy(stage_ref.at[sl, pl.ds(src, win)],
                                     xs_ref.at[pl.ds(dst, win)], sem.at[sl, e])

    def zero_copy(dst, e):
        return pltpu.make_async_copy(stage_ref.at[slot, pl.ds(rows, win)],
                                     xs_ref.at[pl.ds(pl.multiple_of(dst, BF16_ROWS), win)],
                                     sem.at[2, e])

    @pl.when(w == 0)
    def _():
        zeros = jnp.zeros((win, stage_ref.shape[-1]), BF16)
        stage_ref[0, rows:rows + win, :] = zeros
        stage_ref[1, rows:rows + win, :] = zeros

    d1 = dest_ref[0, 0:1, :]
    d2 = dest_ref[0, 1:2, :]
    r = lax.broadcasted_iota(jnp.int32, (rows, win), 0).astype(F32)
    onehot = jnp.logical_or(r == d1, r == d2).astype(BF16)
    xg = jnp.dot(onehot, hn_ref[...], preferred_element_type=F32)
    stage_ref[slot, 0:rows, :] = xg.astype(BF16)

    @pl.when(w > 0)
    def _():
        for e in range(n_experts):
            slab_copy(w - 1, e, 1 - slot).wait()

    for e in range(n_experts):
        slab_copy(w, e, slot).start()

    @pl.when(w == nwin - 1)
    def _():
        for e in range(n_experts):
            slab_copy(w, e, slot).wait()
        for k in range(zero_slabs):
            for e in range(n_experts):
                zero_copy(fill_ref[k * n_experts + e], e).start()
            for e in range(n_experts):
                zero_copy(fill_ref[k * n_experts + e], e).wait()
        tail0 = fill_ref[zero_slabs * n_experts]
        ntail = fill_ref[zero_slabs * n_experts + 1]

        def tail_start(j, c):
            zero_copy(tail0 + j * win, 0).start()
            return c

        def tail_wait(j, c):
            zero_copy(tail0 + j * win, 0).wait()
            return c

        lax.fori_loop(0, ntail, tail_start, 0)
        lax.fori_loop(0, ntail, tail_wait, 0)


def _compact(seg, off, fill, hn, dest, *, win, n_experts, cap, zero_slabs):
    T, D = hn.shape
    nwin = T // win
    rows = _round_up(TOP_K * win + n_experts * (BF16_ROWS - 1), BF16_ROWS)
    kern = functools.partial(_compact_kernel, win=win, n_experts=n_experts, rows=rows,
                             zero_slabs=zero_slabs)
    return pl.pallas_call(
        kern,
        grid_spec=pltpu.PrefetchScalarGridSpec(
            num_scalar_prefetch=3,
            grid=(nwin,),
            in_specs=[
                pl.BlockSpec((win, D), lambda w, *_: (w, 0)),
                pl.BlockSpec((1, SUBLANES, win), lambda w, *_: (w, 0, 0)),
            ],
            out_specs=pl.BlockSpec(memory_space=pl.ANY),
            scratch_shapes=[pltpu.VMEM((2, rows + win, D), BF16),
                            pltpu.SemaphoreType.DMA((3, n_experts))],
        ),
        out_shape=jax.ShapeDtypeStruct((cap, D), BF16),
        compiler_params=pltpu.CompilerParams(
            dimension_semantics=("arbitrary",), vmem_limit_bytes=VMEM_LIMIT_BYTES),
        name="moe_compact",
    )(seg, off, fill, hn, dest)


def _expert_kernel(te_ref, tv_ref, xs_ref, wg_ref, wu_ref, wd_ref, ys_ref, *, ff_chunk):
    i = pl.program_id(0)

    @pl.when(tv_ref[i] > 0)
    def _():
        y = _swiglu_tile(xs_ref[...], wg_ref.at[0], wu_ref.at[0], wd_ref.at[0], ff_chunk)
        ys_ref[...] = y.astype(BF16)

    @pl.when(tv_ref[i] == 0)
    def _():
        ys_ref[...] = jnp.zeros(ys_ref.shape, BF16)


def _experts(tile_e, tile_v, xs, wg, wu, wd, *, tm, ff_chunk):
    cap, D = xs.shape
    FF = wg.shape[-1]
    assert cap == tile_e.shape[0] * tm
    kern = functools.partial(_expert_kernel, ff_chunk=ff_chunk)
    return pl.pallas_call(
        kern,
        grid_spec=pltpu.PrefetchScalarGridSpec(
            num_scalar_prefetch=2,
            grid=(cap // tm,),
            in_specs=[
                pl.BlockSpec((tm, D), lambda i, te, tv: (i, 0)),
                pl.BlockSpec((1, D, FF), lambda i, te, tv: (te[i], 0, 0)),
                pl.BlockSpec((1, D, FF), lambda i, te, tv: (te[i], 0, 0)),
                pl.BlockSpec((1, FF, D), lambda i, te, tv: (te[i], 0, 0)),
            ],
            out_specs=pl.BlockSpec((tm, D), lambda i, te, tv: (i, 0)),
        ),
        out_shape=jax.ShapeDtypeStruct((cap, D), BF16),
        compiler_params=pltpu.CompilerParams(
            dimension_semantics=("arbitrary",), vmem_limit_bytes=VMEM_LIMIT_BYTES),
        name="moe_experts",
    )(tile_e, tile_v, xs, wg.astype(BF16), wu.astype(BF16), wd.astype(BF16))


def _combine_kernel(off_ref, x_ref, gates_ref, rankm_ref, fw_ref, ys_ref, o_ref,
                    slab_ref, sem, *, win, n_experts):
    w = pl.program_id(0)
    nwin = pl.num_programs(0)
    slot = w % 2

    def slab_copy(ww, e, sl):
        src = pl.multiple_of(off_ref[ww * n_experts + e], BF16_ROWS)
        return pltpu.make_async_copy(ys_ref.at[pl.ds(src, win)], slab_ref.at[sl, e], sem.at[sl, e])

    @pl.when(w == 0)
    def _():
        for e in range(n_experts):
            slab_copy(0, e, 0).start()

    @pl.when(w + 1 < nwin)
    def _():
        for e in range(n_experts):
            slab_copy(w + 1, e, 1 - slot).start()

    for e in range(n_experts):
        slab_copy(w, e, slot).wait()

    acc = x_ref[...]
    gates = gates_ref[...]
    rankm = rankm_ref[...]
    lane = lax.broadcasted_iota(jnp.int32, (win, win), 1).astype(F32)
    for e in range(n_experts):
        rk = rankm[:, e:e + 1]
        onehot = (lane == rk).astype(BF16)
        y = jnp.dot(onehot, slab_ref[slot, e], preferred_element_type=F32)
        acc = acc + gates[:, e:e + 1] * y
    o_ref[...] = _rms_norm(acc, fw_ref[...])


def _combine(off, x2, gates, rankm, fw, ys, *, win, n_experts):
    T, D = x2.shape
    E = n_experts
    kern = functools.partial(_combine_kernel, win=win, n_experts=E)
    return pl.pallas_call(
        kern,
        grid_spec=pltpu.PrefetchScalarGridSpec(
            num_scalar_prefetch=1,
            grid=(T // win,),
            in_specs=[
                pl.BlockSpec((win, D), lambda w, *_: (w, 0)),
                pl.BlockSpec((win, LANES), lambda w, *_: (w, 0)),
                pl.BlockSpec((win, LANES), lambda w, *_: (w, 0)),
                pl.BlockSpec((1, D), lambda w, *_: (0, 0)),
                pl.BlockSpec(memory_space=pl.ANY),
            ],
            out_specs=pl.BlockSpec((win, D), lambda w, *_: (w, 0)),
            scratch_shapes=[pltpu.VMEM((2, E, win, D), BF16), pltpu.SemaphoreType.DMA((2, E))],
        ),
        out_shape=jax.ShapeDtypeStruct((T, D), F32),
        compiler_params=pltpu.CompilerParams(
            dimension_semantics=("arbitrary",), vmem_limit_bytes=VMEM_LIMIT_BYTES),
        name="moe_combine",
    )(off, x2, gates, rankm, fw.reshape(1, D).astype(F32), ys)


def _moe(x2, nw, router, wg, wu, wd, fw, *, win=MOE_WINDOW, tm=MOE_TILE, ff_chunk=1408):
    T, D = x2.shape
    E = router.shape[-1]
    nwin = T // win
    hn, gates, rankm, dest, cnt = _route(x2, nw, router, win=win)

    cnt = cnt.reshape(nwin, LANES)[:, :E]
    cnt_pad = _round_up(cnt, BF16_ROWS)
    seg = jnp.cumsum(cnt_pad, axis=1) - cnt_pad
    total = jnp.sum(cnt_pad, axis=0)
    ntile = (total + tm - 1) // tm
    region = _round_up(total + win, tm)
    r_end = jnp.cumsum(region)
    r_start = r_end - region
    off = r_start[None, :] + jnp.cumsum(cnt_pad, axis=0) - cnt_pad
    max_rows = TOP_K * T + nwin * E * (BF16_ROWS - 1)
    ntiles_max = (max_rows + E * (win + tm - 1)) // tm
    cap = ntiles_max * tm
    t_end = r_end // tm
    ti = jnp.arange(ntiles_max)
    tile_e = jnp.minimum(jnp.sum(ti[:, None] >= t_end[None, :], axis=1), E - 1).astype(jnp.int32)
    tile_v = (ti < (r_start // tm + ntile)[tile_e]).astype(jnp.int32)
    zero_slabs = -(-(win + tm - 1) // win)
    fill = jnp.minimum((r_start + total)[None, :] + win * jnp.arange(zero_slabs)[:, None],
                       r_end[None, :] - win)
    tail = jnp.stack([r_end[-1], (cap - r_end[-1]) // win])
    flat = lambda a: a.reshape(-1).astype(jnp.int32)

    xs = _compact(flat(seg), flat(off), jnp.concatenate([flat(fill), flat(tail)]), hn, dest,
                  win=win, n_experts=E, cap=cap, zero_slabs=zero_slabs)
    ys = _experts(tile_e, tile_v, xs, wg, wu, wd, tm=tm, ff_chunk=ff_chunk)
    return _combine(flat(off), x2, gates, rankm, fw, ys, win=win, n_experts=E)


def kernel(x, norm_mix, w_in, conv_w, conv_b, conv_ng, conv_nb, sgu_ng, sgu_nb, sgu_w, sgu_b, w_out,
           norm_ffn, ffn_wg, ffn_wu, ffn_wd, moe_router, moe_wg, moe_wu, moe_wd, norm_final):
    B, S, D = x.shape
    depth = norm_mix.shape[0]
    assert depth == 2, "trunk is one dense layer followed by one MoE layer"
    mix = functools.partial(_mixer, ts=256)
    x = mix(x, norm_mix[0], w_in[0], conv_w[0], conv_b[0], conv_ng[0], conv_nb[0],
            sgu_ng[0], sgu_nb[0], sgu_w[0], sgu_b[0], w_out[0])
    x2 = _ffn(x.reshape(B * S, D), norm_ffn[0], ffn_wg[0], ffn_wu[0], ffn_wd[0], tm=512, ff_chunk=1408)
    x = mix(x2.reshape(B, S, D), norm_mix[1], w_in[1], conv_w[1], conv_b[1], conv_ng[1], conv_nb[1],
            sgu_ng[1], sgu_nb[1], sgu_w[1], sgu_b[1], w_out[1])
    y2 = _moe(x.reshape(B * S, D), norm_ffn[1], moe_router[0], moe_wg[0], moe_wu[0], moe_wd[0],
              norm_final)
    return y2.reshape(B, S, D)
```

```python
import functools

import jax
import jax.numpy as jnp
from jax import lax
from jax.experimental import pallas as pl
from jax.experimental.pallas import tpu as pltpu

F32 = jnp.float32
BF16 = jnp.bfloat16

EPS = 1e-6
CONV_WIDTH = 31
CONV_GROUPS = 8
SGU_HEADS = 4
CHUNK = 128
TOP_K = 2

SUBLANES = 8
LANES = 128
BF16_ROWS = 16
CONV_HALO = 32
VMEM_LIMIT_BYTES = 56 * 1024 * 1024
_MIX_STAGES = 3
_MIX_PIECES = SUBLANES

MOE_WINDOW = 256
MOE_TILE = 512


def _rms_norm(x, g):
    ms = jnp.mean(x * x, axis=-1, keepdims=True)
    return x * lax.rsqrt(ms + EPS) * g


def _gelu(x):
    return 0.5 * x * (1.0 + lax.erf(x * (2.0 ** -0.5)))


def _silu(x):
    return x * jax.nn.sigmoid(x)


def _mixer_kernel(x_ref, nw_ref, win_ref, cw_ref, cb_ref, cng_ref, cnb_ref, sng_ref, snb_ref,
                  sw_ref, sb_ref, wout_ref, gavg_ref, o_ref, cbuf_ref, *, ts, sub, cc, sc):
    s = pl.program_id(1)

    @pl.when(s == 0)
    def _():
        cbuf_ref[0:CONV_HALO, :] = jnp.zeros((CONV_HALO, cc), F32)

    row = lax.broadcasted_iota(jnp.int32, (CHUNK, CHUNK), 0)
    col = lax.broadcasted_iota(jnp.int32, (CHUNK, CHUNK), 1)
    tril = row >= col
    hd = sc // SGU_HEADS
    nq = -(-CONV_WIDTH // SUBLANES)

    nsub = ts // sub
    d_in = 2 * cc + 2 * sc
    cbw = d_in // _MIX_PIECES

    def sub_tile(j):
        r0 = j * sub
        x = x_ref[0, r0:r0 + sub, :]
        h = _rms_norm(x, nw_ref[...]).astype(BF16)
        zblk = []
        for k in range(_MIX_PIECES):
            zblk.append(jnp.dot(h, win_ref[:, k * cbw:(k + 1) * cbw], preferred_element_type=F32))
            yield
        z = jnp.concatenate(zblk, axis=1)
        a = z[:, :cc]
        gate = z[:, cc:2 * cc]
        u = z[:, 2 * cc:2 * cc + sc]
        v = z[:, 2 * cc + sc:]

        base = CONV_HALO + r0
        cbuf_ref[base:base + sub, :] = a * jax.nn.sigmoid(gate)
        acc = jnp.broadcast_to(cb_ref[...], (sub, cc))
        for b in range(SUBLANES):
            zb = None
            for q in range(nq):
                d = SUBLANES * q + b
                if d >= CONV_WIDTH:
                    continue
                start = base - SUBLANES * (q + 1)
                term = (cw_ref[CONV_WIDTH - 1 - d:CONV_WIDTH - d, :]
                        * cbuf_ref[start:start + sub + SUBLANES, :])
                zb = term if zb is None else zb + term
            acc = acc + zb[SUBLANES - b:SUBLANES - b + sub, :]
            yield

        gavg = gavg_ref[...]
        acc_hi = acc.astype(BF16)
        acc_lo = (acc - acc_hi.astype(F32)).astype(BF16)
        mu = (jnp.dot(acc_hi, gavg, preferred_element_type=F32)
              + jnp.dot(acc_lo, gavg, preferred_element_type=F32))
        yield
        dev = acc - mu
        var = jnp.dot((dev * dev).astype(BF16), gavg, preferred_element_type=F32)
        yield
        cn = dev * lax.rsqrt(var + EPS) * cng_ref[...] + cnb_ref[...]
        c_out = _silu(cn).astype(BF16)
        yield

        u = _gelu(u)
        v = _gelu(v)
        yield
        g_cols = []
        for hh in range(SGU_HEADS):
            vh = v[:, hh * hd:(hh + 1) * hd]
            mu_h = jnp.mean(vh, axis=-1, keepdims=True)
            dh = vh - mu_h
            var_h = jnp.mean(dh * dh, axis=-1, keepdims=True)
            vn = (dh * lax.rsqrt(var_h + EPS) * sng_ref[:, hh * hd:(hh + 1) * hd]
                  + snb_ref[:, hh * hd:(hh + 1) * hd]).astype(BF16)
            ws = jnp.where(tril, sw_ref[hh], 0.0).astype(BF16)
            rows = []
            for ci in range(sub // CHUNK):
                sp = jnp.dot(ws, vn[ci * CHUNK:(ci + 1) * CHUNK, :], preferred_element_type=F32)
                rows.append(sp + sb_ref[hh])
            sp_h = rows[0] if len(rows) == 1 else jnp.concatenate(rows, axis=0)
            g_cols.append((u[:, hh * hd:(hh + 1) * hd] * sp_h).astype(BF16))
            if hh % 2 == 1:
                yield
        g_out = jnp.concatenate(g_cols, axis=1)

        yc = jnp.dot(c_out, wout_ref[0:cc, :], preferred_element_type=F32)
        yield
        yg = jnp.dot(g_out, wout_ref[cc:cc + sc, :], preferred_element_type=F32)
        o_ref[0, r0:r0 + sub, :] = x + (yc + yg)
        yield

    gens = [sub_tile(j) for j in range(nsub)]
    for tick in range(_MIX_PIECES * (nsub + _MIX_STAGES - 1)):
        for j in reversed(range(nsub)):
            if 0 <= tick - _MIX_PIECES * j < _MIX_PIECES * _MIX_STAGES:
                next(gens[j])

    cbuf_ref[0:CONV_HALO, :] = cbuf_ref[ts:ts + CONV_HALO, :]


def _mixer(x, nw, w_in, conv_w, conv_b, conv_ng, conv_nb, sgu_ng, sgu_nb, sgu_w, sgu_b, w_out,
           *, ts, sub):
    B, S, D = x.shape
    cc = conv_w.shape[-1]
    sc = sgu_ng.shape[-1]
    d_in = w_in.shape[-1]
    assert S % ts == 0 and ts % sub == 0 and sub % CHUNK == 0 and d_in == 2 * cc + 2 * sc
    gs = cc // CONV_GROUPS
    gid = jnp.arange(cc) // gs
    gavg = jnp.where(gid[:, None] == gid[None, :], 1.0 / gs, 0.0).astype(BF16)
    sb_full = jnp.broadcast_to(sgu_b[:, :, None], (SGU_HEADS, CHUNK, sc // SGU_HEADS)).astype(F32)
    row = lambda p: p.reshape(1, -1).astype(F32)
    const2 = lambda shape: pl.BlockSpec(shape, lambda b, s: (0, 0))
    const3 = lambda shape: pl.BlockSpec(shape, lambda b, s: (0, 0, 0))
    kern = functools.partial(_mixer_kernel, ts=ts, sub=sub, cc=cc, sc=sc)
    return pl.pallas_call(
        kern,
        grid=(B, S // ts),
        in_specs=[
            pl.BlockSpec((1, ts, D), lambda b, s: (b, s, 0)),
            const2((1, D)),
            const2((D, d_in)),
            const2((CONV_WIDTH, cc)),
            const2((1, cc)), const2((1, cc)), const2((1, cc)),
            const2((1, sc)), const2((1, sc)),
            const3((SGU_HEADS, CHUNK, CHUNK)),
            const3((SGU_HEADS, CHUNK, sc // SGU_HEADS)),
            const2((cc + sc, D)),
            const2((cc, cc)),
        ],
        out_specs=pl.BlockSpec((1, ts, D), lambda b, s: (b, s, 0)),
        out_shape=jax.ShapeDtypeStruct((B, S, D), F32),
        scratch_shapes=[pltpu.VMEM((CONV_HALO + ts, cc), F32)],
        compiler_params=pltpu.CompilerParams(
            dimension_semantics=("arbitrary", "arbitrary"),
            vmem_limit_bytes=VMEM_LIMIT_BYTES),
        name="mixer",
    )(x, row(nw), w_in.astype(BF16), conv_w.reshape(CONV_WIDTH, cc).astype(F32), row(conv_b),
      row(conv_ng), row(conv_nb), row(sgu_ng), row(sgu_nb), sgu_w.astype(F32), sb_full,
      w_out.astype(BF16), gavg)


def _swiglu_tile(h, wg_ref, wu_ref, wd_ref, ff_chunk):
    ff = wg_ref.shape[-1]
    out = None
    for c0 in range(0, ff, ff_chunk):
        c1 = min(ff, c0 + ff_chunk)
        g = jnp.dot(h, wg_ref[:, c0:c1], preferred_element_type=F32)
        u = jnp.dot(h, wu_ref[:, c0:c1], preferred_element_type=F32)
        a = (_silu(g) * u).astype(BF16)
        y = jnp.dot(a, wd_ref[c0:c1, :], preferred_element_type=F32)
        out = y if out is None else out + y
    return out


def _ffn_kernel(x_ref, nw_ref, wg_ref, wu_ref, wd_ref, o_ref, *, ff_chunk):
    x = x_ref[...]
    h = _rms_norm(x, nw_ref[...]).astype(BF16)
    o_ref[...] = x + _swiglu_tile(h, wg_ref, wu_ref, wd_ref, ff_chunk)


def _ffn(x2, nw, wg, wu, wd, *, tm, ff_chunk):
    T, D = x2.shape
    FF = wg.shape[-1]
    assert T % tm == 0
    kern = functools.partial(_ffn_kernel, ff_chunk=ff_chunk)
    return pl.pallas_call(
        kern,
        grid=(T // tm,),
        in_specs=[
            pl.BlockSpec((tm, D), lambda i: (i, 0)),
            pl.BlockSpec((1, D), lambda i: (0, 0)),
            pl.BlockSpec((D, FF), lambda i: (0, 0)),
            pl.BlockSpec((D, FF), lambda i: (0, 0)),
            pl.BlockSpec((FF, D), lambda i: (0, 0)),
        ],
        out_specs=pl.BlockSpec((tm, D), lambda i: (i, 0)),
        out_shape=jax.ShapeDtypeStruct((T, D), F32),
        compiler_params=pltpu.CompilerParams(
            dimension_semantics=("arbitrary",), vmem_limit_bytes=VMEM_LIMIT_BYTES),
        name="ffn",
    )(x2, nw.reshape(1, D).astype(F32), wg.astype(BF16), wu.astype(BF16), wd.astype(BF16))


def _round_up(n, m):
    return (n + m - 1) // m * m


def _route_kernel(x_ref, nw_ref, rt_ref, ltri_ref, utri_ref,
                  hn_ref, gates_ref, rankm_ref, dest_ref, cnt_ref, *, n_experts):
    x = x_ref[...]
    h = _rms_norm(x, nw_ref[...]).astype(BF16)
    hn_ref[...] = h
    logits = jnp.dot(h, rt_ref[...], preferred_element_type=F32)
    lane = lax.broadcasted_iota(jnp.int32, logits.shape, 1)
    logits = jnp.where(lane < n_experts, logits, -jnp.inf)
    m1 = jnp.max(logits, axis=-1, keepdims=True)
    i1 = jnp.min(jnp.where(logits == m1, lane, LANES), axis=-1, keepdims=True)
    rest = jnp.where(lane == i1, -jnp.inf, logits)
    m2 = jnp.max(rest, axis=-1, keepdims=True)
    i2 = jnp.min(jnp.where(rest == m2, lane, LANES), axis=-1, keepdims=True)
    e2 = jnp.exp(m2 - m1)
    den = 1.0 + e2
    sel1 = lane == i1
    sel2 = lane == i2
    gates_ref[...] = jnp.where(sel1, 1.0 / den, 0.0) + jnp.where(sel2, e2 / den, 0.0)

    sel = jnp.logical_or(sel1, sel2)
    self = sel.astype(F32)
    rank = jnp.dot(ltri_ref[...], self.astype(BF16), preferred_element_type=F32)
    rankm_ref[...] = jnp.where(sel, rank, -1.0)
    cnt = jnp.sum(self, axis=0, keepdims=True)
    cnt_pad = jnp.floor((cnt + (BF16_ROWS - 1.0)) * (1.0 / BF16_ROWS)) * BF16_ROWS
    seg = jnp.dot(jnp.broadcast_to(cnt_pad, (SUBLANES, LANES)).astype(BF16), utri_ref[...],
                  preferred_element_type=F32)[0:1, :]
    pos = seg + rank
    dest1 = jnp.sum(jnp.where(sel1, pos, 0.0), axis=-1, keepdims=True)
    dest2 = jnp.sum(jnp.where(sel2, pos, 0.0), axis=-1, keepdims=True)
    dest = jnp.where(lane == 0, dest1, jnp.where(lane == 1, dest2, -1.0))
    dest_ref[0] = jnp.transpose(dest)[0:SUBLANES, :]
    cnt_ref[0] = cnt.astype(jnp.int32)


def _route(x2, nw, router, *, win):
    T, D = x2.shape
    E = router.shape[-1]
    assert T % win == 0 and E <= SUBLANES
    nwin = T // win
    rt = jnp.zeros((D, LANES), F32).at[:, :E].set(router).astype(BF16)
    r = jnp.arange(win)
    ltri = (r[:, None] > r[None, :]).astype(BF16)
    l = jnp.arange(LANES)
    utri = (l[:, None] < l[None, :]).astype(BF16)
    kern = functools.partial(_route_kernel, n_experts=E)
    return pl.pallas_call(
        kern,
        grid=(nwin,),
        in_specs=[
            pl.BlockSpec((win, D), lambda w: (w, 0)),
            pl.BlockSpec((1, D), lambda w: (0, 0)),
            pl.BlockSpec((D, LANES), lambda w: (0, 0)),
            pl.BlockSpec((win, win), lambda w: (0, 0)),
            pl.BlockSpec((LANES, LANES), lambda w: (0, 0)),
        ],
        out_specs=[
            pl.BlockSpec((win, D), lambda w: (w, 0)),
            pl.BlockSpec((win, LANES), lambda w: (w, 0)),
            pl.BlockSpec((win, LANES), lambda w: (w, 0)),
            pl.BlockSpec((1, SUBLANES, win), lambda w: (w, 0, 0)),
            pl.BlockSpec((1, 1, LANES), lambda w: (w, 0, 0)),
        ],
        out_shape=[
            jax.ShapeDtypeStruct((T, D), BF16),
            jax.ShapeDtypeStruct((T, LANES), F32),
            jax.ShapeDtypeStruct((T, LANES), F32),
            jax.ShapeDtypeStruct((nwin, SUBLANES, win), F32),
            jax.ShapeDtypeStruct((nwin, 1, LANES), jnp.int32),
        ],
        compiler_params=pltpu.CompilerParams(
            dimension_semantics=("arbitrary",), vmem_limit_bytes=VMEM_LIMIT_BYTES),
        name="moe_route",
    )(x2, nw.reshape(1, D).astype(F32), rt, ltri, utri)


def _compact_kernel(seg_ref, off_ref, fill_ref, hn_ref, dest_ref, xs_ref, stage_ref, sem,
                    *, win, n_experts, rows, zero_slabs):
    w = pl.program_id(0)
    nwin = pl.num_programs(0)
    slot = w % 2

    def slab_copy(ww, e, sl):
        src = pl.multiple_of(seg_ref[ww * n_experts + e], BF16_ROWS)
        dst = pl.multiple_of(off_ref[ww * n_experts + e], BF16_ROWS)
        return pltpu.make_async_copy(stage_ref.at[sl, pl.ds(src, win)],
                                     xs_ref.at[pl.ds(dst, win)], sem.at[sl, e])

    def zero_copy(dst, e):
        return pltpu.make_async_copy(stage_ref.at[slot, pl.ds(rows, win)],
                                     xs_ref.at[pl.ds(pl.multiple_of(dst, BF16_ROWS), win)],
                                     sem.at[2, e])

    @pl.when(w == 0)
    def _():
        zeros = jnp.zeros((win, stage_ref.shape[-1]), BF16)
        stage_ref[0, rows:rows + win, :] = zeros
        stage_ref[1, rows:rows + win, :] = zeros

    d1 = dest_ref[0, 0:1, :]
    d2 = dest_ref[0, 1:2, :]
    r = lax.broadcasted_iota(jnp.int32, (rows, win), 0).astype(F32)
    onehot = jnp.logical_or(r == d1, r == d2).astype(BF16)
    xg = jnp.dot(onehot, hn_ref[...], preferred_element_type=F32)
    stage_ref[slot, 0:rows, :] = xg.astype(BF16)

    @pl.when(w > 0)
    def _():
        for e in range(n_experts):
            slab_copy(w - 1, e, 1 - slot).wait()

    for e in range(n_experts):
        slab_copy(w, e, slot).start()

    @pl.when(w == nwin - 1)
    def _():
        for e in range(n_experts):
            slab_copy(w, e, slot).wait()
        for k in range(zero_slabs):
            for e in range(n_experts):
                zero_copy(fill_ref[k * n_experts + e], e).start()
            for e in range(n_experts):
                zero_copy(fill_ref[k * n_experts + e], e).wait()
        tail0 = fill_ref[zero_slabs * n_experts]
        ntail = fill_ref[zero_slabs * n_experts + 1]

        def tail_start(j, c):
            zero_copy(tail0 + j * win, 0).start()
            return c

        def tail_wait(j, c):
            zero_copy(tail0 + j * win, 0).wait()
            return c

        lax.fori_loop(0, ntail, tail_start, 0)
        lax.fori_loop(0, ntail, tail_wait, 0)


def _compact(seg, off, fill, hn, dest, *, win, n_experts, cap, zero_slabs):
    T, D = hn.shape
    nwin = T // win
    rows = _round_up(TOP_K * win + n_experts * (BF16_ROWS - 1), BF16_ROWS)
    kern = functools.partial(_compact_kernel, win=win, n_experts=n_experts, rows=rows,
                             zero_slabs=zero_slabs)
    return pl.pallas_call(
        kern,
        grid_spec=pltpu.PrefetchScalarGridSpec(
            num_scalar_prefetch=3,
            grid=(nwin,),
            in_specs=[
                pl.BlockSpec((win, D), lambda w, *_: (w, 0)),
                pl.BlockSpec((1, SUBLANES, win), lambda w, *_: (w, 0, 0)),
            ],
            out_specs=pl.BlockSpec(memory_space=pl.ANY),
            scratch_shapes=[pltpu.VMEM((2, rows + win, D), BF16),
                            pltpu.SemaphoreType.DMA((3, n_experts))],
        ),
        out_shape=jax.ShapeDtypeStruct((cap, D), BF16),
        compiler_params=pltpu.CompilerParams(
            dimension_semantics=("arbitrary",), vmem_limit_bytes=VMEM_LIMIT_BYTES),
        name="moe_compact",
    )(seg, off, fill, hn, dest)


def _expert_kernel(te_ref, tv_ref, xs_ref, wg_ref, wu_ref, wd_ref, ys_ref, *, ff_chunk):
    i = pl.program_id(0)

    @pl.when(tv_ref[i] > 0)
    def _():
        y = _swiglu_tile(xs_ref[...], wg_ref.at[0], wu_ref.at[0], wd_ref.at[0], ff_chunk)
        ys_ref[...] = y.astype(BF16)

    @pl.when(tv_ref[i] == 0)
    def _():
        ys_ref[...] = jnp.zeros(ys_ref.shape, BF16)


def _experts(tile_e, tile_v, xs, wg, wu, wd, *, tm, ff_chunk):
    cap, D = xs.shape
    FF = wg.shape[-1]
    assert cap == tile_e.shape[0] * tm
    kern = functools.partial(_expert_kernel, ff_chunk=ff_chunk)
    return pl.pallas_call(
        kern,
        grid_spec=pltpu.PrefetchScalarGridSpec(
            num_scalar_prefetch=2,
            grid=(cap // tm,),
            in_specs=[
                pl.BlockSpec((tm, D), lambda i, te, tv: (i, 0)),
                pl.BlockSpec((1, D, FF), lambda i, te, tv: (te[i], 0, 0)),
                pl.BlockSpec((1, D, FF), lambda i, te, tv: (te[i], 0, 0)),
                pl.BlockSpec((1, FF, D), lambda i, te, tv: (te[i], 0, 0)),
            ],
            out_specs=pl.BlockSpec((tm, D), lambda i, te, tv: (i, 0)),
        ),
        out_shape=jax.ShapeDtypeStruct((cap, D), BF16),
        compiler_params=pltpu.CompilerParams(
            dimension_semantics=("arbitrary",), vmem_limit_bytes=VMEM_LIMIT_BYTES),
        name="moe_experts",
    )(tile_e, tile_v, xs, wg.astype(BF16), wu.astype(BF16), wd.astype(BF16))


def _combine_kernel(off_ref, x_ref, gates_ref, rankm_ref, fw_ref, ys_ref, o_ref,
                    slab_ref, sem, *, win, n_experts):
    w = pl.program_id(0)
    nwin = pl.num_programs(0)
    slot = w % 2

    def slab_copy(ww, e, sl):
        src = pl.multiple_of(off_ref[ww * n_experts + e], BF16_ROWS)
        return pltpu.make_async_copy(ys_ref.at[pl.ds(src, win)], slab_ref.at[sl, e], sem.at[sl, e])

    @pl.when(w == 0)
    def _():
        for e in range(n_experts):
            slab_copy(0, e, 0).start()

    @pl.when(w + 1 < nwin)
    def _():
        for e in range(n_experts):
            slab_copy(w + 1, e, 1 - slot).start()

    for e in range(n_experts):
        slab_copy(w, e, slot).wait()

    acc = x_ref[...]
    gates = gates_ref[...]
    rankm = rankm_ref[...]
    lane = lax.broadcasted_iota(jnp.int32, (win, win), 1).astype(F32)
    for e in range(n_experts):
        rk = rankm[:, e:e + 1]
        onehot = (lane == rk).astype(BF16)
        y = jnp.dot(onehot, slab_ref[slot, e], preferred_element_type=F32)
        acc = acc + gates[:, e:e + 1] * y
    o_ref[...] = _rms_norm(acc, fw_ref[...])


def _combine(off, x2, gates, rankm, fw, ys, *, win, n_experts):
    T, D = x2.shape
    E = n_experts
    kern = functools.partial(_combine_kernel, win=win, n_experts=E)
    return pl.pallas_call(
        kern,
        grid_spec=pltpu.PrefetchScalarGridSpec(
            num_scalar_prefetch=1,
            grid=(T // win,),
            in_specs=[
                pl.BlockSpec((win, D), lambda w, *_: (w, 0)),
                pl.BlockSpec((win, LANES), lambda w, *_: (w, 0)),
                pl.BlockSpec((win, LANES), lambda w, *_: (w, 0)),
                pl.BlockSpec((1, D), lambda w, *_: (0, 0)),
                pl.BlockSpec(memory_space=pl.ANY),
            ],
            out_specs=pl.BlockSpec((win, D), lambda w, *_: (w, 0)),
            scratch_shapes=[pltpu.VMEM((2, E, win, D), BF16), pltpu.SemaphoreType.DMA((2, E))],
        ),
        out_shape=jax.ShapeDtypeStruct((T, D), F32),
        compiler_params=pltpu.CompilerParams(
            dimension_semantics=("arbitrary",), vmem_limit_bytes=VMEM_LIMIT_BYTES),
        name="moe_combine",
    )(off, x2, gates, rankm, fw.reshape(1, D).astype(F32), ys)


def _moe(x2, nw, router, wg, wu, wd, fw, *, win=MOE_WINDOW, tm=MOE_TILE, ff_chunk=1408):
    T, D = x2.shape
    E = router.shape[-1]
    nwin = T // win
    hn, gates, rankm, dest, cnt = _route(x2, nw, router, win=win)

    cnt = cnt.reshape(nwin, LANES)[:, :E]
    cnt_pad = _round_up(cnt, BF16_ROWS)
    seg = jnp.cumsum(cnt_pad, axis=1) - cnt_pad
    total = jnp.sum(cnt_pad, axis=0)
    ntile = (total + tm - 1) // tm
    region = _round_up(total + win, tm)
    r_end = jnp.cumsum(region)
    r_start = r_end - region
    off = r_start[None, :] + jnp.cumsum(cnt_pad, axis=0) - cnt_pad
    max_rows = TOP_K * T + nwin * E * (BF16_ROWS - 1)
    ntiles_max = (max_rows + E * (win + tm - 1)) // tm
    cap = ntiles_max * tm
    t_end = r_end // tm
    ti = jnp.arange(ntiles_max)
    tile_e = jnp.minimum(jnp.sum(ti[:, None] >= t_end[None, :], axis=1), E - 1).astype(jnp.int32)
    tile_v = (ti < (r_start // tm + ntile)[tile_e]).astype(jnp.int32)
    zero_slabs = -(-(win + tm - 1) // win)
    fill = jnp.minimum((r_start + total)[None, :] + win * jnp.arange(zero_slabs)[:, None],
                       r_end[None, :] - win)
    tail = jnp.stack([r_end[-1], (cap - r_end[-1]) // win])
    flat = lambda a: a.reshape(-1).astype(jnp.int32)

    xs = _compact(flat(seg), flat(off), jnp.concatenate([flat(fill), flat(tail)]), hn, dest,
                  win=win, n_experts=E, cap=cap, zero_slabs=zero_slabs)
    ys = _experts(tile_e, tile_v, xs, wg, wu, wd, tm=tm, ff_chunk=ff_chunk)
    return _combine(flat(off), x2, gates, rankm, fw, ys, win=win, n_experts=E)


def kernel(x, norm_mix, w_in, conv_w, conv_b, conv_ng, conv_nb, sgu_ng, sgu_nb, sgu_w, sgu_b, w_out,
           norm_ffn, ffn_wg, ffn_wu, ffn_wd, moe_router, moe_wg, moe_wu, moe_wd, norm_final):
    B, S, D = x.shape
    depth = norm_mix.shape[0]
    assert depth == 2, "trunk is one dense layer followed by one MoE layer"
    mix = functools.partial(_mixer, ts=512, sub=256)
    x = mix(x, norm_mix[0], w_in[0], conv_w[0], conv_b[0], conv_ng[0], conv_nb[0],
            sgu_ng[0], sgu_nb[0], sgu_w[0], sgu_b[0], w_out[0])
    x2 = _ffn(x.reshape(B * S, D), norm_ffn[0], ffn_wg[0], ffn_wu[0], ffn_wd[0], tm=512, ff_chunk=1408)
    x = mix(x2.reshape(B, S, D), norm_mix[1], w_in[1], conv_w[1], conv_b[1], conv_ng[1], conv_nb[1],
            sgu_ng[1], sgu_nb[1], sgu_w[1], sgu_b[1], w_out[1])
    y2 = _moe(x.reshape(B * S, D), norm_ffn[1], moe_router[0], moe_wg[0], moe_wu[0], moe_wd[0],
              norm_final)
    return y2.reshape(B, S, D)
```

```python
import functools

import jax
import jax.numpy as jnp
from jax import lax
from jax.experimental import pallas as pl
from jax.experimental.pallas import tpu as pltpu

F32 = jnp.float32
BF16 = jnp.bfloat16

EPS = 1e-6
CONV_WIDTH = 31
CONV_GROUPS = 8
SGU_HEADS = 4
CHUNK = 128
TOP_K = 2

SUBLANES = 8
LANES = 128
BF16_ROWS = 16
CONV_HALO = 32
VMEM_LIMIT_BYTES = 56 * 1024 * 1024
_MIX_STAGES = 3
_MIX_PIECES = SUBLANES

MOE_WINDOW = 512
MOE_TILE = 512


def _rms_norm(x, g):
    ms = jnp.mean(x * x, axis=-1, keepdims=True)
    return x * lax.rsqrt(ms + EPS) * g


def _gelu(x):
    return 0.5 * x * (1.0 + lax.erf(x * (2.0 ** -0.5)))


def _silu(x):
    return x * jax.nn.sigmoid(x)


def _mixer_kernel(x_ref, nw_ref, win_ref, cw_ref, cb_ref, cng_ref, cnb_ref, sng_ref, snb_ref,
                  sw_ref, sb_ref, wout_ref, gavg_ref, o_ref, cbuf_ref, *, ts, sub, cc, sc):
    s = pl.program_id(1)

    @pl.when(s == 0)
    def _():
        cbuf_ref[0:CONV_HALO, :] = jnp.zeros((CONV_HALO, cc), F32)

    row = lax.broadcasted_iota(jnp.int32, (CHUNK, CHUNK), 0)
    col = lax.broadcasted_iota(jnp.int32, (CHUNK, CHUNK), 1)
    tril = row >= col
    hd = sc // SGU_HEADS
    nq = -(-CONV_WIDTH // SUBLANES)

    nsub = ts // sub
    d_in = 2 * cc + 2 * sc
    cbw = d_in // _MIX_PIECES

    def sub_tile(j):
        r0 = j * sub
        x = x_ref[0, r0:r0 + sub, :]
        h = _rms_norm(x, nw_ref[...]).astype(BF16)
        zblk = []
        for k in range(_MIX_PIECES):
            zblk.append(jnp.dot(h, win_ref[:, k * cbw:(k + 1) * cbw], preferred_element_type=F32))
            yield
        z = jnp.concatenate(zblk, axis=1)
        a = z[:, :cc]
        gate = z[:, cc:2 * cc]
        u = z[:, 2 * cc:2 * cc + sc]
        v = z[:, 2 * cc + sc:]

        base = CONV_HALO + r0
        cbuf_ref[base:base + sub, :] = a * jax.nn.sigmoid(gate)
        acc = jnp.broadcast_to(cb_ref[...], (sub, cc))
        for b in range(SUBLANES):
            zb = None
            for q in range(nq):
                d = SUBLANES * q + b
                if d >= CONV_WIDTH:
                    continue
                start = base - SUBLANES * (q + 1)
                term = (cw_ref[CONV_WIDTH - 1 - d:CONV_WIDTH - d, :]
                        * cbuf_ref[start:start + sub + SUBLANES, :])
                zb = term if zb is None else zb + term
            acc = acc + zb[SUBLANES - b:SUBLANES - b + sub, :]
            yield

        gavg = gavg_ref[...]
        acc_hi = acc.astype(BF16)
        acc_lo = (acc - acc_hi.astype(F32)).astype(BF16)
        mu = (jnp.dot(acc_hi, gavg, preferred_element_type=F32)
              + jnp.dot(acc_lo, gavg, preferred_element_type=F32))
        yield
        dev = acc - mu
        var = jnp.dot((dev * dev).astype(BF16), gavg, preferred_element_type=F32)
        yield
        cn = dev * lax.rsqrt(var + EPS) * cng_ref[...] + cnb_ref[...]
        c_out = _silu(cn).astype(BF16)
        yield

        u = _gelu(u)
        v = _gelu(v)
        yield
        g_cols = []
        for hh in range(SGU_HEADS):
            vh = v[:, hh * hd:(hh + 1) * hd]
            mu_h = jnp.mean(vh, axis=-1, keepdims=True)
            dh = vh - mu_h
            var_h = jnp.mean(dh * dh, axis=-1, keepdims=True)
            vn = (dh * lax.rsqrt(var_h + EPS) * sng_ref[:, hh * hd:(hh + 1) * hd]
                  + snb_ref[:, hh * hd:(hh + 1) * hd]).astype(BF16)
            ws = jnp.where(tril, sw_ref[hh], 0.0).astype(BF16)
            rows = []
            for ci in range(sub // CHUNK):
                sp = jnp.dot(ws, vn[ci * CHUNK:(ci + 1) * CHUNK, :], preferred_element_type=F32)
                rows.append(sp + sb_ref[hh])
            sp_h = rows[0] if len(rows) == 1 else jnp.concatenate(rows, axis=0)
            g_cols.append((u[:, hh * hd:(hh + 1) * hd] * sp_h).astype(BF16))
            if hh % 2 == 1:
                yield
        g_out = jnp.concatenate(g_cols, axis=1)

        yc = jnp.dot(c_out, wout_ref[0:cc, :], preferred_element_type=F32)
        yield
        yg = jnp.dot(g_out, wout_ref[cc:cc + sc, :], preferred_element_type=F32)
        o_ref[0, r0:r0 + sub, :] = x + (yc + yg)
        yield

    gens = [sub_tile(j) for j in range(nsub)]
    for tick in range(_MIX_PIECES * (nsub + _MIX_STAGES - 1)):
        for j in reversed(range(nsub)):
            if 0 <= tick - _MIX_PIECES * j < _MIX_PIECES * _MIX_STAGES:
                next(gens[j])

    cbuf_ref[0:CONV_HALO, :] = cbuf_ref[ts:ts + CONV_HALO, :]


def _mixer(x, nw, w_in, conv_w, conv_b, conv_ng, conv_nb, sgu_ng, sgu_nb, sgu_w, sgu_b, w_out,
           *, ts, sub):
    B, S, D = x.shape
    cc = conv_w.shape[-1]
    sc = sgu_ng.shape[-1]
    d_in = w_in.shape[-1]
    assert S % ts == 0 and ts % sub == 0 and sub % CHUNK == 0 and d_in == 2 * cc + 2 * sc
    gs = cc // CONV_GROUPS
    gid = jnp.arange(cc) // gs
    gavg = jnp.where(gid[:, None] == gid[None, :], 1.0 / gs, 0.0).astype(BF16)
    sb_full = jnp.broadcast_to(sgu_b[:, :, None], (SGU_HEADS, CHUNK, sc // SGU_HEADS)).astype(F32)
    row = lambda p: p.reshape(1, -1).astype(F32)
    const2 = lambda shape: pl.BlockSpec(shape, lambda b, s: (0, 0))
    const3 = lambda shape: pl.BlockSpec(shape, lambda b, s: (0, 0, 0))
    kern = functools.partial(_mixer_kernel, ts=ts, sub=sub, cc=cc, sc=sc)
    return pl.pallas_call(
        kern,
        grid=(B, S // ts),
        in_specs=[
            pl.BlockSpec((1, ts, D), lambda b, s: (b, s, 0)),
            const2((1, D)),
            const2((D, d_in)),
            const2((CONV_WIDTH, cc)),
            const2((1, cc)), const2((1, cc)), const2((1, cc)),
            const2((1, sc)), const2((1, sc)),
            const3((SGU_HEADS, CHUNK, CHUNK)),
            const3((SGU_HEADS, CHUNK, sc // SGU_HEADS)),
            const2((cc + sc, D)),
            const2((cc, cc)),
        ],
        out_specs=pl.BlockSpec((1, ts, D), lambda b, s: (b, s, 0)),
        out_shape=jax.ShapeDtypeStruct((B, S, D), F32),
        scratch_shapes=[pltpu.VMEM((CONV_HALO + ts, cc), F32)],
        compiler_params=pltpu.CompilerParams(
            dimension_semantics=("arbitrary", "arbitrary"),
            vmem_limit_bytes=VMEM_LIMIT_BYTES),
        name="mixer",
    )(x, row(nw), w_in.astype(BF16), conv_w.reshape(CONV_WIDTH, cc).astype(F32), row(conv_b),
      row(conv_ng), row(conv_nb), row(sgu_ng), row(sgu_nb), sgu_w.astype(F32), sb_full,
      w_out.astype(BF16), gavg)


def _swiglu_tile(h, wg_ref, wu_ref, wd_ref, ff_chunk):
    ff = wg_ref.shape[-1]
    out = None
    for c0 in range(0, ff, ff_chunk):
        c1 = min(ff, c0 + ff_chunk)
        g = jnp.dot(h, wg_ref[:, c0:c1], preferred_element_type=F32)
        u = jnp.dot(h, wu_ref[:, c0:c1], preferred_element_type=F32)
        a = (_silu(g) * u).astype(BF16)
        y = jnp.dot(a, wd_ref[c0:c1, :], preferred_element_type=F32)
        out = y if out is None else out + y
    return out


def _ffn_kernel(x_ref, nw_ref, wg_ref, wu_ref, wd_ref, o_ref, *, ff_chunk):
    x = x_ref[...]
    h = _rms_norm(x, nw_ref[...]).astype(BF16)
    o_ref[...] = x + _swiglu_tile(h, wg_ref, wu_ref, wd_ref, ff_chunk)


def _ffn(x2, nw, wg, wu, wd, *, tm, ff_chunk):
    T, D = x2.shape
    FF = wg.shape[-1]
    assert T % tm == 0
    kern = functools.partial(_ffn_kernel, ff_chunk=ff_chunk)
    return pl.pallas_call(
        kern,
        grid=(T // tm,),
        in_specs=[
            pl.BlockSpec((tm, D), lambda i: (i, 0)),
            pl.BlockSpec((1, D), lambda i: (0, 0)),
            pl.BlockSpec((D, FF), lambda i: (0, 0)),
            pl.BlockSpec((D, FF), lambda i: (0, 0)),
            pl.BlockSpec((FF, D), lambda i: (0, 0)),
        ],
        out_specs=pl.BlockSpec((tm, D), lambda i: (i, 0)),
        out_shape=jax.ShapeDtypeStruct((T, D), F32),
        compiler_params=pltpu.CompilerParams(
            dimension_semantics=("arbitrary",), vmem_limit_bytes=VMEM_LIMIT_BYTES),
        name="ffn",
    )(x2, nw.reshape(1, D).astype(F32), wg.astype(BF16), wu.astype(BF16), wd.astype(BF16))


def _round_up(n, m):
    return (n + m - 1) // m * m


def _piece_sizes(largest):
    sizes = []
    s = largest
    while s >= BF16_ROWS:
        sizes.append(s)
        s //= 2
    assert sizes and sizes[-1] == BF16_ROWS
    return sizes


def _for_each_piece(count, sizes, fn):
    for s in sizes:
        offset = jnp.bitwise_and(count, -2 * s)

        @pl.when(jnp.bitwise_and(count, s) != 0)
        def _():
            fn(offset, s)


def _route_kernel(x_ref, nw_ref, rt_ref, ltri_ref, utri_ref,
                  hn_ref, dcol_ref, drow_ref, cnt_ref, *, n_experts):
    x = x_ref[...]
    h = _rms_norm(x, nw_ref[...]).astype(BF16)
    hn_ref[...] = h
    logits = jnp.dot(h, rt_ref[...], preferred_element_type=F32)
    lane = lax.broadcasted_iota(jnp.int32, logits.shape, 1)
    logits = jnp.where(lane < n_experts, logits, -jnp.inf)
    m1 = jnp.max(logits, axis=-1, keepdims=True)
    i1 = jnp.min(jnp.where(logits == m1, lane, LANES), axis=-1, keepdims=True)
    rest = jnp.where(lane == i1, -jnp.inf, logits)
    m2 = jnp.max(rest, axis=-1, keepdims=True)
    i2 = jnp.min(jnp.where(rest == m2, lane, LANES), axis=-1, keepdims=True)
    e2 = jnp.exp(m2 - m1)
    den = 1.0 + e2
    g1 = 1.0 / den
    g2 = e2 / den
    sel1 = lane == i1
    sel2 = lane == i2
    self = jnp.logical_or(sel1, sel2).astype(F32)
    rank = jnp.dot(ltri_ref[...], self.astype(BF16), preferred_element_type=F32)
    cnt = jnp.sum(self, axis=0, keepdims=True)
    cnt_pad = jnp.floor((cnt + (BF16_ROWS - 1.0)) * (1.0 / BF16_ROWS)) * BF16_ROWS
    seg = jnp.dot(jnp.broadcast_to(cnt_pad, (SUBLANES, LANES)).astype(BF16), utri_ref[...],
                  preferred_element_type=F32)[0:1, :]
    pos = seg + rank
    dest1 = jnp.sum(jnp.where(sel1, pos, 0.0), axis=-1, keepdims=True)
    dest2 = jnp.sum(jnp.where(sel2, pos, 0.0), axis=-1, keepdims=True)
    info = jnp.where(lane == 0, dest1, jnp.where(lane == 1, dest2,
                     jnp.where(lane == 2, g1, jnp.where(lane == 3, g2, -1.0))))
    dcol_ref[...] = info
    drow_ref[0] = jnp.transpose(info)[0:SUBLANES, :]
    cnt_ref[0] = cnt.astype(jnp.int32)


def _route(x2, nw, router, *, win):
    T, D = x2.shape
    E = router.shape[-1]
    assert T % win == 0 and E <= SUBLANES
    nwin = T // win
    rt = jnp.zeros((D, LANES), F32).at[:, :E].set(router).astype(BF16)
    r = jnp.arange(win)
    ltri = (r[:, None] > r[None, :]).astype(BF16)
    l = jnp.arange(LANES)
    utri = (l[:, None] < l[None, :]).astype(BF16)
    kern = functools.partial(_route_kernel, n_experts=E)
    return pl.pallas_call(
        kern,
        grid=(nwin,),
        in_specs=[
            pl.BlockSpec((win, D), lambda w: (w, 0)),
            pl.BlockSpec((1, D), lambda w: (0, 0)),
            pl.BlockSpec((D, LANES), lambda w: (0, 0)),
            pl.BlockSpec((win, win), lambda w: (0, 0)),
            pl.BlockSpec((LANES, LANES), lambda w: (0, 0)),
        ],
        out_specs=[
            pl.BlockSpec((win, D), lambda w: (w, 0)),
            pl.BlockSpec((win, LANES), lambda w: (w, 0)),
            pl.BlockSpec((1, SUBLANES, win), lambda w: (w, 0, 0)),
            pl.BlockSpec((1, 1, LANES), lambda w: (w, 0, 0)),
        ],
        out_shape=[
            jax.ShapeDtypeStruct((T, D), BF16),
            jax.ShapeDtypeStruct((T, LANES), F32),
            jax.ShapeDtypeStruct((nwin, SUBLANES, win), F32),
            jax.ShapeDtypeStruct((nwin, 1, LANES), jnp.int32),
        ],
        compiler_params=pltpu.CompilerParams(
            dimension_semantics=("arbitrary",), vmem_limit_bytes=VMEM_LIMIT_BYTES),
        name="moe_route",
    )(x2, nw.reshape(1, D).astype(F32), rt, ltri, utri)


def _compact_kernel(seg_ref, off_ref, cnt_ref, fill_ref, hn_ref, drow_ref, xs_ref,
                    stage_ref, zero_ref, sem, *, win, n_experts, rows, tm):
    w = pl.program_id(0)
    nwin = pl.num_programs(0)
    slot = w % 2
    d = hn_ref.shape[-1]
    sizes = _piece_sizes(win)

    def segments(ww, sl, start):
        for e in range(n_experts):
            src0 = seg_ref[ww * n_experts + e]
            dst0 = off_ref[ww * n_experts + e]

            def piece(offset, size):
                cp = pltpu.make_async_copy(
                    stage_ref.at[sl, pl.ds(pl.multiple_of(src0 + offset, BF16_ROWS), size)],
                    xs_ref.at[pl.ds(pl.multiple_of(dst0 + offset, BF16_ROWS), size)],
                    sem.at[sl, e])
                cp.start() if start else cp.wait()

            _for_each_piece(cnt_ref[ww * n_experts + e], sizes, piece)

    def zero_rows(dst, size, e, start):
        cp = pltpu.make_async_copy(zero_ref.at[pl.ds(0, size)],
                                   xs_ref.at[pl.ds(pl.multiple_of(dst, BF16_ROWS), size)],
                                   sem.at[2, e])
        cp.start() if start else cp.wait()

    @pl.when(w == 0)
    def _():
        zero_ref[...] = jnp.zeros(zero_ref.shape, BF16)

    d1 = drow_ref[0, 0:1, :]
    d2 = drow_ref[0, 1:2, :]
    g1 = drow_ref[0, 2:3, :]
    g2 = drow_ref[0, 3:4, :]
    r = lax.broadcasted_iota(jnp.int32, (rows, win), 0).astype(F32)
    m1 = r == d1
    m2 = r == d2
    onehot = jnp.logical_or(m1, m2).astype(BF16)
    xg = jnp.dot(onehot, hn_ref[...], preferred_element_type=F32)
    stage_ref[slot, :, 0:d] = xg.astype(BF16)
    gate = jnp.sum(jnp.where(m1, g1, 0.0) + jnp.where(m2, g2, 0.0), axis=-1, keepdims=True)
    hi = gate.astype(BF16).astype(F32)
    mid = (gate - hi).astype(BF16).astype(F32)
    lo = gate - hi - mid
    lane = lax.broadcasted_iota(jnp.int32, (rows, LANES), 1)
    terms = jnp.where(lane == 0, hi, jnp.where(lane == 1, mid, jnp.where(lane == 2, lo, 0.0)))
    stage_ref[slot, :, d:d + LANES] = terms.astype(BF16)

    segments(w, slot, True)

    @pl.when(w > 0)
    def _():
        segments(w - 1, 1 - slot, False)

    @pl.when(w == nwin - 1)
    def _():
        segments(w, slot, False)
        gap_sizes = _piece_sizes(tm // 2)
        for start in (True, False):
            for e in range(n_experts):
                _for_each_piece(fill_ref[n_experts + e], gap_sizes,
                                lambda offset, size, e=e: zero_rows(fill_ref[e] + offset, size, e, start))
        tail0 = fill_ref[2 * n_experts]
        ntail = fill_ref[2 * n_experts + 1]
        tail_rows = zero_ref.shape[0]

        def tail_start(j, c):
            zero_rows(tail0 + j * tail_rows, tail_rows, 0, True)
            return c

        def tail_wait(j, c):
            zero_rows(tail0 + j * tail_rows, tail_rows, 0, False)
            return c

        lax.fori_loop(0, ntail, tail_start, 0)
        lax.fori_loop(0, ntail, tail_wait, 0)


def _compact(seg, off, cnt_pad, fill, hn, drow, *, win, n_experts, cap, tm):
    T, D = hn.shape
    nwin = T // win
    rows = _round_up(TOP_K * win + n_experts * (BF16_ROWS - 1), BF16_ROWS)
    kern = functools.partial(_compact_kernel, win=win, n_experts=n_experts, rows=rows, tm=tm)
    return pl.pallas_call(
        kern,
        grid_spec=pltpu.PrefetchScalarGridSpec(
            num_scalar_prefetch=4,
            grid=(nwin,),
            in_specs=[
                pl.BlockSpec((win, D), lambda w, *_: (w, 0)),
                pl.BlockSpec((1, SUBLANES, win), lambda w, *_: (w, 0, 0)),
            ],
            out_specs=pl.BlockSpec(memory_space=pl.ANY),
            scratch_shapes=[pltpu.VMEM((2, rows, D + LANES), BF16),
                            pltpu.VMEM((tm // 2, D + LANES), BF16),
                            pltpu.SemaphoreType.DMA((3, n_experts))],
        ),
        out_shape=jax.ShapeDtypeStruct((cap, D + LANES), BF16),
        compiler_params=pltpu.CompilerParams(
            dimension_semantics=("arbitrary",), vmem_limit_bytes=VMEM_LIMIT_BYTES),
        name="moe_compact",
    )(seg, off, cnt_pad, fill, hn, drow)


def _expert_kernel(te_ref, tv_ref, xs_ref, wg_ref, wu_ref, wd_ref, ys_ref, *, ff_chunk):
    i = pl.program_id(0)
    d = ys_ref.shape[-1]

    @pl.when(tv_ref[i] > 0)
    def _():
        y = _swiglu_tile(xs_ref[:, 0:d], wg_ref.at[0], wu_ref.at[0], wd_ref.at[0], ff_chunk)
        gate = jnp.sum(xs_ref[:, d:d + LANES].astype(F32), axis=-1, keepdims=True)
        ys_ref[...] = (gate * y).astype(BF16)

    @pl.when(tv_ref[i] == 0)
    def _():
        ys_ref[...] = jnp.zeros(ys_ref.shape, BF16)


def _experts(tile_e, tile_v, xs, wg, wu, wd, *, tm, ff_chunk):
    cap = xs.shape[0]
    _, D, FF = wg.shape
    assert cap == tile_e.shape[0] * tm and xs.shape[1] == D + LANES
    kern = functools.partial(_expert_kernel, ff_chunk=ff_chunk)
    return pl.pallas_call(
        kern,
        grid_spec=pltpu.PrefetchScalarGridSpec(
            num_scalar_prefetch=2,
            grid=(cap // tm,),
            in_specs=[
                pl.BlockSpec((tm, D + LANES), lambda i, te, tv: (i, 0)),
                pl.BlockSpec((1, D, FF), lambda i, te, tv: (te[i], 0, 0)),
                pl.BlockSpec((1, D, FF), lambda i, te, tv: (te[i], 0, 0)),
                pl.BlockSpec((1, FF, D), lambda i, te, tv: (te[i], 0, 0)),
            ],
            out_specs=pl.BlockSpec((tm, D), lambda i, te, tv: (i, 0)),
        ),
        out_shape=jax.ShapeDtypeStruct((cap, D), BF16),
        compiler_params=pltpu.CompilerParams(
            dimension_semantics=("arbitrary",), vmem_limit_bytes=VMEM_LIMIT_BYTES),
        name="moe_experts",
    )(tile_e, tile_v, xs, wg.astype(BF16), wu.astype(BF16), wd.astype(BF16))


def _combine_kernel(seg_ref, off_ref, cnt_ref, x_ref, dcol_ref, fw_ref, ys_ref, o_ref,
                    stage_ref, sem, *, win, n_experts):
    w = pl.program_id(0)
    nwin = pl.num_programs(0)
    slot = w % 2
    rows = stage_ref.shape[1]
    sizes = _piece_sizes(win)

    def segments(ww, sl, start):
        for e in range(n_experts):
            src0 = off_ref[ww * n_experts + e]
            dst0 = seg_ref[ww * n_experts + e]

            def piece(offset, size):
                cp = pltpu.make_async_copy(
                    ys_ref.at[pl.ds(pl.multiple_of(src0 + offset, BF16_ROWS), size)],
                    stage_ref.at[sl, pl.ds(pl.multiple_of(dst0 + offset, BF16_ROWS), size)],
                    sem.at[sl, e])
                cp.start() if start else cp.wait()

            _for_each_piece(cnt_ref[ww * n_experts + e], sizes, piece)

    @pl.when(w == 0)
    def _():
        stage_ref[...] = jnp.zeros(stage_ref.shape, BF16)
        segments(0, 0, True)

    @pl.when(w + 1 < nwin)
    def _():
        segments(w + 1, 1 - slot, True)

    segments(w, slot, False)

    d1 = dcol_ref[:, 0:1]
    d2 = dcol_ref[:, 1:2]
    r = lax.broadcasted_iota(jnp.int32, (win, rows), 1).astype(F32)
    onehot = jnp.logical_or(r == d1, r == d2).astype(BF16)
    y = jnp.dot(onehot, stage_ref[slot], preferred_element_type=F32)
    o_ref[...] = _rms_norm(x_ref[...] + y, fw_ref[...])


def _combine(seg, off, cnt_pad, x2, dcol, fw, ys, *, win, n_experts):
    T, D = x2.shape
    rows = _round_up(TOP_K * win + n_experts * (BF16_ROWS - 1), BF16_ROWS)
    kern = functools.partial(_combine_kernel, win=win, n_experts=n_experts)
    return pl.pallas_call(
        kern,
        grid_spec=pltpu.PrefetchScalarGridSpec(
            num_scalar_prefetch=3,
            grid=(T // win,),
            in_specs=[
                pl.BlockSpec((win, D), lambda w, *_: (w, 0)),
                pl.BlockSpec((win, LANES), lambda w, *_: (w, 0)),
                pl.BlockSpec((1, D), lambda w, *_: (0, 0)),
                pl.BlockSpec(memory_space=pl.ANY),
            ],
            out_specs=pl.BlockSpec((win, D), lambda w, *_: (w, 0)),
            scratch_shapes=[pltpu.VMEM((2, rows, D), BF16),
                            pltpu.SemaphoreType.DMA((2, n_experts))],
        ),
        out_shape=jax.ShapeDtypeStruct((T, D), F32),
        compiler_params=pltpu.CompilerParams(
            dimension_semantics=("arbitrary",), vmem_limit_bytes=VMEM_LIMIT_BYTES),
        name="moe_combine",
    )(seg, off, cnt_pad, x2, dcol, fw.reshape(1, D).astype(F32), ys)


def _moe(x2, nw, router, wg, wu, wd, fw, *, win=MOE_WINDOW, tm=MOE_TILE, ff_chunk=2816):
    T, D = x2.shape
    E = router.shape[-1]
    nwin = T // win
    assert win <= tm and tm % win == 0
    hn, dcol, drow, cnt = _route(x2, nw, router, win=win)

    cnt = cnt.reshape(nwin, LANES)[:, :E]
    cnt_pad = _round_up(cnt, BF16_ROWS)
    seg = jnp.cumsum(cnt_pad, axis=1) - cnt_pad
    total = jnp.sum(cnt_pad, axis=0)
    region = _round_up(total, tm)
    r_end = jnp.cumsum(region)
    r_start = r_end - region
    off = r_start[None, :] + jnp.cumsum(cnt_pad, axis=0) - cnt_pad
    max_rows = TOP_K * T + nwin * E * (BF16_ROWS - 1)
    ntiles_max = max_rows // tm + E
    cap = ntiles_max * tm
    ti = jnp.arange(ntiles_max)
    t_end = r_end // tm
    tile_e = jnp.minimum(jnp.sum(ti[:, None] >= t_end[None, :], axis=1), E - 1).astype(jnp.int32)
    tile_v = (ti < t_end[-1]).astype(jnp.int32)
    tail = jnp.stack([r_end[-1], (cap - r_end[-1]) // (tm // 2)])
    fill = jnp.concatenate([r_start + total, region - total, tail])
    flat = lambda a: a.reshape(-1).astype(jnp.int32)
    seg, off, cnt_pad, fill = flat(seg), flat(off), flat(cnt_pad), flat(fill)

    xs = _compact(seg, off, cnt_pad, fill, hn, drow, win=win, n_experts=E, cap=cap, tm=tm)
    ys = _experts(tile_e, tile_v, xs, wg, wu, wd, tm=tm, ff_chunk=ff_chunk)
    return _combine(seg, off, cnt_pad, x2, dcol, fw, ys, win=win, n_experts=E)


def kernel(x, norm_mix, w_in, conv_w, conv_b, conv_ng, conv_nb, sgu_ng, sgu_nb, sgu_w, sgu_b, w_out,
           norm_ffn, ffn_wg, ffn_wu, ffn_wd, moe_router, moe_wg, moe_wu, moe_wd, norm_final):
    B, S, D = x.shape
    depth = norm_mix.shape[0]
    assert depth == 2, "trunk is one dense layer followed by one MoE layer"
    mix = functools.partial(_mixer, ts=512, sub=256)
    x = mix(x, norm_mix[0], w_in[0], conv_w[0], conv_b[0], conv_ng[0], conv_nb[0],
            sgu_ng[0], sgu_nb[0], sgu_w[0], sgu_b[0], w_out[0])
    x2 = _ffn(x.reshape(B * S, D), norm_ffn[0], ffn_wg[0], ffn_wu[0], ffn_wd[0], tm=512, ff_chunk=2816)
    x = mix(x2.reshape(B, S, D), norm_mix[1], w_in[1], conv_w[1], conv_b[1], conv_ng[1], conv_nb[1],
            sgu_ng[1], sgu_nb[1], sgu_w[1], sgu_b[1], w_out[1])
    y2 = _moe(x.reshape(B * S, D), norm_ffn[1], moe_router[0], moe_wg[0], moe_wu[0], moe_wd[0],
              norm_final)
    return y2.reshape(B, S, D)
```

```python
import functools

import jax
import jax.numpy as jnp
from jax import lax
from jax.experimental import pallas as pl
from jax.experimental.pallas import tpu as pltpu

F32 = jnp.float32
BF16 = jnp.bfloat16

EPS = 1e-6
CONV_WIDTH = 31
CONV_GROUPS = 8
SGU_HEADS = 4
CHUNK = 128
TOP_K = 2

SUBLANES = 8
LANES = 128
BF16_ROWS = 16
CONV_HALO = 32
VMEM_LIMIT_BYTES = 56 * 1024 * 1024
_MIX_STAGES = 3
_MIX_PIECES = SUBLANES

MOE_WINDOW = 512
MOE_TILE = 512


def _rms_norm(x, g):
    ms = jnp.mean(x * x, axis=-1, keepdims=True)
    return x * lax.rsqrt(ms + EPS) * g


def _gelu(x):
    return 0.5 * x * (1.0 + lax.erf(x * (2.0 ** -0.5)))


def _silu(x):
    return x * jax.nn.sigmoid(x)


def _cast_plan(arr, nsteps):
    rows = arr.shape[0]
    nblk = nsteps
    while rows % nblk or (rows // nblk) % BF16_ROWS:
        nblk //= 2
        assert nblk >= 1, arr.shape
    return nblk, rows // nblk


def _cast_specs(arrs, nsteps, step_of):
    specs, shapes = [], []
    for a in arrs:
        nblk, blk = _cast_plan(a, nsteps)
        index = lambda *g, nblk=nblk: (jnp.minimum(step_of(*g), nblk - 1), 0)
        specs.append(pl.BlockSpec((blk, a.shape[1]), index))
        shapes.append(jax.ShapeDtypeStruct(a.shape, BF16))
    return specs, shapes


def _cast_blocks(src_refs, dst_refs):
    for src, dst in zip(src_refs, dst_refs):
        dst[...] = src[...].astype(BF16)


_MIXER_INPUTS = 13


def _mixer_kernel(*refs, ts, sub, cc, sc, n_cast):
    (x_ref, nw_ref, win_ref, cw_ref, cb_ref, cng_ref, cnb_ref, sng_ref, snb_ref,
     sw_ref, sb_ref, wout_ref, gavg_ref) = refs[:_MIXER_INPUTS]
    cast_in = refs[_MIXER_INPUTS:_MIXER_INPUTS + n_cast]
    o_ref = refs[_MIXER_INPUTS + n_cast]
    cast_out = refs[_MIXER_INPUTS + n_cast + 1:_MIXER_INPUTS + 2 * n_cast + 1]
    cbuf_ref = refs[_MIXER_INPUTS + 2 * n_cast + 1]
    s = pl.program_id(1)
    _cast_blocks(cast_in, cast_out)

    @pl.when(s == 0)
    def _():
        cbuf_ref[0:CONV_HALO, :] = jnp.zeros((CONV_HALO, cc), F32)

    row = lax.broadcasted_iota(jnp.int32, (CHUNK, CHUNK), 0)
    col = lax.broadcasted_iota(jnp.int32, (CHUNK, CHUNK), 1)
    tril = row >= col
    hd = sc // SGU_HEADS
    nq = -(-CONV_WIDTH // SUBLANES)

    nsub = ts // sub
    d_in = 2 * cc + 2 * sc
    cbw = d_in // _MIX_PIECES

    def sub_tile(j):
        r0 = j * sub
        x = x_ref[0, r0:r0 + sub, :]
        h = _rms_norm(x, nw_ref[...]).astype(BF16)
        zblk = []
        for k in range(_MIX_PIECES):
            zblk.append(jnp.dot(h, win_ref[:, k * cbw:(k + 1) * cbw], preferred_element_type=F32))
            yield
        z = jnp.concatenate(zblk, axis=1)
        a = z[:, :cc]
        gate = z[:, cc:2 * cc]
        u = z[:, 2 * cc:2 * cc + sc]
        v = z[:, 2 * cc + sc:]

        base = CONV_HALO + r0
        cbuf_ref[base:base + sub, :] = a * jax.nn.sigmoid(gate)
        acc = jnp.broadcast_to(cb_ref[...], (sub, cc))
        for b in range(SUBLANES):
            zb = None
            for q in range(nq):
                d = SUBLANES * q + b
                if d >= CONV_WIDTH:
                    continue
                start = base - SUBLANES * (q + 1)
                term = (cw_ref[CONV_WIDTH - 1 - d:CONV_WIDTH - d, :]
                        * cbuf_ref[start:start + sub + SUBLANES, :])
                zb = term if zb is None else zb + term
            acc = acc + zb[SUBLANES - b:SUBLANES - b + sub, :]
            yield

        gavg = gavg_ref[...]
        acc_hi = acc.astype(BF16)
        acc_lo = (acc - acc_hi.astype(F32)).astype(BF16)
        mu = (jnp.dot(acc_hi, gavg, preferred_element_type=F32)
              + jnp.dot(acc_lo, gavg, preferred_element_type=F32))
        yield
        dev = acc - mu
        var = jnp.dot((dev * dev).astype(BF16), gavg, preferred_element_type=F32)
        yield
        cn = dev * lax.rsqrt(var + EPS) * cng_ref[...] + cnb_ref[...]
        c_out = _silu(cn).astype(BF16)
        yield

        u = _gelu(u)
        v = _gelu(v)
        yield
        g_cols = []
        for hh in range(SGU_HEADS):
            vh = v[:, hh * hd:(hh + 1) * hd]
            mu_h = jnp.mean(vh, axis=-1, keepdims=True)
            dh = vh - mu_h
            var_h = jnp.mean(dh * dh, axis=-1, keepdims=True)
            vn = (dh * lax.rsqrt(var_h + EPS) * sng_ref[:, hh * hd:(hh + 1) * hd]
                  + snb_ref[:, hh * hd:(hh + 1) * hd]).astype(BF16)
            ws = jnp.where(tril, sw_ref[hh], 0.0).astype(BF16)
            rows = []
            for ci in range(sub // CHUNK):
                sp = jnp.dot(ws, vn[ci * CHUNK:(ci + 1) * CHUNK, :], preferred_element_type=F32)
                rows.append(sp + sb_ref[hh])
            sp_h = rows[0] if len(rows) == 1 else jnp.concatenate(rows, axis=0)
            g_cols.append((u[:, hh * hd:(hh + 1) * hd] * sp_h).astype(BF16))
            if hh % 2 == 1:
                yield
        g_out = jnp.concatenate(g_cols, axis=1)

        yc = jnp.dot(c_out, wout_ref[0:cc, :], preferred_element_type=F32)
        yield
        yg = jnp.dot(g_out, wout_ref[cc:cc + sc, :], preferred_element_type=F32)
        o_ref[0, r0:r0 + sub, :] = x + (yc + yg)
        yield

    gens = [sub_tile(j) for j in range(nsub)]
    for tick in range(_MIX_PIECES * (nsub + _MIX_STAGES - 1)):
        for j in reversed(range(nsub)):
            if 0 <= tick - _MIX_PIECES * j < _MIX_PIECES * _MIX_STAGES:
                next(gens[j])

    cbuf_ref[0:CONV_HALO, :] = cbuf_ref[ts:ts + CONV_HALO, :]


def _mixer(x, nw, w_in, conv_w, conv_b, conv_ng, conv_nb, sgu_ng, sgu_nb, sgu_w, sgu_b, w_out,
           *, ts, sub, cast=()):
    B, S, D = x.shape
    cc = conv_w.shape[-1]
    sc = sgu_ng.shape[-1]
    d_in = w_in.shape[-1]
    assert S % ts == 0 and ts % sub == 0 and sub % CHUNK == 0 and d_in == 2 * cc + 2 * sc
    ns = S // ts
    gs = cc // CONV_GROUPS
    gid = jnp.arange(cc) // gs
    gavg = jnp.where(gid[:, None] == gid[None, :], 1.0 / gs, 0.0).astype(BF16)
    sb_full = jnp.broadcast_to(sgu_b[:, :, None], (SGU_HEADS, CHUNK, sc // SGU_HEADS)).astype(F32)
    row = lambda p: p.reshape(1, -1).astype(F32)
    const2 = lambda shape: pl.BlockSpec(shape, lambda b, s: (0, 0))
    const3 = lambda shape: pl.BlockSpec(shape, lambda b, s: (0, 0, 0))
    cast_specs, cast_shapes = _cast_specs(cast, B * ns, lambda b, s: b * ns + s)
    kern = functools.partial(_mixer_kernel, ts=ts, sub=sub, cc=cc, sc=sc, n_cast=len(cast))
    in_specs = [
        pl.BlockSpec((1, ts, D), lambda b, s: (b, s, 0)),
        const2((1, D)),
        const2((D, d_in)),
        const2((CONV_WIDTH, cc)),
        const2((1, cc)), const2((1, cc)), const2((1, cc)),
        const2((1, sc)), const2((1, sc)),
        const3((SGU_HEADS, CHUNK, CHUNK)),
        const3((SGU_HEADS, CHUNK, sc // SGU_HEADS)),
        const2((cc + sc, D)),
        const2((cc, cc)),
    ]
    assert len(in_specs) == _MIXER_INPUTS
    return pl.pallas_call(
        kern,
        grid=(B, ns),
        in_specs=in_specs + cast_specs,
        out_specs=[pl.BlockSpec((1, ts, D), lambda b, s: (b, s, 0))] + cast_specs,
        out_shape=[jax.ShapeDtypeStruct((B, S, D), F32)] + cast_shapes,
        scratch_shapes=[pltpu.VMEM((CONV_HALO + ts, cc), F32)],
        compiler_params=pltpu.CompilerParams(
            dimension_semantics=("arbitrary", "arbitrary"),
            vmem_limit_bytes=VMEM_LIMIT_BYTES),
        name="mixer",
    )(x, row(nw), w_in.astype(BF16), conv_w.reshape(CONV_WIDTH, cc).astype(F32), row(conv_b),
      row(conv_ng), row(conv_nb), row(sgu_ng), row(sgu_nb), sgu_w.astype(F32), sb_full,
      w_out.astype(BF16), gavg, *cast)


def _swiglu_tile(h, wg_ref, wu_ref, wd_ref, ff_chunk):
    ff = wg_ref.shape[-1]
    out = None
    for c0 in range(0, ff, ff_chunk):
        c1 = min(ff, c0 + ff_chunk)
        g = jnp.dot(h, wg_ref[:, c0:c1], preferred_element_type=F32)
        u = jnp.dot(h, wu_ref[:, c0:c1], preferred_element_type=F32)
        a = (_silu(g) * u).astype(BF16)
        y = jnp.dot(a, wd_ref[c0:c1, :], preferred_element_type=F32)
        out = y if out is None else out + y
    return out


_FFN_INPUTS = 5


def _ffn_kernel(*refs, ff_chunk, n_cast):
    x_ref, nw_ref, wg_ref, wu_ref, wd_ref = refs[:_FFN_INPUTS]
    cast_in = refs[_FFN_INPUTS:_FFN_INPUTS + n_cast]
    o_ref = refs[_FFN_INPUTS + n_cast]
    cast_out = refs[_FFN_INPUTS + n_cast + 1:]
    _cast_blocks(cast_in, cast_out)
    x = x_ref[...]
    h = _rms_norm(x, nw_ref[...]).astype(BF16)
    o_ref[...] = x + _swiglu_tile(h, wg_ref, wu_ref, wd_ref, ff_chunk)


def _ffn(x2, nw, wg, wu, wd, *, tm, ff_chunk, cast=()):
    T, D = x2.shape
    FF = wg.shape[-1]
    assert T % tm == 0
    cast_specs, cast_shapes = _cast_specs(cast, T // tm, lambda i: i)
    kern = functools.partial(_ffn_kernel, ff_chunk=ff_chunk, n_cast=len(cast))
    in_specs = [
        pl.BlockSpec((tm, D), lambda i: (i, 0)),
        pl.BlockSpec((1, D), lambda i: (0, 0)),
        pl.BlockSpec((D, FF), lambda i: (0, 0)),
        pl.BlockSpec((D, FF), lambda i: (0, 0)),
        pl.BlockSpec((FF, D), lambda i: (0, 0)),
    ]
    assert len(in_specs) == _FFN_INPUTS
    return pl.pallas_call(
        kern,
        grid=(T // tm,),
        in_specs=in_specs + cast_specs,
        out_specs=[pl.BlockSpec((tm, D), lambda i: (i, 0))] + cast_specs,
        out_shape=[jax.ShapeDtypeStruct((T, D), F32)] + cast_shapes,
        compiler_params=pltpu.CompilerParams(
            dimension_semantics=("arbitrary",), vmem_limit_bytes=VMEM_LIMIT_BYTES),
        name="ffn",
    )(x2, nw.reshape(1, D).astype(F32), wg.astype(BF16), wu.astype(BF16), wd.astype(BF16), *cast)


def _round_up(n, m):
    return (n + m - 1) // m * m


def _piece_sizes(largest):
    sizes = []
    s = largest
    while s >= BF16_ROWS:
        sizes.append(s)
        s //= 2
    assert sizes and sizes[-1] == BF16_ROWS
    return sizes


def _for_each_piece(count, sizes, fn):
    for s in sizes:
        offset = jnp.bitwise_and(count, -2 * s)

        @pl.when(jnp.bitwise_and(count, s) != 0)
        def _():
            fn(offset, s)


def _route_kernel(x_ref, nw_ref, rt_ref, ltri_ref, utri_ref,
                  hn_ref, dcol_ref, drow_ref, cnt_ref, *, n_experts):
    x = x_ref[...]
    h = _rms_norm(x, nw_ref[...]).astype(BF16)
    hn_ref[...] = h
    logits = jnp.dot(h, rt_ref[...], preferred_element_type=F32)
    lane = lax.broadcasted_iota(jnp.int32, logits.shape, 1)
    logits = jnp.where(lane < n_experts, logits, -jnp.inf)
    m1 = jnp.max(logits, axis=-1, keepdims=True)
    i1 = jnp.min(jnp.where(logits == m1, lane, LANES), axis=-1, keepdims=True)
    rest = jnp.where(lane == i1, -jnp.inf, logits)
    m2 = jnp.max(rest, axis=-1, keepdims=True)
    i2 = jnp.min(jnp.where(rest == m2, lane, LANES), axis=-1, keepdims=True)
    e2 = jnp.exp(m2 - m1)
    den = 1.0 + e2
    g1 = 1.0 / den
    g2 = e2 / den
    sel1 = lane == i1
    sel2 = lane == i2
    self = jnp.logical_or(sel1, sel2).astype(F32)
    rank = jnp.dot(ltri_ref[...], self.astype(BF16), preferred_element_type=F32)
    cnt = jnp.sum(self, axis=0, keepdims=True)
    cnt_pad = jnp.floor((cnt + (BF16_ROWS - 1.0)) * (1.0 / BF16_ROWS)) * BF16_ROWS
    seg = jnp.dot(jnp.broadcast_to(cnt_pad, (SUBLANES, LANES)).astype(BF16), utri_ref[...],
                  preferred_element_type=F32)[0:1, :]
    pos = seg + rank
    dest1 = jnp.sum(jnp.where(sel1, pos, 0.0), axis=-1, keepdims=True)
    dest2 = jnp.sum(jnp.where(sel2, pos, 0.0), axis=-1, keepdims=True)
    info = jnp.where(lane == 0, dest1, jnp.where(lane == 1, dest2,
                     jnp.where(lane == 2, g1, jnp.where(lane == 3, g2, -1.0))))
    dcol_ref[...] = info
    drow_ref[0] = jnp.transpose(info)[0:SUBLANES, :]
    cnt_ref[0] = cnt.astype(jnp.int32)


def _route(x2, nw, router, *, win):
    T, D = x2.shape
    E = router.shape[-1]
    assert T % win == 0 and E <= SUBLANES
    nwin = T // win
    rt = jnp.zeros((D, LANES), F32).at[:, :E].set(router).astype(BF16)
    r = jnp.arange(win)
    ltri = (r[:, None] > r[None, :]).astype(BF16)
    l = jnp.arange(LANES)
    utri = (l[:, None] < l[None, :]).astype(BF16)
    kern = functools.partial(_route_kernel, n_experts=E)
    return pl.pallas_call(
        kern,
        grid=(nwin,),
        in_specs=[
            pl.BlockSpec((win, D), lambda w: (w, 0)),
            pl.BlockSpec((1, D), lambda w: (0, 0)),
            pl.BlockSpec((D, LANES), lambda w: (0, 0)),
            pl.BlockSpec((win, win), lambda w: (0, 0)),
            pl.BlockSpec((LANES, LANES), lambda w: (0, 0)),
        ],
        out_specs=[
            pl.BlockSpec((win, D), lambda w: (w, 0)),
            pl.BlockSpec((win, LANES), lambda w: (w, 0)),
            pl.BlockSpec((1, SUBLANES, win), lambda w: (w, 0, 0)),
            pl.BlockSpec((1, 1, LANES), lambda w: (w, 0, 0)),
        ],
        out_shape=[
            jax.ShapeDtypeStruct((T, D), BF16),
            jax.ShapeDtypeStruct((T, LANES), F32),
            jax.ShapeDtypeStruct((nwin, SUBLANES, win), F32),
            jax.ShapeDtypeStruct((nwin, 1, LANES), jnp.int32),
        ],
        compiler_params=pltpu.CompilerParams(
            dimension_semantics=("arbitrary",), vmem_limit_bytes=VMEM_LIMIT_BYTES),
        name="moe_route",
    )(x2, nw.reshape(1, D).astype(F32), rt, ltri, utri)


def _compact_kernel(seg_ref, off_ref, cnt_ref, fill_ref, hn_ref, drow_ref, xs_ref,
                    stage_ref, zero_ref, sem, *, win, n_experts, rows, tm):
    w = pl.program_id(0)
    nwin = pl.num_programs(0)
    slot = w % 2
    d = hn_ref.shape[-1]
    sizes = _piece_sizes(win)

    def segments(ww, sl, start):
        for e in range(n_experts):
            src0 = seg_ref[ww * n_experts + e]
            dst0 = off_ref[ww * n_experts + e]

            def piece(offset, size):
                cp = pltpu.make_async_copy(
                    stage_ref.at[sl, pl.ds(pl.multiple_of(src0 + offset, BF16_ROWS), size)],
                    xs_ref.at[pl.ds(pl.multiple_of(dst0 + offset, BF16_ROWS), size)],
                    sem.at[sl, e])
                cp.start() if start else cp.wait()

            _for_each_piece(cnt_ref[ww * n_experts + e], sizes, piece)

    def zero_rows(dst, size, e, start):
        cp = pltpu.make_async_copy(zero_ref.at[pl.ds(0, size)],
                                   xs_ref.at[pl.ds(pl.multiple_of(dst, BF16_ROWS), size)],
                                   sem.at[2, e])
        cp.start() if start else cp.wait()

    @pl.when(w == 0)
    def _():
        zero_ref[...] = jnp.zeros(zero_ref.shape, BF16)

    d1 = drow_ref[0, 0:1, :]
    d2 = drow_ref[0, 1:2, :]
    g1 = drow_ref[0, 2:3, :]
    g2 = drow_ref[0, 3:4, :]
    r = lax.broadcasted_iota(jnp.int32, (rows, win), 0).astype(F32)
    m1 = r == d1
    m2 = r == d2
    onehot = jnp.logical_or(m1, m2).astype(BF16)
    xg = jnp.dot(onehot, hn_ref[...], preferred_element_type=F32)
    stage_ref[slot, :, 0:d] = xg.astype(BF16)
    gate = jnp.sum(jnp.where(m1, g1, 0.0) + jnp.where(m2, g2, 0.0), axis=-1, keepdims=True)
    hi = gate.astype(BF16).astype(F32)
    mid = (gate - hi).astype(BF16).astype(F32)
    lo = gate - hi - mid
    lane = lax.broadcasted_iota(jnp.int32, (rows, LANES), 1)
    terms = jnp.where(lane == 0, hi, jnp.where(lane == 1, mid, jnp.where(lane == 2, lo, 0.0)))
    stage_ref[slot, :, d:d + LANES] = terms.astype(BF16)

    segments(w, slot, True)

    @pl.when(w > 0)
    def _():
        segments(w - 1, 1 - slot, False)

    @pl.when(w == nwin - 1)
    def _():
        segments(w, slot, False)
        gap_sizes = _piece_sizes(tm // 2)
        for start in (True, False):
            for e in range(n_experts):
                _for_each_piece(fill_ref[n_experts + e], gap_sizes,
                                lambda offset, size, e=e: zero_rows(fill_ref[e] + offset, size, e, start))
        tail0 = fill_ref[2 * n_experts]
        ntail = fill_ref[2 * n_experts + 1]
        tail_rows = zero_ref.shape[0]

        def tail_start(j, c):
            zero_rows(tail0 + j * tail_rows, tail_rows, 0, True)
            return c

        def tail_wait(j, c):
            zero_rows(tail0 + j * tail_rows, tail_rows, 0, False)
            return c

        lax.fori_loop(0, ntail, tail_start, 0)
        lax.fori_loop(0, ntail, tail_wait, 0)


def _compact(seg, off, cnt_pad, fill, hn, drow, *, win, n_experts, cap, tm):
    T, D = hn.shape
    nwin = T // win
    rows = _round_up(TOP_K * win + n_experts * (BF16_ROWS - 1), BF16_ROWS)
    kern = functools.partial(_compact_kernel, win=win, n_experts=n_experts, rows=rows, tm=tm)
    return pl.pallas_call(
        kern,
        grid_spec=pltpu.PrefetchScalarGridSpec(
            num_scalar_prefetch=4,
            grid=(nwin,),
            in_specs=[
                pl.BlockSpec((win, D), lambda w, *_: (w, 0)),
                pl.BlockSpec((1, SUBLANES, win), lambda w, *_: (w, 0, 0)),
            ],
            out_specs=pl.BlockSpec(memory_space=pl.ANY),
            scratch_shapes=[pltpu.VMEM((2, rows, D + LANES), BF16),
                            pltpu.VMEM((tm // 2, D + LANES), BF16),
                            pltpu.SemaphoreType.DMA((3, n_experts))],
        ),
        out_shape=jax.ShapeDtypeStruct((cap, D + LANES), BF16),
        compiler_params=pltpu.CompilerParams(
            dimension_semantics=("arbitrary",), vmem_limit_bytes=VMEM_LIMIT_BYTES),
        name="moe_compact",
    )(seg, off, cnt_pad, fill, hn, drow)


def _expert_kernel(te_ref, tv_ref, xs_ref, wg_ref, wu_ref, wd_ref, ys_ref, *, ff_chunk):
    i = pl.program_id(0)
    d = ys_ref.shape[-1]

    @pl.when(tv_ref[i] > 0)
    def _():
        y = _swiglu_tile(xs_ref[:, 0:d], wg_ref.at[0], wu_ref.at[0], wd_ref.at[0], ff_chunk)
        gate = jnp.sum(xs_ref[:, d:d + LANES].astype(F32), axis=-1, keepdims=True)
        ys_ref[...] = (gate * y).astype(BF16)

    @pl.when(tv_ref[i] == 0)
    def _():
        ys_ref[...] = jnp.zeros(ys_ref.shape, BF16)


def _experts(tile_e, tile_v, xs, wg, wu, wd, *, tm, ff_chunk):
    cap = xs.shape[0]
    _, D, FF = wg.shape
    assert cap == tile_e.shape[0] * tm and xs.shape[1] == D + LANES
    kern = functools.partial(_expert_kernel, ff_chunk=ff_chunk)
    return pl.pallas_call(
        kern,
        grid_spec=pltpu.PrefetchScalarGridSpec(
            num_scalar_prefetch=2,
            grid=(cap // tm,),
            in_specs=[
                pl.BlockSpec((tm, D + LANES), lambda i, te, tv: (i, 0)),
                pl.BlockSpec((1, D, FF), lambda i, te, tv: (te[i], 0, 0)),
                pl.BlockSpec((1, D, FF), lambda i, te, tv: (te[i], 0, 0)),
                pl.BlockSpec((1, FF, D), lambda i, te, tv: (te[i], 0, 0)),
            ],
            out_specs=pl.BlockSpec((tm, D), lambda i, te, tv: (i, 0)),
        ),
        out_shape=jax.ShapeDtypeStruct((cap, D), BF16),
        compiler_params=pltpu.CompilerParams(
            dimension_semantics=("arbitrary",), vmem_limit_bytes=VMEM_LIMIT_BYTES),
        name="moe_experts",
    )(tile_e, tile_v, xs, wg.astype(BF16), wu.astype(BF16), wd.astype(BF16))


def _combine_kernel(seg_ref, off_ref, cnt_ref, x_ref, dcol_ref, fw_ref, ys_ref, o_ref,
                    stage_ref, sem, *, win, n_experts):
    w = pl.program_id(0)
    nwin = pl.num_programs(0)
    slot = w % 2
    rows = stage_ref.shape[1]
    sizes = _piece_sizes(win)

    def segments(ww, sl, start):
        for e in range(n_experts):
            src0 = off_ref[ww * n_experts + e]
            dst0 = seg_ref[ww * n_experts + e]

            def piece(offset, size):
                cp = pltpu.make_async_copy(
                    ys_ref.at[pl.ds(pl.multiple_of(src0 + offset, BF16_ROWS), size)],
                    stage_ref.at[sl, pl.ds(pl.multiple_of(dst0 + offset, BF16_ROWS), size)],
                    sem.at[sl, e])
                cp.start() if start else cp.wait()

            _for_each_piece(cnt_ref[ww * n_experts + e], sizes, piece)

    @pl.when(w == 0)
    def _():
        stage_ref[...] = jnp.zeros(stage_ref.shape, BF16)
        segments(0, 0, True)

    @pl.when(w + 1 < nwin)
    def _():
        segments(w + 1, 1 - slot, True)

    segments(w, slot, False)

    d1 = dcol_ref[:, 0:1]
    d2 = dcol_ref[:, 1:2]
    r = lax.broadcasted_iota(jnp.int32, (win, rows), 1).astype(F32)
    onehot = jnp.logical_or(r == d1, r == d2).astype(BF16)
    y = jnp.dot(onehot, stage_ref[slot], preferred_element_type=F32)
    o_ref[...] = _rms_norm(x_ref[...] + y, fw_ref[...])


def _combine(seg, off, cnt_pad, x2, dcol, fw, ys, *, win, n_experts):
    T, D = x2.shape
    rows = _round_up(TOP_K * win + n_experts * (BF16_ROWS - 1), BF16_ROWS)
    kern = functools.partial(_combine_kernel, win=win, n_experts=n_experts)
    return pl.pallas_call(
        kern,
        grid_spec=pltpu.PrefetchScalarGridSpec(
            num_scalar_prefetch=3,
            grid=(T // win,),
            in_specs=[
                pl.BlockSpec((win, D), lambda w, *_: (w, 0)),
                pl.BlockSpec((win, LANES), lambda w, *_: (w, 0)),
                pl.BlockSpec((1, D), lambda w, *_: (0, 0)),
                pl.BlockSpec(memory_space=pl.ANY),
            ],
            out_specs=pl.BlockSpec((win, D), lambda w, *_: (w, 0)),
            scratch_shapes=[pltpu.VMEM((2, rows, D), BF16),
                            pltpu.SemaphoreType.DMA((2, n_experts))],
        ),
        out_shape=jax.ShapeDtypeStruct((T, D), F32),
        compiler_params=pltpu.CompilerParams(
            dimension_semantics=("arbitrary",), vmem_limit_bytes=VMEM_LIMIT_BYTES),
        name="moe_combine",
    )(seg, off, cnt_pad, x2, dcol, fw.reshape(1, D).astype(F32), ys)


def _moe(x2, nw, router, wg, wu, wd, fw, *, win=MOE_WINDOW, tm=MOE_TILE, ff_chunk=2816):
    T, D = x2.shape
    E = router.shape[-1]
    nwin = T // win
    assert win <= tm and tm % win == 0
    hn, dcol, drow, cnt = _route(x2, nw, router, win=win)

    cnt = cnt.reshape(nwin, LANES)[:, :E]
    cnt_pad = _round_up(cnt, BF16_ROWS)
    seg = jnp.cumsum(cnt_pad, axis=1) - cnt_pad
    total = jnp.sum(cnt_pad, axis=0)
    region = _round_up(total, tm)
    r_end = jnp.cumsum(region)
    r_start = r_end - region
    off = r_start[None, :] + jnp.cumsum(cnt_pad, axis=0) - cnt_pad
    max_rows = TOP_K * T + nwin * E * (BF16_ROWS - 1)
    ntiles_max = max_rows // tm + E
    cap = ntiles_max * tm
    ti = jnp.arange(ntiles_max)
    t_end = r_end // tm
    tile_e = jnp.minimum(jnp.sum(ti[:, None] >= t_end[None, :], axis=1), E - 1).astype(jnp.int32)
    tile_v = (ti < t_end[-1]).astype(jnp.int32)
    tail = jnp.stack([r_end[-1], (cap - r_end[-1]) // (tm // 2)])
    fill = jnp.concatenate([r_start + total, region - total, tail])
    flat = lambda a: a.reshape(-1).astype(jnp.int32)
    seg, off, cnt_pad, fill = flat(seg), flat(off), flat(cnt_pad), flat(fill)

    xs = _compact(seg, off, cnt_pad, fill, hn, drow, win=win, n_experts=E, cap=cap, tm=tm)
    ys = _experts(tile_e, tile_v, xs, wg, wu, wd, tm=tm, ff_chunk=ff_chunk)
    return _combine(seg, off, cnt_pad, x2, dcol, fw, ys, win=win, n_experts=E)


def kernel(x, norm_mix, w_in, conv_w, conv_b, conv_ng, conv_nb, sgu_ng, sgu_nb, sgu_w, sgu_b, w_out,
           norm_ffn, ffn_wg, ffn_wu, ffn_wd, moe_router, moe_wg, moe_wu, moe_wd, norm_final):
    B, S, D = x.shape
    depth = norm_mix.shape[0]
    assert depth == 2, "trunk is one dense layer followed by one MoE layer"
    mix = functools.partial(_mixer, ts=512, sub=256)
    rows2d = lambda a: a.reshape(-1, a.shape[-1])
    x, ffn_wg_b, ffn_wu_b, ffn_wd_b, w_in1_b, w_out1_b, moe_wg_b = mix(
        x, norm_mix[0], w_in[0], conv_w[0], conv_b[0], conv_ng[0], conv_nb[0],
        sgu_ng[0], sgu_nb[0], sgu_w[0], sgu_b[0], w_out[0],
        cast=(ffn_wg[0], ffn_wu[0], ffn_wd[0], w_in[1], w_out[1], rows2d(moe_wg[0])))
    x2, moe_wu_b = _ffn(x.reshape(B * S, D), norm_ffn[0], ffn_wg_b, ffn_wu_b, ffn_wd_b,
                        tm=512, ff_chunk=2816, cast=(rows2d(moe_wu[0]),))
    x, moe_wd_b = mix(x2.reshape(B, S, D), norm_mix[1], w_in1_b, conv_w[1], conv_b[1], conv_ng[1],
                      conv_nb[1], sgu_ng[1], sgu_nb[1], sgu_w[1], sgu_b[1], w_out1_b,
                      cast=(rows2d(moe_wd[0]),))
    y2 = _moe(x.reshape(B * S, D), norm_ffn[1], moe_router[0], moe_wg_b.reshape(moe_wg[0].shape),
              moe_wu_b.reshape(moe_wu[0].shape), moe_wd_b.reshape(moe_wd[0].shape), norm_final)
    return y2.reshape(B, S, D)
```

```python
import functools

import jax
import jax.numpy as jnp
from jax import lax
from jax.experimental import pallas as pl
from jax.experimental.pallas import tpu as pltpu

F32 = jnp.float32
BF16 = jnp.bfloat16

EPS = 1e-6
CONV_WIDTH = 31
CONV_GROUPS = 8
SGU_HEADS = 4
CHUNK = 128
TOP_K = 2

SUBLANES = 8
LANES = 128
BF16_ROWS = 16
CONV_HALO = 32
VMEM_LIMIT_BYTES = 56 * 1024 * 1024
_MIX_STAGES = 3
_MIX_PIECES = SUBLANES

MOE_WINDOW = 512
MOE_TILE = 512


def _rms_norm(x, g):
    ms = jnp.mean(x * x, axis=-1, keepdims=True)
    return x * lax.rsqrt(ms + EPS) * g


def _gelu(x):
    return 0.5 * x * (1.0 + lax.erf(x * (2.0 ** -0.5)))


def _sigmoid(x):
    return 0.5 * jnp.tanh(0.5 * x) + 0.5


def _silu(x):
    h = 0.5 * x
    return h * jnp.tanh(h) + h


def _cast_plan(rows, nsteps):
    nblk = nsteps
    while rows % nblk or (rows // nblk) % BF16_ROWS:
        nblk //= 2
        assert nblk >= 1, rows
    return nblk, rows // nblk


def _cast_specs(items, nsteps, step_of):
    arrs, in_specs, out_specs, shapes = [], [], [], []
    for a, layer in items:
        a = a.reshape(a.shape[0], -1, a.shape[-1])
        nblk, blk = _cast_plan(a.shape[1], nsteps)
        blk_of = lambda *g, nblk=nblk: jnp.minimum(step_of(*g), nblk - 1)
        arrs.append(a)
        in_specs.append(pl.BlockSpec((None, blk, a.shape[2]),
                                     lambda *g, layer=layer, blk_of=blk_of: (layer, blk_of(*g), 0)))
        out_specs.append(pl.BlockSpec((blk, a.shape[2]), lambda *g, blk_of=blk_of: (blk_of(*g), 0)))
        shapes.append(jax.ShapeDtypeStruct(a.shape[1:], BF16))
    return arrs, in_specs, out_specs, shapes


def _cast_blocks(src_refs, dst_refs):
    for src, dst in zip(src_refs, dst_refs):
        dst[...] = src[...].astype(BF16)


_MIXER_INPUTS = 13


def _mixer_kernel(*refs, ts, sub, cc, sc, n_cast):
    (x_ref, nw_ref, win_f32_ref, cw_ref, cb_ref, cng_ref, cnb_ref, sng_ref, snb_ref,
     sw_ref, sb_ref, wout_f32_ref, gavg_ref) = refs[:_MIXER_INPUTS]
    cast_in = refs[_MIXER_INPUTS:_MIXER_INPUTS + n_cast]
    o_ref = refs[_MIXER_INPUTS + n_cast]
    cast_out = refs[_MIXER_INPUTS + n_cast + 1:_MIXER_INPUTS + 2 * n_cast + 1]
    cbuf_ref, win_ref, wout_ref = refs[_MIXER_INPUTS + 2 * n_cast + 1:]
    s = pl.program_id(1)
    _cast_blocks(cast_in, cast_out)

    @pl.when(jnp.logical_and(pl.program_id(0) == 0, s == 0))
    def _():
        win_ref[...] = win_f32_ref[...].astype(BF16)
        wout_ref[...] = wout_f32_ref[...].astype(BF16)

    @pl.when(s == 0)
    def _():
        cbuf_ref[0:CONV_HALO, :] = jnp.zeros((CONV_HALO, cc), F32)

    row = lax.broadcasted_iota(jnp.int32, (CHUNK, CHUNK), 0)
    col = lax.broadcasted_iota(jnp.int32, (CHUNK, CHUNK), 1)
    tril = row >= col
    hd = sc // SGU_HEADS
    nq = -(-CONV_WIDTH // SUBLANES)

    nsub = ts // sub
    d_in = 2 * cc + 2 * sc
    cbw = d_in // _MIX_PIECES

    def sub_tile(j):
        r0 = j * sub
        x = x_ref[0, r0:r0 + sub, :]
        h = _rms_norm(x, nw_ref[...]).astype(BF16)
        zblk = []
        for k in range(_MIX_PIECES):
            zblk.append(jnp.dot(h, win_ref[:, k * cbw:(k + 1) * cbw], preferred_element_type=F32))
            yield
        z = jnp.concatenate(zblk, axis=1)
        a = z[:, :cc]
        gate = z[:, cc:2 * cc]
        u = z[:, 2 * cc:2 * cc + sc]
        v = z[:, 2 * cc + sc:]

        base = CONV_HALO + r0
        cbuf_ref[base:base + sub, :] = a * _sigmoid(gate)
        acc = jnp.broadcast_to(cb_ref[...], (sub, cc))
        for b in range(SUBLANES):
            zb = None
            for q in range(nq):
                d = SUBLANES * q + b
                if d >= CONV_WIDTH:
                    continue
                start = base - SUBLANES * (q + 1)
                term = (cw_ref[CONV_WIDTH - 1 - d:CONV_WIDTH - d, :]
                        * cbuf_ref[start:start + sub + SUBLANES, :])
                zb = term if zb is None else zb + term
            acc = acc + zb[SUBLANES - b:SUBLANES - b + sub, :]
            yield

        gavg = gavg_ref[...]
        acc_hi = acc.astype(BF16)
        acc_lo = (acc - acc_hi.astype(F32)).astype(BF16)
        mu = (jnp.dot(acc_hi, gavg, preferred_element_type=F32)
              + jnp.dot(acc_lo, gavg, preferred_element_type=F32))
        yield
        dev = acc - mu
        var = jnp.dot((dev * dev).astype(BF16), gavg, preferred_element_type=F32)
        yield
        cn = dev * lax.rsqrt(var + EPS) * cng_ref[...] + cnb_ref[...]
        c_out = _silu(cn).astype(BF16)
        yield

        u = _gelu(u)
        v = _gelu(v)
        yield
        g_cols = []
        for hh in range(SGU_HEADS):
            vh = v[:, hh * hd:(hh + 1) * hd]
            mu_h = jnp.mean(vh, axis=-1, keepdims=True)
            dh = vh - mu_h
            var_h = jnp.mean(dh * dh, axis=-1, keepdims=True)
            vn = (dh * lax.rsqrt(var_h + EPS) * sng_ref[:, hh * hd:(hh + 1) * hd]
                  + snb_ref[:, hh * hd:(hh + 1) * hd]).astype(BF16)
            ws = jnp.where(tril, sw_ref[hh], 0.0).astype(BF16)
            rows = []
            for ci in range(sub // CHUNK):
                sp = jnp.dot(ws, vn[ci * CHUNK:(ci + 1) * CHUNK, :], preferred_element_type=F32)
                rows.append(sp + sb_ref[hh])
            sp_h = rows[0] if len(rows) == 1 else jnp.concatenate(rows, axis=0)
            g_cols.append((u[:, hh * hd:(hh + 1) * hd] * sp_h).astype(BF16))
            if hh % 2 == 1:
                yield
        g_out = jnp.concatenate(g_cols, axis=1)

        yc = jnp.dot(c_out, wout_ref[0:cc, :], preferred_element_type=F32)
        yield
        yg = jnp.dot(g_out, wout_ref[cc:cc + sc, :], preferred_element_type=F32)
        o_ref[0, r0:r0 + sub, :] = x + (yc + yg)
        yield

    gens = [sub_tile(j) for j in range(nsub)]
    for tick in range(_MIX_PIECES * (nsub + _MIX_STAGES - 1)):
        for j in reversed(range(nsub)):
            if 0 <= tick - _MIX_PIECES * j < _MIX_PIECES * _MIX_STAGES:
                next(gens[j])

    cbuf_ref[0:CONV_HALO, :] = cbuf_ref[ts:ts + CONV_HALO, :]


def _mixer(x, layer, norm_mix, w_in, conv_w, conv_b, conv_ng, conv_nb, sgu_ng, sgu_nb, sgu_w, sgu_b,
           w_out, *, ts, sub, cast=()):
    B, S, D = x.shape
    L = w_in.shape[0]
    cc = conv_w.shape[-1]
    sc = sgu_ng.shape[-1]
    d_in = w_in.shape[-1]
    hd = sc // SGU_HEADS
    assert S % ts == 0 and ts % sub == 0 and sub % CHUNK == 0 and d_in == 2 * cc + 2 * sc
    ns = S // ts
    gs = cc // CONV_GROUPS
    gid = jnp.arange(cc) // gs
    gavg = jnp.where(gid[:, None] == gid[None, :], 1.0 / gs, 0.0).astype(BF16)
    sb_full = jnp.broadcast_to(sgu_b[:, :, :, None], (L, SGU_HEADS, CHUNK, hd))
    rows = lambda p: p.reshape(L, 1, -1)
    layer_block = lambda *shape: pl.BlockSpec((None,) + shape, lambda b, s: (layer,) + (0,) * len(shape))
    cast_arrs, cast_in_specs, cast_out_specs, cast_shapes = _cast_specs(
        cast, B * ns, lambda b, s: b * ns + s)
    kern = functools.partial(_mixer_kernel, ts=ts, sub=sub, cc=cc, sc=sc, n_cast=len(cast))
    in_specs = [
        pl.BlockSpec((1, ts, D), lambda b, s: (b, s, 0)),
        layer_block(1, D),
        layer_block(D, d_in),
        layer_block(CONV_WIDTH, cc),
        layer_block(1, cc), layer_block(1, cc), layer_block(1, cc),
        layer_block(1, sc), layer_block(1, sc),
        layer_block(SGU_HEADS, CHUNK, CHUNK),
        layer_block(SGU_HEADS, CHUNK, hd),
        layer_block(cc + sc, D),
        pl.BlockSpec((cc, cc), lambda b, s: (0, 0)),
    ]
    assert len(in_specs) == _MIXER_INPUTS
    return pl.pallas_call(
        kern,
        grid=(B, ns),
        in_specs=in_specs + cast_in_specs,
        out_specs=[pl.BlockSpec((1, ts, D), lambda b, s: (b, s, 0))] + cast_out_specs,
        out_shape=[jax.ShapeDtypeStruct((B, S, D), F32)] + cast_shapes,
        scratch_shapes=[pltpu.VMEM((CONV_HALO + ts, cc), F32),
                        pltpu.VMEM((D, d_in), BF16), pltpu.VMEM((cc + sc, D), BF16)],
        compiler_params=pltpu.CompilerParams(
            dimension_semantics=("arbitrary", "arbitrary"),
            vmem_limit_bytes=VMEM_LIMIT_BYTES),
        name="mixer",
    )(x, rows(norm_mix), w_in, conv_w.reshape(L, CONV_WIDTH, cc), rows(conv_b), rows(conv_ng),
      rows(conv_nb), rows(sgu_ng), rows(sgu_nb), sgu_w, sb_full, w_out, gavg, *cast_arrs)


def _swiglu_tile(h, wg_ref, wu_ref, wd_ref, ff_chunk):
    ff = wg_ref.shape[-1]
    out = None
    for c0 in range(0, ff, ff_chunk):
        c1 = min(ff, c0 + ff_chunk)
        g = jnp.dot(h, wg_ref[:, c0:c1], preferred_element_type=F32)
        u = jnp.dot(h, wu_ref[:, c0:c1], preferred_element_type=F32)
        a = (_silu(g) * u).astype(BF16)
        y = jnp.dot(a, wd_ref[c0:c1, :], preferred_element_type=F32)
        out = y if out is None else out + y
    return out


_FFN_INPUTS = 5


def _ffn_kernel(*refs, ff_chunk, n_cast):
    x_ref, nw_ref, wg_ref, wu_ref, wd_ref = refs[:_FFN_INPUTS]
    cast_in = refs[_FFN_INPUTS:_FFN_INPUTS + n_cast]
    o_ref = refs[_FFN_INPUTS + n_cast]
    cast_out = refs[_FFN_INPUTS + n_cast + 1:]
    _cast_blocks(cast_in, cast_out)
    x = x_ref[...]
    h = _rms_norm(x, nw_ref[...]).astype(BF16)
    o_ref[...] = x + _swiglu_tile(h, wg_ref, wu_ref, wd_ref, ff_chunk)


def _ffn(x2, layer, norm_ffn, wg, wu, wd, *, tm, ff_chunk, cast=()):
    T, D = x2.shape
    FF = wg.shape[-1]
    assert T % tm == 0 and wg.dtype == BF16
    cast_arrs, cast_in_specs, cast_out_specs, cast_shapes = _cast_specs(cast, T // tm, lambda i: i)
    kern = functools.partial(_ffn_kernel, ff_chunk=ff_chunk, n_cast=len(cast))
    in_specs = [
        pl.BlockSpec((tm, D), lambda i: (i, 0)),
        pl.BlockSpec((None, 1, D), lambda i: (layer, 0, 0)),
        pl.BlockSpec((D, FF), lambda i: (0, 0)),
        pl.BlockSpec((D, FF), lambda i: (0, 0)),
        pl.BlockSpec((FF, D), lambda i: (0, 0)),
    ]
    assert len(in_specs) == _FFN_INPUTS
    return pl.pallas_call(
        kern,
        grid=(T // tm,),
        in_specs=in_specs + cast_in_specs,
        out_specs=[pl.BlockSpec((tm, D), lambda i: (i, 0))] + cast_out_specs,
        out_shape=[jax.ShapeDtypeStruct((T, D), F32)] + cast_shapes,
        compiler_params=pltpu.CompilerParams(
            dimension_semantics=("arbitrary",), vmem_limit_bytes=VMEM_LIMIT_BYTES),
        name="ffn",
    )(x2, norm_ffn.reshape(norm_ffn.shape[0], 1, D), wg, wu, wd, *cast_arrs)


def _round_up(n, m):
    return (n + m - 1) // m * m


def _piece_sizes(largest):
    sizes = []
    s = largest
    while s >= BF16_ROWS:
        sizes.append(s)
        s //= 2
    assert sizes and sizes[-1] == BF16_ROWS
    return sizes


def _for_each_piece(count, sizes, fn):
    for s in sizes:
        offset = jnp.bitwise_and(count, -2 * s)

        @pl.when(jnp.bitwise_and(count, s) != 0)
        def _():
            fn(offset, s)


def _route_kernel(x_ref, nw_ref, rt_ref, ltri_ref, utri_ref,
                  hn_ref, dcol_ref, drow_ref, cnt_ref, *, n_experts):
    x = x_ref[...]
    h = _rms_norm(x, nw_ref[...]).astype(BF16)
    hn_ref[...] = h
    logits = jnp.dot(h, rt_ref[...], preferred_element_type=F32)
    lane = lax.broadcasted_iota(jnp.int32, logits.shape, 1)
    logits = jnp.where(lane < n_experts, logits, -jnp.inf)
    m1 = jnp.max(logits, axis=-1, keepdims=True)
    i1 = jnp.min(jnp.where(logits == m1, lane, LANES), axis=-1, keepdims=True)
    rest = jnp.where(lane == i1, -jnp.inf, logits)
    m2 = jnp.max(rest, axis=-1, keepdims=True)
    i2 = jnp.min(jnp.where(rest == m2, lane, LANES), axis=-1, keepdims=True)
    e2 = jnp.exp(m2 - m1)
    den = 1.0 + e2
    g1 = 1.0 / den
    g2 = e2 / den
    sel1 = lane == i1
    sel2 = lane == i2
    self = jnp.logical_or(sel1, sel2).astype(F32)
    rank = jnp.dot(ltri_ref[...], self.astype(BF16), preferred_element_type=F32)
    cnt = jnp.sum(self, axis=0, keepdims=True)
    cnt_pad = jnp.floor((cnt + (BF16_ROWS - 1.0)) * (1.0 / BF16_ROWS)) * BF16_ROWS
    seg = jnp.dot(jnp.broadcast_to(cnt_pad, (SUBLANES, LANES)).astype(BF16), utri_ref[...],
                  preferred_element_type=F32)[0:1, :]
    pos = seg + rank
    dest1 = jnp.sum(jnp.where(sel1, pos, 0.0), axis=-1, keepdims=True)
    dest2 = jnp.sum(jnp.where(sel2, pos, 0.0), axis=-1, keepdims=True)
    info = jnp.where(lane == 0, dest1, jnp.where(lane == 1, dest2,
                     jnp.where(lane == 2, g1, jnp.where(lane == 3, g2, -1.0))))
    dcol_ref[...] = info
    drow_ref[0] = jnp.transpose(info)[0:SUBLANES, :]
    cnt_ref[0] = cnt.astype(jnp.int32)


def _route(x2, nw, router, *, win):
    T, D = x2.shape
    E = router.shape[-1]
    assert T % win == 0 and E <= SUBLANES
    nwin = T // win
    rt = jnp.zeros((D, LANES), F32).at[:, :E].set(router).astype(BF16)
    r = jnp.arange(win)
    ltri = (r[:, None] > r[None, :]).astype(BF16)
    l = jnp.arange(LANES)
    utri = (l[:, None] < l[None, :]).astype(BF16)
    kern = functools.partial(_route_kernel, n_experts=E)
    return pl.pallas_call(
        kern,
        grid=(nwin,),
        in_specs=[
            pl.BlockSpec((win, D), lambda w: (w, 0)),
            pl.BlockSpec((1, D), lambda w: (0, 0)),
            pl.BlockSpec((D, LANES), lambda w: (0, 0)),
            pl.BlockSpec((win, win), lambda w: (0, 0)),
            pl.BlockSpec((LANES, LANES), lambda w: (0, 0)),
        ],
        out_specs=[
            pl.BlockSpec((win, D), lambda w: (w, 0)),
            pl.BlockSpec((win, LANES), lambda w: (w, 0)),
            pl.BlockSpec((1, SUBLANES, win), lambda w: (w, 0, 0)),
            pl.BlockSpec((1, 1, LANES), lambda w: (w, 0, 0)),
        ],
        out_shape=[
            jax.ShapeDtypeStruct((T, D), BF16),
            jax.ShapeDtypeStruct((T, LANES), F32),
            jax.ShapeDtypeStruct((nwin, SUBLANES, win), F32),
            jax.ShapeDtypeStruct((nwin, 1, LANES), jnp.int32),
        ],
        compiler_params=pltpu.CompilerParams(
            dimension_semantics=("arbitrary",), vmem_limit_bytes=VMEM_LIMIT_BYTES),
        name="moe_route",
    )(x2, nw.reshape(1, D).astype(F32), rt, ltri, utri)


def _compact_kernel(seg_ref, off_ref, cnt_ref, fill_ref, hn_ref, drow_ref, xs_ref,
                    stage_ref, zero_ref, sem, *, win, n_experts, rows, tm):
    w = pl.program_id(0)
    nwin = pl.num_programs(0)
    slot = w % 2
    d = hn_ref.shape[-1]
    sizes = _piece_sizes(win)

    def segments(ww, sl, start):
        for e in range(n_experts):
            src0 = seg_ref[ww * n_experts + e]
            dst0 = off_ref[ww * n_experts + e]

            def piece(offset, size):
                cp = pltpu.make_async_copy(
                    stage_ref.at[sl, pl.ds(pl.multiple_of(src0 + offset, BF16_ROWS), size)],
                    xs_ref.at[pl.ds(pl.multiple_of(dst0 + offset, BF16_ROWS), size)],
                    sem.at[sl, e])
                cp.start() if start else cp.wait()

            _for_each_piece(cnt_ref[ww * n_experts + e], sizes, piece)

    def zero_rows(dst, size, e, start):
        cp = pltpu.make_async_copy(zero_ref.at[pl.ds(0, size)],
                                   xs_ref.at[pl.ds(pl.multiple_of(dst, BF16_ROWS), size)],
                                   sem.at[2, e])
        cp.start() if start else cp.wait()

    @pl.when(w == 0)
    def _():
        zero_ref[...] = jnp.zeros(zero_ref.shape, BF16)

    d1 = drow_ref[0, 0:1, :]
    d2 = drow_ref[0, 1:2, :]
    g1 = drow_ref[0, 2:3, :]
    g2 = drow_ref[0, 3:4, :]
    r = lax.broadcasted_iota(jnp.int32, (rows, win), 0).astype(F32)
    m1 = r == d1
    m2 = r == d2
    onehot = jnp.logical_or(m1, m2).astype(BF16)
    xg = jnp.dot(onehot, hn_ref[...], preferred_element_type=F32)
    stage_ref[slot, :, 0:d] = xg.astype(BF16)
    gate = jnp.sum(jnp.where(m1, g1, 0.0) + jnp.where(m2, g2, 0.0), axis=-1, keepdims=True)
    hi = gate.astype(BF16).astype(F32)
    mid = (gate - hi).astype(BF16).astype(F32)
    lo = gate - hi - mid
    lane = lax.broadcasted_iota(jnp.int32, (rows, LANES), 1)
    terms = jnp.where(lane == 0, hi, jnp.where(lane == 1, mid, jnp.where(lane == 2, lo, 0.0)))
    stage_ref[slot, :, d:d + LANES] = terms.astype(BF16)

    segments(w, slot, True)

    @pl.when(w > 0)
    def _():
        segments(w - 1, 1 - slot, False)

    @pl.when(w == nwin - 1)
    def _():
        segments(w, slot, False)
        gap_sizes = _piece_sizes(tm // 2)
        for start in (True, False):
            for e in range(n_experts):
                _for_each_piece(fill_ref[n_experts + e], gap_sizes,
                                lambda offset, size, e=e: zero_rows(fill_ref[e] + offset, size, e, start))
        tail0 = fill_ref[2 * n_experts]
        ntail = fill_ref[2 * n_experts + 1]
        tail_rows = zero_ref.shape[0]

        def tail_start(j, c):
            zero_rows(tail0 + j * tail_rows, tail_rows, 0, True)
            return c

        def tail_wait(j, c):
            zero_rows(tail0 + j * tail_rows, tail_rows, 0, False)
            return c

        lax.fori_loop(0, ntail, tail_start, 0)
        lax.fori_loop(0, ntail, tail_wait, 0)


def _compact(seg, off, cnt_pad, fill, hn, drow, *, win, n_experts, cap, tm):
    T, D = hn.shape
    nwin = T // win
    rows = _round_up(TOP_K * win + n_experts * (BF16_ROWS - 1), BF16_ROWS)
    kern = functools.partial(_compact_kernel, win=win, n_experts=n_experts, rows=rows, tm=tm)
    return pl.pallas_call(
        kern,
        grid_spec=pltpu.PrefetchScalarGridSpec(
            num_scalar_prefetch=4,
            grid=(nwin,),
            in_specs=[
                pl.BlockSpec((win, D), lambda w, *_: (w, 0)),
                pl.BlockSpec((1, SUBLANES, win), lambda w, *_: (w, 0, 0)),
            ],
            out_specs=pl.BlockSpec(memory_space=pl.ANY),
            scratch_shapes=[pltpu.VMEM((2, rows, D + LANES), BF16),
                            pltpu.VMEM((tm // 2, D + LANES), BF16),
                            pltpu.SemaphoreType.DMA((3, n_experts))],
        ),
        out_shape=jax.ShapeDtypeStruct((cap, D + LANES), BF16),
        compiler_params=pltpu.CompilerParams(
            dimension_semantics=("arbitrary",), vmem_limit_bytes=VMEM_LIMIT_BYTES),
        name="moe_compact",
    )(seg, off, cnt_pad, fill, hn, drow)


def _expert_kernel(te_ref, tv_ref, xs_ref, wg_ref, wu_ref, wd_ref, ys_ref, *, ff_chunk):
    i = pl.program_id(0)
    d = ys_ref.shape[-1]

    @pl.when(tv_ref[i] > 0)
    def _():
        y = _swiglu_tile(xs_ref[:, 0:d], wg_ref.at[0], wu_ref.at[0], wd_ref.at[0], ff_chunk)
        gate = jnp.sum(xs_ref[:, d:d + LANES].astype(F32), axis=-1, keepdims=True)
        ys_ref[...] = (gate * y).astype(BF16)

    @pl.when(tv_ref[i] == 0)
    def _():
        ys_ref[...] = jnp.zeros(ys_ref.shape, BF16)


def _experts(tile_e, tile_v, xs, wg, wu, wd, *, tm, ff_chunk):
    cap = xs.shape[0]
    _, D, FF = wg.shape
    assert cap == tile_e.shape[0] * tm and xs.shape[1] == D + LANES
    kern = functools.partial(_expert_kernel, ff_chunk=ff_chunk)
    return pl.pallas_call(
        kern,
        grid_spec=pltpu.PrefetchScalarGridSpec(
            num_scalar_prefetch=2,
            grid=(cap // tm,),
            in_specs=[
                pl.BlockSpec((tm, D + LANES), lambda i, te, tv: (i, 0)),
                pl.BlockSpec((1, D, FF), lambda i, te, tv: (te[i], 0, 0)),
                pl.BlockSpec((1, D, FF), lambda i, te, tv: (te[i], 0, 0)),
                pl.BlockSpec((1, FF, D), lambda i, te, tv: (te[i], 0, 0)),
            ],
            out_specs=pl.BlockSpec((tm, D), lambda i, te, tv: (i, 0)),
        ),
        out_shape=jax.ShapeDtypeStruct((cap, D), BF16),
        compiler_params=pltpu.CompilerParams(
            dimension_semantics=("arbitrary",), vmem_limit_bytes=VMEM_LIMIT_BYTES),
        name="moe_experts",
    )(tile_e, tile_v, xs, wg.astype(BF16), wu.astype(BF16), wd.astype(BF16))


def _combine_kernel(seg_ref, off_ref, cnt_ref, x_ref, dcol_ref, fw_ref, ys_ref, o_ref,
                    stage_ref, sem, *, win, n_experts):
    w = pl.program_id(0)
    nwin = pl.num_programs(0)
    slot = w % 2
    rows = stage_ref.shape[1]
    sizes = _piece_sizes(win)

    def segments(ww, sl, start):
        for e in range(n_experts):
            src0 = off_ref[ww * n_experts + e]
            dst0 = seg_ref[ww * n_experts + e]

            def piece(offset, size):
                cp = pltpu.make_async_copy(
                    ys_ref.at[pl.ds(pl.multiple_of(src0 + offset, BF16_ROWS), size)],
                    stage_ref.at[sl, pl.ds(pl.multiple_of(dst0 + offset, BF16_ROWS), size)],
                    sem.at[sl, e])
                cp.start() if start else cp.wait()

            _for_each_piece(cnt_ref[ww * n_experts + e], sizes, piece)

    @pl.when(w == 0)
    def _():
        stage_ref[...] = jnp.zeros(stage_ref.shape, BF16)
        segments(0, 0, True)

    @pl.when(w + 1 < nwin)
    def _():
        segments(w + 1, 1 - slot, True)

    segments(w, slot, False)

    d1 = dcol_ref[:, 0:1]
    d2 = dcol_ref[:, 1:2]
    r = lax.broadcasted_iota(jnp.int32, (win, rows), 1).astype(F32)
    onehot = jnp.logical_or(r == d1, r == d2).astype(BF16)
    y = jnp.dot(onehot, stage_ref[slot], preferred_element_type=F32)
    o_ref[...] = _rms_norm(x_ref[...] + y, fw_ref[...])


def _combine(seg, off, cnt_pad, x2, dcol, fw, ys, *, win, n_experts):
    T, D = x2.shape
    rows = _round_up(TOP_K * win + n_experts * (BF16_ROWS - 1), BF16_ROWS)
    kern = functools.partial(_combine_kernel, win=win, n_experts=n_experts)
    return pl.pallas_call(
        kern,
        grid_spec=pltpu.PrefetchScalarGridSpec(
            num_scalar_prefetch=3,
            grid=(T // win,),
            in_specs=[
                pl.BlockSpec((win, D), lambda w, *_: (w, 0)),
                pl.BlockSpec((win, LANES), lambda w, *_: (w, 0)),
                pl.BlockSpec((1, D), lambda w, *_: (0, 0)),
                pl.BlockSpec(memory_space=pl.ANY),
            ],
            out_specs=pl.BlockSpec((win, D), lambda w, *_: (w, 0)),
            scratch_shapes=[pltpu.VMEM((2, rows, D), BF16),
                            pltpu.SemaphoreType.DMA((2, n_experts))],
        ),
        out_shape=jax.ShapeDtypeStruct((T, D), F32),
        compiler_params=pltpu.CompilerParams(
            dimension_semantics=("arbitrary",), vmem_limit_bytes=VMEM_LIMIT_BYTES),
        name="moe_combine",
    )(seg, off, cnt_pad, x2, dcol, fw.reshape(1, D).astype(F32), ys)


def _moe(x2, nw, router, wg, wu, wd, fw, *, win=MOE_WINDOW, tm=MOE_TILE):
    T, D = x2.shape
    E = router.shape[-1]
    nwin = T // win
    hn, dcol, drow, cnt = _route(x2, nw, router, win=win)

    cnt = cnt.reshape(nwin, LANES)[:, :E]
    cnt_pad = _round_up(cnt, BF16_ROWS)
    seg = jnp.cumsum(cnt_pad, axis=1) - cnt_pad
    total = jnp.sum(cnt_pad, axis=0)
    region = _round_up(total, tm)
    r_end = jnp.cumsum(region)
    r_start = r_end - region
    off = r_start[None, :] + jnp.cumsum(cnt_pad, axis=0) - cnt_pad
    max_rows = TOP_K * T + nwin * E * (BF16_ROWS - 1)
    ntiles_max = max_rows // tm + E
    cap = ntiles_max * tm
    ti = jnp.arange(ntiles_max)
    t_end = r_end // tm
    tile_e = jnp.minimum(jnp.sum(ti[:, None] >= t_end[None, :], axis=1), E - 1).astype(jnp.int32)
    tile_v = (ti < t_end[-1]).astype(jnp.int32)
    tail = jnp.stack([r_end[-1], (cap - r_end[-1]) // (tm // 2)])
    fill = jnp.concatenate([r_start + total, region - total, tail])
    flat = lambda a: a.reshape(-1).astype(jnp.int32)
    seg, off, cnt_pad, fill = flat(seg), flat(off), flat(cnt_pad), flat(fill)

    xs = _compact(seg, off, cnt_pad, fill, hn, drow, win=win, n_experts=E, cap=cap, tm=tm)
    ys = _experts(tile_e, tile_v, xs, wg, wu, wd, tm=tm, ff_chunk=wg.shape[-1])
    return _combine(seg, off, cnt_pad, x2, dcol, fw, ys, win=win, n_experts=E)


def kernel(x, norm_mix, w_in, conv_w, conv_b, conv_ng, conv_nb, sgu_ng, sgu_nb, sgu_w, sgu_b, w_out,
           norm_ffn, ffn_wg, ffn_wu, ffn_wd, moe_router, moe_wg, moe_wu, moe_wd, norm_final):
    B, S, D = x.shape
    depth = norm_mix.shape[0]
    assert depth == 2, "trunk is one dense layer followed by one MoE layer"
    mix = functools.partial(_mixer, ts=512, sub=256)
    mixer_params = (norm_mix, w_in, conv_w, conv_b, conv_ng, conv_nb, sgu_ng, sgu_nb, sgu_w, sgu_b, w_out)
    x, ffn_wg_b, ffn_wu_b, ffn_wd_b, moe_wg_b = mix(
        x, 0, *mixer_params, cast=((ffn_wg, 0), (ffn_wu, 0), (ffn_wd, 0), (moe_wg, 0)))
    FF = ffn_wg.shape[-1]
    x2, moe_wu_b = _ffn(x.reshape(B * S, D), 0, norm_ffn, ffn_wg_b, ffn_wu_b, ffn_wd_b,
                        tm=512, ff_chunk=FF, cast=((moe_wu, 0),))
    x, moe_wd_b = mix(x2.reshape(B, S, D), 1, *mixer_params, cast=((moe_wd, 0),))
    y2 = _moe(x.reshape(B * S, D), norm_ffn[1], moe_router[0], moe_wg_b.reshape(moe_wg[0].shape),
              moe_wu_b.reshape(moe_wu[0].shape), moe_wd_b.reshape(moe_wd[0].shape), norm_final)
    return y2.reshape(B, S, D)
```

```python
import functools

import jax
import jax.numpy as jnp
from jax import lax
from jax.experimental import pallas as pl
from jax.experimental.pallas import tpu as pltpu

F32 = jnp.float32
BF16 = jnp.bfloat16

EPS = 1e-6
CONV_WIDTH = 31
CONV_GROUPS = 8
SGU_HEADS = 4
CHUNK = 128
TOP_K = 2

SUBLANES = 8
LANES = 128
BF16_ROWS = 16
CONV_HALO = 32
VMEM_LIMIT_BYTES = 56 * 1024 * 1024
_MIX_STAGES = 3
_MIX_PIECES = SUBLANES

MOE_WINDOW = 512
MOE_TILE = 512


def _rms_norm(x, g):
    ms = jnp.mean(x * x, axis=-1, keepdims=True)
    return x * lax.rsqrt(ms + EPS) * g


def _gelu(x):
    return 0.5 * x * (1.0 + lax.erf(x * (2.0 ** -0.5)))


def _sigmoid(x):
    return 0.5 * jnp.tanh(0.5 * x) + 0.5


def _silu(x):
    h = 0.5 * x
    return h * jnp.tanh(h) + h


def _cast_plan(rows, nsteps):
    nblk = nsteps
    while rows % nblk or (rows // nblk) % BF16_ROWS:
        nblk //= 2
        assert nblk >= 1, rows
    return nblk, rows // nblk


def _cast_specs(items, nsteps, step_of):
    arrs, in_specs, out_specs, shapes = [], [], [], []
    for a, layer in items:
        a = a.reshape(a.shape[0], -1, a.shape[-1])
        nblk, blk = _cast_plan(a.shape[1], nsteps)
        blk_of = lambda *g, nblk=nblk: jnp.minimum(step_of(*g), nblk - 1)
        arrs.append(a)
        in_specs.append(pl.BlockSpec((None, blk, a.shape[2]),
                                     lambda *g, layer=layer, blk_of=blk_of: (layer, blk_of(*g), 0)))
        out_specs.append(pl.BlockSpec((blk, a.shape[2]), lambda *g, blk_of=blk_of: (blk_of(*g), 0)))
        shapes.append(jax.ShapeDtypeStruct(a.shape[1:], BF16))
    return arrs, in_specs, out_specs, shapes


def _cast_blocks(src_refs, dst_refs):
    for src, dst in zip(src_refs, dst_refs):
        dst[...] = src[...].astype(BF16)


_MIXER_INPUTS = 12


def _mixer_kernel(*refs, layer, ts, sub, cc, sc, n_cast):
    (x_ref, nw_all, win_f32_ref, cw_ref, cb_all, cng_all, cnb_all, sng_all, snb_all,
     sw_ref, sb_ref, wout_f32_ref) = refs[:_MIXER_INPUTS]
    nw_ref, cb_ref, cng_ref, cnb_ref, sng_ref, snb_ref = (
        r.at[layer:layer + 1] for r in (nw_all, cb_all, cng_all, cnb_all, sng_all, snb_all))
    cast_in = refs[_MIXER_INPUTS:_MIXER_INPUTS + n_cast]
    o_ref = refs[_MIXER_INPUTS + n_cast]
    cast_out = refs[_MIXER_INPUTS + n_cast + 1:_MIXER_INPUTS + 2 * n_cast + 1]
    cbuf_ref, win_ref, wout_ref = refs[_MIXER_INPUTS + 2 * n_cast + 1:]
    s = pl.program_id(1)
    _cast_blocks(cast_in, cast_out)

    @pl.when(jnp.logical_and(pl.program_id(0) == 0, s == 0))
    def _():
        win_ref[...] = win_f32_ref[...].astype(BF16)
        wout_ref[...] = wout_f32_ref[...].astype(BF16)

    @pl.when(s == 0)
    def _():
        cbuf_ref[0:CONV_HALO, :] = jnp.zeros((CONV_HALO, cc), F32)

    row = lax.broadcasted_iota(jnp.int32, (CHUNK, CHUNK), 0)
    col = lax.broadcasted_iota(jnp.int32, (CHUNK, CHUNK), 1)
    tril = row >= col
    hd = sc // SGU_HEADS
    nq = -(-CONV_WIDTH // SUBLANES)

    nsub = ts // sub
    d_in = 2 * cc + 2 * sc
    cbw = d_in // _MIX_PIECES

    def sub_tile(j):
        r0 = j * sub
        x = x_ref[0, r0:r0 + sub, :]
        h = _rms_norm(x, nw_ref[...]).astype(BF16)
        zblk = []
        for k in range(_MIX_PIECES):
            zblk.append(jnp.dot(h, win_ref[:, k * cbw:(k + 1) * cbw], preferred_element_type=F32))
            yield
        z = jnp.concatenate(zblk, axis=1)
        a = z[:, :cc]
        gate = z[:, cc:2 * cc]
        u = z[:, 2 * cc:2 * cc + sc]
        v = z[:, 2 * cc + sc:]

        base = CONV_HALO + r0
        cbuf_ref[base:base + sub, :] = a * _sigmoid(gate)
        acc = jnp.broadcast_to(cb_ref[...], (sub, cc))
        for b in range(SUBLANES):
            zb = None
            for q in range(nq):
                d = SUBLANES * q + b
                if d >= CONV_WIDTH:
                    continue
                start = base - SUBLANES * (q + 1)
                term = (cw_ref[CONV_WIDTH - 1 - d]
                        * cbuf_ref[start:start + sub + SUBLANES, :])
                zb = term if zb is None else zb + term
            acc = acc + zb[SUBLANES - b:SUBLANES - b + sub, :]
            yield

        gs = cc // CONV_GROUPS
        assert 2 * gs == LANES
        lower = lax.broadcasted_iota(jnp.int32, (sub, LANES), 1) < gs

        def group_mean(t):
            s_all = jnp.sum(t, axis=-1, keepdims=True)
            s_lo = jnp.sum(jnp.where(lower, t, 0.0), axis=-1, keepdims=True)
            return jnp.where(lower, s_lo, s_all - s_lo) * (1.0 / gs)

        cn_blocks = []
        for lb in range(cc // LANES):
            blk = acc[:, lb * LANES:(lb + 1) * LANES]
            dev = blk - group_mean(blk)
            var = group_mean(dev * dev)
            cn_blocks.append(dev * lax.rsqrt(var + EPS))
            if lb % 2 == 1:
                yield
        cn = jnp.concatenate(cn_blocks, axis=1) * cng_ref[...] + cnb_ref[...]
        c_out = _silu(cn).astype(BF16)
        yield

        u = _gelu(u)
        v = _gelu(v)
        yield
        g_cols = []
        for hh in range(SGU_HEADS):
            vh = v[:, hh * hd:(hh + 1) * hd]
            mu_h = jnp.mean(vh, axis=-1, keepdims=True)
            dh = vh - mu_h
            var_h = jnp.mean(dh * dh, axis=-1, keepdims=True)
            vn = (dh * lax.rsqrt(var_h + EPS) * sng_ref[:, hh * hd:(hh + 1) * hd]
                  + snb_ref[:, hh * hd:(hh + 1) * hd]).astype(BF16)
            ws = jnp.where(tril, sw_ref[hh], 0.0).astype(BF16)
            rows = []
            for ci in range(sub // CHUNK):
                sp = jnp.dot(ws, vn[ci * CHUNK:(ci + 1) * CHUNK, :], preferred_element_type=F32)
                rows.append(sp + sb_ref[hh])
            sp_h = rows[0] if len(rows) == 1 else jnp.concatenate(rows, axis=0)
            g_cols.append((u[:, hh * hd:(hh + 1) * hd] * sp_h).astype(BF16))
            if hh % 2 == 1:
                yield
        g_out = jnp.concatenate(g_cols, axis=1)

        yc = jnp.dot(c_out, wout_ref[0:cc, :], preferred_element_type=F32)
        yield
        yg = jnp.dot(g_out, wout_ref[cc:cc + sc, :], preferred_element_type=F32)
        o_ref[0, r0:r0 + sub, :] = x + (yc + yg)
        yield

    gens = [sub_tile(j) for j in range(nsub)]
    for tick in range(_MIX_PIECES * (nsub + _MIX_STAGES - 1)):
        for j in reversed(range(nsub)):
            if 0 <= tick - _MIX_PIECES * j < _MIX_PIECES * _MIX_STAGES:
                next(gens[j])

    cbuf_ref[0:CONV_HALO, :] = cbuf_ref[ts:ts + CONV_HALO, :]


def _mixer(x, layer, norm_mix, w_in, conv_w, conv_b, conv_ng, conv_nb, sgu_ng, sgu_nb, sgu_w, sgu_b,
           w_out, *, ts, sub, cast=()):
    B, S, D = x.shape
    L = w_in.shape[0]
    cc = conv_w.shape[-1]
    sc = sgu_ng.shape[-1]
    d_in = w_in.shape[-1]
    hd = sc // SGU_HEADS
    assert S % ts == 0 and ts % sub == 0 and sub % CHUNK == 0 and d_in == 2 * cc + 2 * sc
    ns = S // ts
    sb_full = jnp.broadcast_to(sgu_b[:, :, :, None], (L, SGU_HEADS, CHUNK, hd))
    whole = lambda p: pl.BlockSpec(p.shape, lambda b, s: (0,) * p.ndim)
    layer_block = lambda *shape: pl.BlockSpec((None,) + shape, lambda b, s: (layer,) + (0,) * len(shape))
    cast_arrs, cast_in_specs, cast_out_specs, cast_shapes = _cast_specs(
        cast, B * ns, lambda b, s: b * ns + s)
    kern = functools.partial(_mixer_kernel, layer=layer, ts=ts, sub=sub, cc=cc, sc=sc, n_cast=len(cast))
    in_specs = [
        pl.BlockSpec((1, ts, D), lambda b, s: (b, s, 0)),
        whole(norm_mix),
        layer_block(D, d_in),
        layer_block(CONV_WIDTH, 1, cc),
        whole(conv_b), whole(conv_ng), whole(conv_nb),
        whole(sgu_ng), whole(sgu_nb),
        layer_block(SGU_HEADS, CHUNK, CHUNK),
        layer_block(SGU_HEADS, CHUNK, hd),
        layer_block(cc + sc, D),
    ]
    assert len(in_specs) == _MIXER_INPUTS
    return pl.pallas_call(
        kern,
        grid=(B, ns),
        in_specs=in_specs + cast_in_specs,
        out_specs=[pl.BlockSpec((1, ts, D), lambda b, s: (b, s, 0))] + cast_out_specs,
        out_shape=[jax.ShapeDtypeStruct((B, S, D), F32)] + cast_shapes,
        scratch_shapes=[pltpu.VMEM((CONV_HALO + ts, cc), F32),
                        pltpu.VMEM((D, d_in), BF16), pltpu.VMEM((cc + sc, D), BF16)],
        compiler_params=pltpu.CompilerParams(
            dimension_semantics=("arbitrary", "arbitrary"),
            vmem_limit_bytes=VMEM_LIMIT_BYTES),
        name="mixer",
    )(x, norm_mix, w_in, conv_w, conv_b, conv_ng, conv_nb, sgu_ng, sgu_nb, sgu_w, sb_full, w_out,
      *cast_arrs)


def _swiglu_tile(h, wg_ref, wu_ref, wd_ref, ff_chunk):
    ff = wg_ref.shape[-1]
    out = None
    for c0 in range(0, ff, ff_chunk):
        c1 = min(ff, c0 + ff_chunk)
        g = jnp.dot(h, wg_ref[:, c0:c1], preferred_element_type=F32)
        u = jnp.dot(h, wu_ref[:, c0:c1], preferred_element_type=F32)
        a = (_silu(g) * u).astype(BF16)
        y = jnp.dot(a, wd_ref[c0:c1, :], preferred_element_type=F32)
        out = y if out is None else out + y
    return out


_FFN_INPUTS = 5


def _ffn_kernel(*refs, layer, ff_chunk, n_cast):
    x_ref, nw_all, wg_ref, wu_ref, wd_ref = refs[:_FFN_INPUTS]
    nw_ref = nw_all.at[layer:layer + 1]
    cast_in = refs[_FFN_INPUTS:_FFN_INPUTS + n_cast]
    o_ref = refs[_FFN_INPUTS + n_cast]
    cast_out = refs[_FFN_INPUTS + n_cast + 1:]
    _cast_blocks(cast_in, cast_out)
    x = x_ref[...]
    h = _rms_norm(x, nw_ref[...]).astype(BF16)
    o_ref[...] = x + _swiglu_tile(h, wg_ref, wu_ref, wd_ref, ff_chunk)


def _ffn(x2, layer, norm_ffn, wg, wu, wd, *, tm, ff_chunk, cast=()):
    T, D = x2.shape
    FF = wg.shape[-1]
    assert T % tm == 0 and wg.dtype == BF16
    cast_arrs, cast_in_specs, cast_out_specs, cast_shapes = _cast_specs(cast, T // tm, lambda i: i)
    kern = functools.partial(_ffn_kernel, layer=layer, ff_chunk=ff_chunk, n_cast=len(cast))
    in_specs = [
        pl.BlockSpec((tm, D), lambda i: (i, 0)),
        pl.BlockSpec(norm_ffn.shape, lambda i: (0, 0)),
        pl.BlockSpec((D, FF), lambda i: (0, 0)),
        pl.BlockSpec((D, FF), lambda i: (0, 0)),
        pl.BlockSpec((FF, D), lambda i: (0, 0)),
    ]
    assert len(in_specs) == _FFN_INPUTS
    return pl.pallas_call(
        kern,
        grid=(T // tm,),
        in_specs=in_specs + cast_in_specs,
        out_specs=[pl.BlockSpec((tm, D), lambda i: (i, 0))] + cast_out_specs,
        out_shape=[jax.ShapeDtypeStruct((T, D), F32)] + cast_shapes,
        compiler_params=pltpu.CompilerParams(
            dimension_semantics=("arbitrary",), vmem_limit_bytes=VMEM_LIMIT_BYTES),
        name="ffn",
    )(x2, norm_ffn, wg, wu, wd, *cast_arrs)


def _round_up(n, m):
    return (n + m - 1) // m * m


def _piece_sizes(largest):
    sizes = []
    s = largest
    while s >= BF16_ROWS:
        sizes.append(s)
        s //= 2
    assert sizes and sizes[-1] == BF16_ROWS
    return sizes


def _for_each_piece(count, sizes, fn):
    for s in sizes:
        offset = jnp.bitwise_and(count, -2 * s)

        @pl.when(jnp.bitwise_and(count, s) != 0)
        def _():
            fn(offset, s)


def _route_kernel(x_ref, nw_ref, rt_ref, ltri_ref, utri_ref,
                  hn_ref, dcol_ref, drow_ref, cnt_ref, *, n_experts):
    win = ltri_ref.shape[0]
    for k in range(x_ref.shape[0] // win):
        rows = slice(k * win, (k + 1) * win)
        _route_window(x_ref.at[rows], nw_ref, rt_ref, ltri_ref, utri_ref, hn_ref.at[rows],
                      dcol_ref.at[rows], drow_ref.at[k], cnt_ref.at[k], n_experts)


def _route_window(x_ref, nw_ref, rt_ref, ltri_ref, utri_ref, hn_ref, dcol_ref, drow_ref, cnt_ref,
                  n_experts):
    x = x_ref[...]
    h = _rms_norm(x, nw_ref[...]).astype(BF16)
    hn_ref[...] = h
    logits = jnp.dot(h, rt_ref[...], preferred_element_type=F32)
    lane = lax.broadcasted_iota(jnp.int32, logits.shape, 1)
    logits = jnp.where(lane < n_experts, logits, -jnp.inf)
    m1 = jnp.max(logits, axis=-1, keepdims=True)
    i1 = jnp.min(jnp.where(logits == m1, lane, LANES), axis=-1, keepdims=True)
    rest = jnp.where(lane == i1, -jnp.inf, logits)
    m2 = jnp.max(rest, axis=-1, keepdims=True)
    i2 = jnp.min(jnp.where(rest == m2, lane, LANES), axis=-1, keepdims=True)
    e2 = jnp.exp(m2 - m1)
    den = 1.0 + e2
    g1 = 1.0 / den
    g2 = e2 / den
    sel1 = lane == i1
    sel2 = lane == i2
    self = jnp.logical_or(sel1, sel2).astype(F32)
    rank = jnp.dot(ltri_ref[...], self.astype(BF16), preferred_element_type=F32)
    cnt = jnp.sum(self, axis=0, keepdims=True)
    cnt_pad = jnp.floor((cnt + (BF16_ROWS - 1.0)) * (1.0 / BF16_ROWS)) * BF16_ROWS
    seg = jnp.dot(jnp.broadcast_to(cnt_pad, (SUBLANES, LANES)).astype(BF16), utri_ref[...],
                  preferred_element_type=F32)[0:1, :]
    pos = seg + rank
    dest1 = jnp.sum(jnp.where(sel1, pos, 0.0), axis=-1, keepdims=True)
    dest2 = jnp.sum(jnp.where(sel2, pos, 0.0), axis=-1, keepdims=True)
    info = jnp.where(lane == 0, dest1, jnp.where(lane == 1, dest2,
                     jnp.where(lane == 2, g1, jnp.where(lane == 3, g2, -1.0))))
    dcol_ref[...] = info
    drow_ref[...] = jnp.transpose(info)[0:SUBLANES, :]
    cnt_ref[...] = cnt.astype(jnp.int32)


def _route(x2, nw, router, *, win, per_step=2):
    T, D = x2.shape
    E = router.shape[-1]
    assert T % (win * per_step) == 0 and E <= SUBLANES
    nwin = T // win
    blk = win * per_step
    rt = jnp.zeros((D, LANES), F32).at[:, :E].set(router).astype(BF16)
    r = jnp.arange(win)
    ltri = (r[:, None] > r[None, :]).astype(BF16)
    l = jnp.arange(LANES)
    utri = (l[:, None] < l[None, :]).astype(BF16)
    kern = functools.partial(_route_kernel, n_experts=E)
    return pl.pallas_call(
        kern,
        grid=(nwin // per_step,),
        in_specs=[
            pl.BlockSpec((blk, D), lambda w: (w, 0)),
            pl.BlockSpec((1, D), lambda w: (0, 0)),
            pl.BlockSpec((D, LANES), lambda w: (0, 0)),
            pl.BlockSpec((win, win), lambda w: (0, 0)),
            pl.BlockSpec((LANES, LANES), lambda w: (0, 0)),
        ],
        out_specs=[
            pl.BlockSpec((blk, D), lambda w: (w, 0)),
            pl.BlockSpec((blk, LANES), lambda w: (w, 0)),
            pl.BlockSpec((per_step, SUBLANES, win), lambda w: (w, 0, 0)),
            pl.BlockSpec((per_step, 1, LANES), lambda w: (w, 0, 0)),
        ],
        out_shape=[
            jax.ShapeDtypeStruct((T, D), BF16),
            jax.ShapeDtypeStruct((T, LANES), F32),
            jax.ShapeDtypeStruct((nwin, SUBLANES, win), F32),
            jax.ShapeDtypeStruct((nwin, 1, LANES), jnp.int32),
        ],
        compiler_params=pltpu.CompilerParams(
            dimension_semantics=("arbitrary",), vmem_limit_bytes=VMEM_LIMIT_BYTES),
        name="moe_route",
    )(x2, nw.reshape(1, D).astype(F32), rt, ltri, utri)


def _compact_kernel(seg_ref, off_ref, cnt_ref, fill_ref, hn_ref, drow_ref, xs_ref,
                    stage_ref, zero_ref, sem, *, win, n_experts, rows, tm):
    w = pl.program_id(0)
    nwin = pl.num_programs(0)
    slot = w % 2
    d = hn_ref.shape[-1]
    sizes = _piece_sizes(win)

    def segments(ww, sl, start):
        for e in range(n_experts):
            src0 = seg_ref[ww * n_experts + e]
            dst0 = off_ref[ww * n_experts + e]

            def piece(offset, size):
                cp = pltpu.make_async_copy(
                    stage_ref.at[sl, pl.ds(pl.multiple_of(src0 + offset, BF16_ROWS), size)],
                    xs_ref.at[pl.ds(pl.multiple_of(dst0 + offset, BF16_ROWS), size)],
                    sem.at[sl, e])
                cp.start() if start else cp.wait()

            _for_each_piece(cnt_ref[ww * n_experts + e], sizes, piece)

    def zero_rows(dst, size, e, start):
        cp = pltpu.make_async_copy(zero_ref.at[pl.ds(0, size)],
                                   xs_ref.at[pl.ds(pl.multiple_of(dst, BF16_ROWS), size)],
                                   sem.at[2, e])
        cp.start() if start else cp.wait()

    @pl.when(w == 0)
    def _():
        zero_ref[...] = jnp.zeros(zero_ref.shape, BF16)

    d1 = drow_ref[0, 0:1, :]
    d2 = drow_ref[0, 1:2, :]
    g1 = drow_ref[0, 2:3, :]
    g2 = drow_ref[0, 3:4, :]
    r = lax.broadcasted_iota(jnp.int32, (rows, win), 0).astype(F32)
    m1 = r == d1
    m2 = r == d2
    onehot = jnp.logical_or(m1, m2).astype(BF16)
    xg = jnp.dot(onehot, hn_ref[...], preferred_element_type=F32)
    stage_ref[slot, :, 0:d] = xg.astype(BF16)
    gate = jnp.sum(jnp.where(m1, g1, 0.0) + jnp.where(m2, g2, 0.0), axis=-1, keepdims=True)
    hi = gate.astype(BF16).astype(F32)
    mid = (gate - hi).astype(BF16).astype(F32)
    lo = gate - hi - mid
    lane = lax.broadcasted_iota(jnp.int32, (rows, LANES), 1)
    terms = jnp.where(lane == 0, hi, jnp.where(lane == 1, mid, jnp.where(lane == 2, lo, 0.0)))
    stage_ref[slot, :, d:d + LANES] = terms.astype(BF16)

    segments(w, slot, True)

    @pl.when(w > 0)
    def _():
        segments(w - 1, 1 - slot, False)

    @pl.when(w == nwin - 1)
    def _():
        segments(w, slot, False)
        gap_sizes = _piece_sizes(tm // 2)
        for start in (True, False):
            for e in range(n_experts):
                _for_each_piece(fill_ref[n_experts + e], gap_sizes,
                                lambda offset, size, e=e: zero_rows(fill_ref[e] + offset, size, e, start))
        tail0 = fill_ref[2 * n_experts]
        ntail = fill_ref[2 * n_experts + 1]
        tail_rows = zero_ref.shape[0]

        def tail_start(j, c):
            zero_rows(tail0 + j * tail_rows, tail_rows, 0, True)
            return c

        def tail_wait(j, c):
            zero_rows(tail0 + j * tail_rows, tail_rows, 0, False)
            return c

        lax.fori_loop(0, ntail, tail_start, 0)
        lax.fori_loop(0, ntail, tail_wait, 0)


def _compact(seg, off, cnt_pad, fill, hn, drow, *, win, n_experts, cap, tm):
    T, D = hn.shape
    nwin = T // win
    rows = _round_up(TOP_K * win + n_experts * (BF16_ROWS - 1), BF16_ROWS)
    kern = functools.partial(_compact_kernel, win=win, n_experts=n_experts, rows=rows, tm=tm)
    return pl.pallas_call(
        kern,
        grid_spec=pltpu.PrefetchScalarGridSpec(
            num_scalar_prefetch=4,
            grid=(nwin,),
            in_specs=[
                pl.BlockSpec((win, D), lambda w, *_: (w, 0)),
                pl.BlockSpec((1, SUBLANES, win), lambda w, *_: (w, 0, 0)),
            ],
            out_specs=pl.BlockSpec(memory_space=pl.ANY),
            scratch_shapes=[pltpu.VMEM((2, rows, D + LANES), BF16),
                            pltpu.VMEM((tm // 2, D + LANES), BF16),
                            pltpu.SemaphoreType.DMA((3, n_experts))],
        ),
        out_shape=jax.ShapeDtypeStruct((cap, D + LANES), BF16),
        compiler_params=pltpu.CompilerParams(
            dimension_semantics=("arbitrary",), vmem_limit_bytes=VMEM_LIMIT_BYTES),
        name="moe_compact",
    )(seg, off, cnt_pad, fill, hn, drow)


def _expert_kernel(te_ref, tv_ref, xs_ref, wg_ref, wu_ref, wd_ref, ys_ref, *, ff_chunk):
    i = pl.program_id(0)
    d = ys_ref.shape[-1]

    @pl.when(tv_ref[i] > 0)
    def _():
        y = _swiglu_tile(xs_ref[:, 0:d], wg_ref.at[0], wu_ref.at[0], wd_ref.at[0], ff_chunk)
        gate = jnp.sum(xs_ref[:, d:d + LANES].astype(F32), axis=-1, keepdims=True)
        ys_ref[...] = (gate * y).astype(BF16)

    @pl.when(tv_ref[i] == 0)
    def _():
        ys_ref[...] = jnp.zeros(ys_ref.shape, BF16)


def _experts(tile_e, tile_v, xs, wg, wu, wd, *, tm, ff_chunk):
    cap = xs.shape[0]
    _, D, FF = wg.shape
    assert cap == tile_e.shape[0] * tm and xs.shape[1] == D + LANES
    kern = functools.partial(_expert_kernel, ff_chunk=ff_chunk)
    return pl.pallas_call(
        kern,
        grid_spec=pltpu.PrefetchScalarGridSpec(
            num_scalar_prefetch=2,
            grid=(cap // tm,),
            in_specs=[
                pl.BlockSpec((tm, D + LANES), lambda i, te, tv: (i, 0)),
                pl.BlockSpec((1, D, FF), lambda i, te, tv: (te[i], 0, 0)),
                pl.BlockSpec((1, D, FF), lambda i, te, tv: (te[i], 0, 0)),
                pl.BlockSpec((1, FF, D), lambda i, te, tv: (te[i], 0, 0)),
            ],
            out_specs=pl.BlockSpec((tm, D), lambda i, te, tv: (i, 0)),
        ),
        out_shape=jax.ShapeDtypeStruct((cap, D), BF16),
        compiler_params=pltpu.CompilerParams(
            dimension_semantics=("arbitrary",), vmem_limit_bytes=VMEM_LIMIT_BYTES),
        name="moe_experts",
    )(tile_e, tile_v, xs, wg.astype(BF16), wu.astype(BF16), wd.astype(BF16))


def _combine_kernel(seg_ref, off_ref, cnt_ref, x_ref, dcol_ref, fw_ref, ys_ref, o_ref,
                    stage_ref, sem, *, win, n_experts):
    w = pl.program_id(0)
    nwin = pl.num_programs(0)
    slot = w % 2
    rows = stage_ref.shape[1]
    sizes = _piece_sizes(win)

    def segments(ww, sl, start):
        for e in range(n_experts):
            src0 = off_ref[ww * n_experts + e]
            dst0 = seg_ref[ww * n_experts + e]

            def piece(offset, size):
                cp = pltpu.make_async_copy(
                    ys_ref.at[pl.ds(pl.multiple_of(src0 + offset, BF16_ROWS), size)],
                    stage_ref.at[sl, pl.ds(pl.multiple_of(dst0 + offset, BF16_ROWS), size)],
                    sem.at[sl, e])
                cp.start() if start else cp.wait()

            _for_each_piece(cnt_ref[ww * n_experts + e], sizes, piece)

    @pl.when(w == 0)
    def _():
        stage_ref[...] = jnp.zeros(stage_ref.shape, BF16)
        segments(0, 0, True)

    @pl.when(w + 1 < nwin)
    def _():
        segments(w + 1, 1 - slot, True)

    segments(w, slot, False)

    d1 = dcol_ref[:, 0:1]
    d2 = dcol_ref[:, 1:2]
    r = lax.broadcasted_iota(jnp.int32, (win, rows), 1).astype(F32)
    onehot = jnp.logical_or(r == d1, r == d2).astype(BF16)
    y = jnp.dot(onehot, stage_ref[slot], preferred_element_type=F32)
    o_ref[...] = _rms_norm(x_ref[...] + y, fw_ref[...])


def _combine(seg, off, cnt_pad, x2, dcol, fw, ys, *, win, n_experts):
    T, D = x2.shape
    rows = _round_up(TOP_K * win + n_experts * (BF16_ROWS - 1), BF16_ROWS)
    kern = functools.partial(_combine_kernel, win=win, n_experts=n_experts)
    return pl.pallas_call(
        kern,
        grid_spec=pltpu.PrefetchScalarGridSpec(
            num_scalar_prefetch=3,
            grid=(T // win,),
            in_specs=[
                pl.BlockSpec((win, D), lambda w, *_: (w, 0)),
                pl.BlockSpec((win, LANES), lambda w, *_: (w, 0)),
                pl.BlockSpec((1, D), lambda w, *_: (0, 0)),
                pl.BlockSpec(memory_space=pl.ANY),
            ],
            out_specs=pl.BlockSpec((win, D), lambda w, *_: (w, 0)),
            scratch_shapes=[pltpu.VMEM((2, rows, D), BF16),
                            pltpu.SemaphoreType.DMA((2, n_experts))],
        ),
        out_shape=jax.ShapeDtypeStruct((T, D), F32),
        compiler_params=pltpu.CompilerParams(
            dimension_semantics=("arbitrary",), vmem_limit_bytes=VMEM_LIMIT_BYTES),
        name="moe_combine",
    )(seg, off, cnt_pad, x2, dcol, fw.reshape(1, D).astype(F32), ys)


def _moe(x2, nw, router, wg, wu, wd, fw, *, win=MOE_WINDOW, tm=MOE_TILE):
    T, D = x2.shape
    E = router.shape[-1]
    nwin = T // win
    hn, dcol, drow, cnt = _route(x2, nw, router, win=win)

    cnt = cnt.reshape(nwin, LANES)[:, :E]
    cnt_pad = _round_up(cnt, BF16_ROWS)
    seg = jnp.cumsum(cnt_pad, axis=1) - cnt_pad
    total = jnp.sum(cnt_pad, axis=0)
    region = _round_up(total, tm)
    r_end = jnp.cumsum(region)
    r_start = r_end - region
    off = r_start[None, :] + jnp.cumsum(cnt_pad, axis=0) - cnt_pad
    max_rows = TOP_K * T + nwin * E * (BF16_ROWS - 1)
    ntiles_max = max_rows // tm + E
    cap = ntiles_max * tm
    ti = jnp.arange(ntiles_max)
    t_end = r_end // tm
    tile_e = jnp.minimum(jnp.sum(ti[:, None] >= t_end[None, :], axis=1), E - 1).astype(jnp.int32)
    tile_v = (ti < t_end[-1]).astype(jnp.int32)
    tail = jnp.stack([r_end[-1], (cap - r_end[-1]) // (tm // 2)])
    fill = jnp.concatenate([r_start + total, region - total, tail])
    flat = lambda a: a.reshape(-1).astype(jnp.int32)
    seg, off, cnt_pad, fill = flat(seg), flat(off), flat(cnt_pad), flat(fill)

    xs = _compact(seg, off, cnt_pad, fill, hn, drow, win=win, n_experts=E, cap=cap, tm=tm)
    ys = _experts(tile_e, tile_v, xs, wg, wu, wd, tm=tm, ff_chunk=wg.shape[-1])
    return _combine(seg, off, cnt_pad, x2, dcol, fw, ys, win=win, n_experts=E)


def kernel(x, norm_mix, w_in, conv_w, conv_b, conv_ng, conv_nb, sgu_ng, sgu_nb, sgu_w, sgu_b, w_out,
           norm_ffn, ffn_wg, ffn_wu, ffn_wd, moe_router, moe_wg, moe_wu, moe_wd, norm_final):
    B, S, D = x.shape
    depth = norm_mix.shape[0]
    assert depth == 2, "trunk is one dense layer followed by one MoE layer"
    mix = functools.partial(_mixer, ts=512, sub=256)
    mixer_params = (norm_mix, w_in, conv_w, conv_b, conv_ng, conv_nb, sgu_ng, sgu_nb, sgu_w, sgu_b, w_out)
    x, ffn_wg_b, ffn_wu_b, ffn_wd_b, moe_wg_b = mix(
        x, 0, *mixer_params, cast=((ffn_wg, 0), (ffn_wu, 0), (ffn_wd, 0), (moe_wg, 0)))
    FF = ffn_wg.shape[-1]
    x2, moe_wu_b = _ffn(x.reshape(B * S, D), 0, norm_ffn, ffn_wg_b, ffn_wu_b, ffn_wd_b,
                        tm=512, ff_chunk=FF, cast=((moe_wu, 0),))
    x, moe_wd_b = mix(x2.reshape(B, S, D), 1, *mixer_params, cast=((moe_wd, 0),))
    y2 = _moe(x.reshape(B * S, D), norm_ffn[1], moe_router[0], moe_wg_b.reshape(moe_wg[0].shape),
              moe_wu_b.reshape(moe_wu[0].shape), moe_wd_b.reshape(moe_wd[0].shape), norm_final)
    return y2.reshape(B, S, D)
```

```python
import functools

import jax
import jax.numpy as jnp
from jax import lax
from jax.experimental import pallas as pl
from jax.experimental.pallas import tpu as pltpu

F32 = jnp.float32
BF16 = jnp.bfloat16

EPS = 1e-6
CONV_WIDTH = 31
CONV_GROUPS = 8
SGU_HEADS = 4
CHUNK = 128
TOP_K = 2

SUBLANES = 8
LANES = 128
BF16_ROWS = 16
CONV_HALO = 32
VMEM_LIMIT_BYTES = 56 * 1024 * 1024
_MIX_STAGES = 3
_MIX_PIECES = SUBLANES

MOE_WINDOW = 512
MOE_TILE = 512


def _rms_norm(x, g):
    ms = jnp.mean(x * x, axis=-1, keepdims=True)
    return x * lax.rsqrt(ms + EPS) * g


def _gelu(x):
    return 0.5 * x * (1.0 + lax.erf(x * (2.0 ** -0.5)))


def _sigmoid(x):
    return 0.5 * jnp.tanh(0.5 * x) + 0.5


def _silu(x):
    h = 0.5 * x
    return h * jnp.tanh(h) + h


def _cast_plan(rows, nsteps):
    nblk = nsteps
    while rows % nblk or (rows // nblk) % BF16_ROWS:
        nblk //= 2
        assert nblk >= 1, rows
    return nblk, rows // nblk


def _cast_specs(items, nsteps, step_of):
    arrs, in_specs, out_specs, shapes = [], [], [], []
    for a, layer in items:
        a = a.reshape(a.shape[0], -1, a.shape[-1])
        nblk, blk = _cast_plan(a.shape[1], nsteps)
        blk_of = lambda *g, nblk=nblk: jnp.minimum(step_of(*g), nblk - 1)
        arrs.append(a)
        in_specs.append(pl.BlockSpec((None, blk, a.shape[2]),
                                     lambda *g, layer=layer, blk_of=blk_of: (layer, blk_of(*g), 0)))
        out_specs.append(pl.BlockSpec((blk, a.shape[2]), lambda *g, blk_of=blk_of: (blk_of(*g), 0)))
        shapes.append(jax.ShapeDtypeStruct(a.shape[1:], BF16))
    return arrs, in_specs, out_specs, shapes


def _cast_blocks(src_refs, dst_refs):
    for src, dst in zip(src_refs, dst_refs):
        dst[...] = src[...].astype(BF16)


_MIXER_INPUTS = 13


def _mixer_kernel(*refs, layer, ts, sub, cc, sc, n_cast):
    (x_ref, nw_all, win_f32_ref, cw_ref, cb_all, cng_all, cnb_all, sng_all, snb_all,
     sw_ref, sb_ref, wout_f32_ref, gavg_ref) = refs[:_MIXER_INPUTS]
    nw_ref, cb_ref, cng_ref, cnb_ref, sng_ref, snb_ref = (
        r.at[layer:layer + 1] for r in (nw_all, cb_all, cng_all, cnb_all, sng_all, snb_all))
    cast_in = refs[_MIXER_INPUTS:_MIXER_INPUTS + n_cast]
    o_ref = refs[_MIXER_INPUTS + n_cast]
    cast_out = refs[_MIXER_INPUTS + n_cast + 1:_MIXER_INPUTS + 2 * n_cast + 1]
    cbuf_ref, win_ref, wout_ref = refs[_MIXER_INPUTS + 2 * n_cast + 1:]
    s = pl.program_id(1)
    _cast_blocks(cast_in, cast_out)

    @pl.when(jnp.logical_and(pl.program_id(0) == 0, s == 0))
    def _():
        win_ref[...] = win_f32_ref[...].astype(BF16)
        wout_ref[...] = wout_f32_ref[...].astype(BF16)

    @pl.when(s == 0)
    def _():
        cbuf_ref[0:CONV_HALO, :] = jnp.zeros((CONV_HALO, cc), F32)

    row = lax.broadcasted_iota(jnp.int32, (CHUNK, CHUNK), 0)
    col = lax.broadcasted_iota(jnp.int32, (CHUNK, CHUNK), 1)
    tril = row >= col
    hd = sc // SGU_HEADS
    nq = -(-CONV_WIDTH // SUBLANES)

    nsub = ts // sub
    d_in = 2 * cc + 2 * sc
    cbw = d_in // _MIX_PIECES

    def sub_tile(j):
        r0 = j * sub
        x = x_ref[0, r0:r0 + sub, :]
        h = _rms_norm(x, nw_ref[...]).astype(BF16)
        zblk = []
        for k in range(_MIX_PIECES):
            zblk.append(jnp.dot(h, win_ref[:, k * cbw:(k + 1) * cbw], preferred_element_type=F32))
            yield
        z = jnp.concatenate(zblk, axis=1)
        a = z[:, :cc]
        gate = z[:, cc:2 * cc]
        u = z[:, 2 * cc:2 * cc + sc]
        v = z[:, 2 * cc + sc:]

        base = CONV_HALO + r0
        cbuf_ref[base:base + sub, :] = a * _sigmoid(gate)
        acc = jnp.broadcast_to(cb_ref[...], (sub, cc))
        for b in range(SUBLANES):
            zb = None
            for q in range(nq):
                d = SUBLANES * q + b
                if d >= CONV_WIDTH:
                    continue
                start = base - SUBLANES * (q + 1)
                term = (cw_ref[CONV_WIDTH - 1 - d]
                        * cbuf_ref[start:start + sub + SUBLANES, :])
                zb = term if zb is None else zb + term
            acc = acc + zb[SUBLANES - b:SUBLANES - b + sub, :]
            yield

        gavg = gavg_ref[...]
        acc_hi = acc.astype(BF16)
        acc_lo = (acc - acc_hi.astype(F32)).astype(BF16)
        mu = (jnp.dot(acc_hi, gavg, preferred_element_type=F32)
              + jnp.dot(acc_lo, gavg, preferred_element_type=F32))
        yield
        dev = acc - mu
        var = jnp.dot((dev * dev).astype(BF16), gavg, preferred_element_type=F32)
        yield
        cn = dev * lax.rsqrt(var + EPS) * cng_ref[...] + cnb_ref[...]
        c_out = _silu(cn).astype(BF16)
        yield

        u = _gelu(u)
        v = _gelu(v)
        yield
        g_cols = []
        for hh in range(SGU_HEADS):
            vh = v[:, hh * hd:(hh + 1) * hd]
            mu_h = jnp.mean(vh, axis=-1, keepdims=True)
            dh = vh - mu_h
            var_h = jnp.mean(dh * dh, axis=-1, keepdims=True)
            vn = (dh * lax.rsqrt(var_h + EPS) * sng_ref[:, hh * hd:(hh + 1) * hd]
                  + snb_ref[:, hh * hd:(hh + 1) * hd]).astype(BF16)
            ws = jnp.where(tril, sw_ref[hh], 0.0).astype(BF16)
            rows = []
            for ci in range(sub // CHUNK):
                sp = jnp.dot(ws, vn[ci * CHUNK:(ci + 1) * CHUNK, :], preferred_element_type=F32)
                rows.append(sp + sb_ref[hh])
            sp_h = rows[0] if len(rows) == 1 else jnp.concatenate(rows, axis=0)
            g_cols.append((u[:, hh * hd:(hh + 1) * hd] * sp_h).astype(BF16))
            if hh % 2 == 1:
                yield
        g_out = jnp.concatenate(g_cols, axis=1)

        yc = jnp.dot(c_out, wout_ref[0:cc, :], preferred_element_type=F32)
        yield
        yg = jnp.dot(g_out, wout_ref[cc:cc + sc, :], preferred_element_type=F32)
        o_ref[0, r0:r0 + sub, :] = x + (yc + yg)
        yield

    gens = [sub_tile(j) for j in range(nsub)]
    for tick in range(_MIX_PIECES * (nsub + _MIX_STAGES - 1)):
        for j in reversed(range(nsub)):
            if 0 <= tick - _MIX_PIECES * j < _MIX_PIECES * _MIX_STAGES:
                next(gens[j])

    cbuf_ref[0:CONV_HALO, :] = cbuf_ref[ts:ts + CONV_HALO, :]


def _mixer(x, layer, norm_mix, w_in, conv_w, conv_b, conv_ng, conv_nb, sgu_ng, sgu_nb, sgu_w, sgu_b,
           w_out, *, ts, sub, cast=()):
    B, S, D = x.shape
    L = w_in.shape[0]
    cc = conv_w.shape[-1]
    sc = sgu_ng.shape[-1]
    d_in = w_in.shape[-1]
    hd = sc // SGU_HEADS
    assert S % ts == 0 and ts % sub == 0 and sub % CHUNK == 0 and d_in == 2 * cc + 2 * sc
    ns = S // ts
    gs = cc // CONV_GROUPS
    gid = jnp.arange(cc) // gs
    gavg = jnp.where(gid[:, None] == gid[None, :], 1.0 / gs, 0.0).astype(BF16)
    sb_full = jnp.broadcast_to(sgu_b[:, :, :, None], (L, SGU_HEADS, CHUNK, hd))
    whole = lambda p: pl.BlockSpec(p.shape, lambda b, s: (0,) * p.ndim)
    layer_block = lambda *shape: pl.BlockSpec((None,) + shape, lambda b, s: (layer,) + (0,) * len(shape))
    cast_arrs, cast_in_specs, cast_out_specs, cast_shapes = _cast_specs(
        cast, B * ns, lambda b, s: b * ns + s)
    kern = functools.partial(_mixer_kernel, layer=layer, ts=ts, sub=sub, cc=cc, sc=sc, n_cast=len(cast))
    in_specs = [
        pl.BlockSpec((1, ts, D), lambda b, s: (b, s, 0)),
        whole(norm_mix),
        layer_block(D, d_in),
        layer_block(CONV_WIDTH, 1, cc),
        whole(conv_b), whole(conv_ng), whole(conv_nb),
        whole(sgu_ng), whole(sgu_nb),
        layer_block(SGU_HEADS, CHUNK, CHUNK),
        layer_block(SGU_HEADS, CHUNK, hd),
        layer_block(cc + sc, D),
        whole(gavg),
    ]
    assert len(in_specs) == _MIXER_INPUTS
    return pl.pallas_call(
        kern,
        grid=(B, ns),
        in_specs=in_specs + cast_in_specs,
        out_specs=[pl.BlockSpec((1, ts, D), lambda b, s: (b, s, 0))] + cast_out_specs,
        out_shape=[jax.ShapeDtypeStruct((B, S, D), F32)] + cast_shapes,
        scratch_shapes=[pltpu.VMEM((CONV_HALO + ts, cc), F32),
                        pltpu.VMEM((D, d_in), BF16), pltpu.VMEM((cc + sc, D), BF16)],
        compiler_params=pltpu.CompilerParams(
            dimension_semantics=("arbitrary", "arbitrary"),
            vmem_limit_bytes=VMEM_LIMIT_BYTES),
        name="mixer",
    )(x, norm_mix, w_in, conv_w, conv_b, conv_ng, conv_nb, sgu_ng, sgu_nb, sgu_w, sb_full, w_out,
      gavg, *cast_arrs)


def _swiglu_tile(h, wg_ref, wu_ref, wd_ref, ff_chunk):
    ff = wg_ref.shape[-1]
    out = None
    for c0 in range(0, ff, ff_chunk):
        c1 = min(ff, c0 + ff_chunk)
        g = jnp.dot(h, wg_ref[:, c0:c1], preferred_element_type=F32)
        u = jnp.dot(h, wu_ref[:, c0:c1], preferred_element_type=F32)
        a = (_silu(g) * u).astype(BF16)
        y = jnp.dot(a, wd_ref[c0:c1, :], preferred_element_type=F32)
        out = y if out is None else out + y
    return out


_FFN_INPUTS = 5


def _ffn_kernel(*refs, layer, ff_chunk, n_cast):
    x_ref, nw_all, wg_ref, wu_ref, wd_ref = refs[:_FFN_INPUTS]
    nw_ref = nw_all.at[layer:layer + 1]
    cast_in = refs[_FFN_INPUTS:_FFN_INPUTS + n_cast]
    o_ref = refs[_FFN_INPUTS + n_cast]
    cast_out = refs[_FFN_INPUTS + n_cast + 1:]
    _cast_blocks(cast_in, cast_out)
    x = x_ref[...]
    h = _rms_norm(x, nw_ref[...]).astype(BF16)
    o_ref[...] = x + _swiglu_tile(h, wg_ref, wu_ref, wd_ref, ff_chunk)


def _ffn(x2, layer, norm_ffn, wg, wu, wd, *, tm, ff_chunk, cast=()):
    T, D = x2.shape
    FF = wg.shape[-1]
    assert T % tm == 0 and wg.dtype == BF16
    cast_arrs, cast_in_specs, cast_out_specs, cast_shapes = _cast_specs(cast, T // tm, lambda i: i)
    kern = functools.partial(_ffn_kernel, layer=layer, ff_chunk=ff_chunk, n_cast=len(cast))
    in_specs = [
        pl.BlockSpec((tm, D), lambda i: (i, 0)),
        pl.BlockSpec(norm_ffn.shape, lambda i: (0, 0)),
        pl.BlockSpec((D, FF), lambda i: (0, 0)),
        pl.BlockSpec((D, FF), lambda i: (0, 0)),
        pl.BlockSpec((FF, D), lambda i: (0, 0)),
    ]
    assert len(in_specs) == _FFN_INPUTS
    return pl.pallas_call(
        kern,
        grid=(T // tm,),
        in_specs=in_specs + cast_in_specs,
        out_specs=[pl.BlockSpec((tm, D), lambda i: (i, 0))] + cast_out_specs,
        out_shape=[jax.ShapeDtypeStruct((T, D), F32)] + cast_shapes,
        compiler_params=pltpu.CompilerParams(
            dimension_semantics=("arbitrary",), vmem_limit_bytes=VMEM_LIMIT_BYTES),
        name="ffn",
    )(x2, norm_ffn, wg, wu, wd, *cast_arrs)


def _round_up(n, m):
    return (n + m - 1) // m * m


def _piece_sizes(largest):
    sizes = []
    s = largest
    while s >= BF16_ROWS:
        sizes.append(s)
        s //= 2
    assert sizes and sizes[-1] == BF16_ROWS
    return sizes


def _for_each_piece(count, sizes, fn):
    for s in sizes:
        offset = jnp.bitwise_and(count, -2 * s)

        @pl.when(jnp.bitwise_and(count, s) != 0)
        def _():
            fn(offset, s)


def _route_kernel(x_ref, nw_ref, rt_ref, ltri_ref, utri_ref,
                  hn_ref, dcol_ref, drow_ref, cnt_ref, *, n_experts):
    win = ltri_ref.shape[0]
    for k in range(x_ref.shape[0] // win):
        rows = slice(k * win, (k + 1) * win)
        _route_window(x_ref.at[rows], nw_ref, rt_ref, ltri_ref, utri_ref, hn_ref.at[rows],
                      dcol_ref.at[rows], drow_ref.at[k], cnt_ref.at[k], n_experts)


def _route_window(x_ref, nw_ref, rt_ref, ltri_ref, utri_ref, hn_ref, dcol_ref, drow_ref, cnt_ref,
                  n_experts):
    x = x_ref[...]
    h = _rms_norm(x, nw_ref[...]).astype(BF16)
    hn_ref[...] = h
    logits = jnp.dot(h, rt_ref[...], preferred_element_type=F32)
    lane = lax.broadcasted_iota(jnp.int32, logits.shape, 1)
    logits = jnp.where(lane < n_experts, logits, -jnp.inf)
    m1 = jnp.max(logits, axis=-1, keepdims=True)
    i1 = jnp.min(jnp.where(logits == m1, lane, LANES), axis=-1, keepdims=True)
    rest = jnp.where(lane == i1, -jnp.inf, logits)
    m2 = jnp.max(rest, axis=-1, keepdims=True)
    i2 = jnp.min(jnp.where(rest == m2, lane, LANES), axis=-1, keepdims=True)
    e2 = jnp.exp(m2 - m1)
    den = 1.0 + e2
    g1 = 1.0 / den
    g2 = e2 / den
    sel1 = lane == i1
    sel2 = lane == i2
    self = jnp.logical_or(sel1, sel2).astype(F32)
    rank = jnp.dot(ltri_ref[...], self.astype(BF16), preferred_element_type=F32)
    cnt = jnp.sum(self, axis=0, keepdims=True)
    cnt_pad = jnp.floor((cnt + (BF16_ROWS - 1.0)) * (1.0 / BF16_ROWS)) * BF16_ROWS
    seg = jnp.dot(jnp.broadcast_to(cnt_pad, (SUBLANES, LANES)).astype(BF16), utri_ref[...],
                  preferred_element_type=F32)[0:1, :]
    pos = seg + rank
    dest1 = jnp.sum(jnp.where(sel1, pos, 0.0), axis=-1, keepdims=True)
    dest2 = jnp.sum(jnp.where(sel2, pos, 0.0), axis=-1, keepdims=True)
    info = jnp.where(lane == 0, dest1, jnp.where(lane == 1, dest2,
                     jnp.where(lane == 2, g1, jnp.where(lane == 3, g2, -1.0))))
    dcol_ref[...] = info
    drow_ref[...] = jnp.transpose(info)[0:SUBLANES, :]
    cnt_ref[...] = cnt.astype(jnp.int32)


def _route(x2, nw, router, *, win, per_step=2):
    T, D = x2.shape
    E = router.shape[-1]
    assert T % (win * per_step) == 0 and E <= SUBLANES
    nwin = T // win
    blk = win * per_step
    rt = jnp.zeros((D, LANES), F32).at[:, :E].set(router).astype(BF16)
    r = jnp.arange(win)
    ltri = (r[:, None] > r[None, :]).astype(BF16)
    l = jnp.arange(LANES)
    utri = (l[:, None] < l[None, :]).astype(BF16)
    kern = functools.partial(_route_kernel, n_experts=E)
    return pl.pallas_call(
        kern,
        grid=(nwin // per_step,),
        in_specs=[
            pl.BlockSpec((blk, D), lambda w: (w, 0)),
            pl.BlockSpec((1, D), lambda w: (0, 0)),
            pl.BlockSpec((D, LANES), lambda w: (0, 0)),
            pl.BlockSpec((win, win), lambda w: (0, 0)),
            pl.BlockSpec((LANES, LANES), lambda w: (0, 0)),
        ],
        out_specs=[
            pl.BlockSpec((blk, D), lambda w: (w, 0)),
            pl.BlockSpec((blk, LANES), lambda w: (w, 0)),
            pl.BlockSpec((per_step, SUBLANES, win), lambda w: (w, 0, 0)),
            pl.BlockSpec((per_step, 1, LANES), lambda w: (w, 0, 0)),
        ],
        out_shape=[
            jax.ShapeDtypeStruct((T, D), BF16),
            jax.ShapeDtypeStruct((T, LANES), F32),
            jax.ShapeDtypeStruct((nwin, SUBLANES, win), F32),
            jax.ShapeDtypeStruct((nwin, 1, LANES), jnp.int32),
        ],
        compiler_params=pltpu.CompilerParams(
            dimension_semantics=("arbitrary",), vmem_limit_bytes=VMEM_LIMIT_BYTES),
        name="moe_route",
    )(x2, nw.reshape(1, D).astype(F32), rt, ltri, utri)


def _compact_kernel(seg_ref, off_ref, cnt_ref, fill_ref, hn_ref, drow_ref, xs_ref,
                    stage_ref, zero_ref, sem, *, win, n_experts, rows, tm):
    w = pl.program_id(0)
    nwin = pl.num_programs(0)
    slot = w % 2
    d = hn_ref.shape[-1]
    sizes = _piece_sizes(win)

    def segments(ww, sl, start):
        for e in range(n_experts):
            src0 = seg_ref[ww * n_experts + e]
            dst0 = off_ref[ww * n_experts + e]

            def piece(offset, size):
                cp = pltpu.make_async_copy(
                    stage_ref.at[sl, pl.ds(pl.multiple_of(src0 + offset, BF16_ROWS), size)],
                    xs_ref.at[pl.ds(pl.multiple_of(dst0 + offset, BF16_ROWS), size)],
                    sem.at[sl, e])
                cp.start() if start else cp.wait()

            _for_each_piece(cnt_ref[ww * n_experts + e], sizes, piece)

    def zero_rows(dst, size, e, start):
        cp = pltpu.make_async_copy(zero_ref.at[pl.ds(0, size)],
                                   xs_ref.at[pl.ds(pl.multiple_of(dst, BF16_ROWS), size)],
                                   sem.at[2, e])
        cp.start() if start else cp.wait()

    @pl.when(w == 0)
    def _():
        zero_ref[...] = jnp.zeros(zero_ref.shape, BF16)

    d1 = drow_ref[0, 0:1, :]
    d2 = drow_ref[0, 1:2, :]
    g1 = drow_ref[0, 2:3, :]
    g2 = drow_ref[0, 3:4, :]
    r = lax.broadcasted_iota(jnp.int32, (rows, win), 0).astype(F32)
    m1 = r == d1
    m2 = r == d2
    onehot = jnp.logical_or(m1, m2).astype(BF16)
    xg = jnp.dot(onehot, hn_ref[...], preferred_element_type=F32)
    stage_ref[slot, :, 0:d] = xg.astype(BF16)
    gate = jnp.sum(jnp.where(m1, g1, 0.0) + jnp.where(m2, g2, 0.0), axis=-1, keepdims=True)
    hi = gate.astype(BF16).astype(F32)
    mid = (gate - hi).astype(BF16).astype(F32)
    lo = gate - hi - mid
    lane = lax.broadcasted_iota(jnp.int32, (rows, LANES), 1)
    terms = jnp.where(lane == 0, hi, jnp.where(lane == 1, mid, jnp.where(lane == 2, lo, 0.0)))
    stage_ref[slot, :, d:d + LANES] = terms.astype(BF16)

    segments(w, slot, True)

    @pl.when(w > 0)
    def _():
        segments(w - 1, 1 - slot, False)

    @pl.when(w == nwin - 1)
    def _():
        segments(w, slot, False)
        gap_sizes = _piece_sizes(tm // 2)
        for start in (True, False):
            for e in range(n_experts):
                _for_each_piece(fill_ref[n_experts + e], gap_sizes,
                                lambda offset, size, e=e: zero_rows(fill_ref[e] + offset, size, e, start))
        tail0 = fill_ref[2 * n_experts]
        ntail = fill_ref[2 * n_experts + 1]
        tail_rows = zero_ref.shape[0]

        def tail_start(j, c):
            zero_rows(tail0 + j * tail_rows, tail_rows, 0, True)
            return c

        def tail_wait(j, c):
            zero_rows(tail0 + j * tail_rows, tail_rows, 0, False)
            return c

        lax.fori_loop(0, ntail, tail_start, 0)
        lax.fori_loop(0, ntail, tail_wait, 0)


def _compact(seg, off, cnt_pad, fill, hn, drow, *, win, n_experts, cap, tm):
    T, D = hn.shape
    nwin = T // win
    rows = _round_up(TOP_K * win + n_experts * (BF16_ROWS - 1), BF16_ROWS)
    kern = functools.partial(_compact_kernel, win=win, n_experts=n_experts, rows=rows, tm=tm)
    return pl.pallas_call(
        kern,
        grid_spec=pltpu.PrefetchScalarGridSpec(
            num_scalar_prefetch=4,
            grid=(nwin,),
            in_specs=[
                pl.BlockSpec((win, D), lambda w, *_: (w, 0)),
                pl.BlockSpec((1, SUBLANES, win), lambda w, *_: (w, 0, 0)),
            ],
            out_specs=pl.BlockSpec(memory_space=pl.ANY),
            scratch_shapes=[pltpu.VMEM((2, rows, D + LANES), BF16),
                            pltpu.VMEM((tm // 2, D + LANES), BF16),
                            pltpu.SemaphoreType.DMA((3, n_experts))],
        ),
        out_shape=jax.ShapeDtypeStruct((cap, D + LANES), BF16),
        compiler_params=pltpu.CompilerParams(
            dimension_semantics=("arbitrary",), vmem_limit_bytes=VMEM_LIMIT_BYTES),
        name="moe_compact",
    )(seg, off, cnt_pad, fill, hn, drow)


def _expert_kernel(te_ref, tv_ref, xs_ref, wg_ref, wu_ref, wd_ref, ys_ref, *, ff_chunk):
    i = pl.program_id(0)
    d = ys_ref.shape[-1]

    @pl.when(tv_ref[i] > 0)
    def _():
        y = _swiglu_tile(xs_ref[:, 0:d], wg_ref.at[0], wu_ref.at[0], wd_ref.at[0], ff_chunk)
        gate = jnp.sum(xs_ref[:, d:d + LANES].astype(F32), axis=-1, keepdims=True)
        ys_ref[...] = (gate * y).astype(BF16)

    @pl.when(tv_ref[i] == 0)
    def _():
        ys_ref[...] = jnp.zeros(ys_ref.shape, BF16)


def _experts(tile_e, tile_v, xs, wg, wu, wd, *, tm, ff_chunk):
    cap = xs.shape[0]
    _, D, FF = wg.shape
    assert cap == tile_e.shape[0] * tm and xs.shape[1] == D + LANES
    kern = functools.partial(_expert_kernel, ff_chunk=ff_chunk)
    return pl.pallas_call(
        kern,
        grid_spec=pltpu.PrefetchScalarGridSpec(
            num_scalar_prefetch=2,
            grid=(cap // tm,),
            in_specs=[
                pl.BlockSpec((tm, D + LANES), lambda i, te, tv: (i, 0)),
                pl.BlockSpec((1, D, FF), lambda i, te, tv: (te[i], 0, 0)),
                pl.BlockSpec((1, D, FF), lambda i, te, tv: (te[i], 0, 0)),
                pl.BlockSpec((1, FF, D), lambda i, te, tv: (te[i], 0, 0)),
            ],
            out_specs=pl.BlockSpec((tm, D), lambda i, te, tv: (i, 0)),
        ),
        out_shape=jax.ShapeDtypeStruct((cap, D), BF16),
        compiler_params=pltpu.CompilerParams(
            dimension_semantics=("arbitrary",), vmem_limit_bytes=VMEM_LIMIT_BYTES),
        name="moe_experts",
    )(tile_e, tile_v, xs, wg.astype(BF16), wu.astype(BF16), wd.astype(BF16))


def _combine_kernel(seg_ref, off_ref, cnt_ref, x_ref, dcol_ref, fw_ref, ys_ref, o_ref,
                    stage_ref, sem, *, win, n_experts):
    w = pl.program_id(0)
    nwin = pl.num_programs(0)
    slot = w % 2
    rows = stage_ref.shape[1]
    sizes = _piece_sizes(win)

    def segments(ww, sl, start):
        for e in range(n_experts):
            src0 = off_ref[ww * n_experts + e]
            dst0 = seg_ref[ww * n_experts + e]

            def piece(offset, size):
                cp = pltpu.make_async_copy(
                    ys_ref.at[pl.ds(pl.multiple_of(src0 + offset, BF16_ROWS), size)],
                    stage_ref.at[sl, pl.ds(pl.multiple_of(dst0 + offset, BF16_ROWS), size)],
                    sem.at[sl, e])
                cp.start() if start else cp.wait()

            _for_each_piece(cnt_ref[ww * n_experts + e], sizes, piece)

    @pl.when(w == 0)
    def _():
        stage_ref[...] = jnp.zeros(stage_ref.shape, BF16)
        segments(0, 0, True)

    @pl.when(w + 1 < nwin)
    def _():
        segments(w + 1, 1 - slot, True)

    segments(w, slot, False)

    d1 = dcol_ref[:, 0:1]
    d2 = dcol_ref[:, 1:2]
    r = lax.broadcasted_iota(jnp.int32, (win, rows), 1).astype(F32)
    onehot = jnp.logical_or(r == d1, r == d2).astype(BF16)
    y = jnp.dot(onehot, stage_ref[slot], preferred_element_type=F32)
    o_ref[...] = _rms_norm(x_ref[...] + y, fw_ref[...])


def _combine(seg, off, cnt_pad, x2, dcol, fw, ys, *, win, n_experts):
    T, D = x2.shape
    rows = _round_up(TOP_K * win + n_experts * (BF16_ROWS - 1), BF16_ROWS)
    kern = functools.partial(_combine_kernel, win=win, n_experts=n_experts)
    return pl.pallas_call(
        kern,
        grid_spec=pltpu.PrefetchScalarGridSpec(
            num_scalar_prefetch=3,
            grid=(T // win,),
            in_specs=[
                pl.BlockSpec((win, D), lambda w, *_: (w, 0)),
                pl.BlockSpec((win, LANES), lambda w, *_: (w, 0)),
                pl.BlockSpec((1, D), lambda w, *_: (0, 0)),
                pl.BlockSpec(memory_space=pl.ANY),
            ],
            out_specs=pl.BlockSpec((win, D), lambda w, *_: (w, 0)),
            scratch_shapes=[pltpu.VMEM((2, rows, D), BF16),
                            pltpu.SemaphoreType.DMA((2, n_experts))],
        ),
        out_shape=jax.ShapeDtypeStruct((T, D), F32),
        compiler_params=pltpu.CompilerParams(
            dimension_semantics=("arbitrary",), vmem_limit_bytes=VMEM_LIMIT_BYTES),
        name="moe_combine",
    )(seg, off, cnt_pad, x2, dcol, fw.reshape(1, D).astype(F32), ys)


def _moe(x2, nw, router, wg, wu, wd, fw, *, win=MOE_WINDOW, tm=MOE_TILE):
    T, D = x2.shape
    E = router.shape[-1]
    nwin = T // win
    hn, dcol, drow, cnt = _route(x2, nw, router, win=win)

    cnt = cnt.reshape(nwin, LANES)[:, :E]
    cnt_pad = _round_up(cnt, BF16_ROWS)
    seg = jnp.cumsum(cnt_pad, axis=1) - cnt_pad
    total = jnp.sum(cnt_pad, axis=0)
    region = _round_up(total, tm)
    r_end = jnp.cumsum(region)
    r_start = r_end - region
    off = r_start[None, :] + jnp.cumsum(cnt_pad, axis=0) - cnt_pad
    max_rows = TOP_K * T + nwin * E * (BF16_ROWS - 1)
    ntiles_max = max_rows // tm + E
    cap = ntiles_max * tm
    ti = jnp.arange(ntiles_max)
    t_end = r_end // tm
    tile_e = jnp.minimum(jnp.sum(ti[:, None] >= t_end[None, :], axis=1), E - 1).astype(jnp.int32)
    tile_v = (ti < t_end[-1]).astype(jnp.int32)
    tail = jnp.stack([r_end[-1], (cap - r_end[-1]) // (tm // 2)])
    fill = jnp.concatenate([r_start + total, region - total, tail])
    flat = lambda a: a.reshape(-1).astype(jnp.int32)
    seg, off, cnt_pad, fill = flat(seg), flat(off), flat(cnt_pad), flat(fill)

    xs = _compact(seg, off, cnt_pad, fill, hn, drow, win=win, n_experts=E, cap=cap, tm=tm)
    ys = _experts(tile_e, tile_v, xs, wg, wu, wd, tm=tm, ff_chunk=wg.shape[-1])
    return _combine(seg, off, cnt_pad, x2, dcol, fw, ys, win=win, n_experts=E)


def kernel(x, norm_mix, w_in, conv_w, conv_b, conv_ng, conv_nb, sgu_ng, sgu_nb, sgu_w, sgu_b, w_out,
           norm_ffn, ffn_wg, ffn_wu, ffn_wd, moe_router, moe_wg, moe_wu, moe_wd, norm_final):
    B, S, D = x.shape
    depth = norm_mix.shape[0]
    assert depth == 2, "trunk is one dense layer followed by one MoE layer"
    mix = functools.partial(_mixer, ts=512, sub=256)
    mixer_params = (norm_mix, w_in, conv_w, conv_b, conv_ng, conv_nb, sgu_ng, sgu_nb, sgu_w, sgu_b, w_out)
    x, ffn_wg_b, ffn_wu_b, ffn_wd_b, moe_wg_b = mix(
        x, 0, *mixer_params, cast=((ffn_wg, 0), (ffn_wu, 0), (ffn_wd, 0), (moe_wg, 0)))
    FF = ffn_wg.shape[-1]
    x2, moe_wu_b = _ffn(x.reshape(B * S, D), 0, norm_ffn, ffn_wg_b, ffn_wu_b, ffn_wd_b,
                        tm=512, ff_chunk=FF, cast=((moe_wu, 0),))
    x, moe_wd_b = mix(x2.reshape(B, S, D), 1, *mixer_params, cast=((moe_wd, 0),))
    y2 = _moe(x.reshape(B * S, D), norm_ffn[1], moe_router[0], moe_wg_b.reshape(moe_wg[0].shape),
              moe_wu_b.reshape(moe_wu[0].shape), moe_wd_b.reshape(moe_wd[0].shape), norm_final)
    return y2.reshape(B, S, D)
```

```python
import functools

import jax
import jax.numpy as jnp
from jax import lax
from jax.experimental import pallas as pl
from jax.experimental.pallas import tpu as pltpu

F32 = jnp.float32
BF16 = jnp.bfloat16

EPS = 1e-6
CONV_WIDTH = 31
CONV_GROUPS = 8
SGU_HEADS = 4
CHUNK = 128
TOP_K = 2

SUBLANES = 8
LANES = 128
BF16_ROWS = 16
CONV_HALO = 32
VMEM_LIMIT_BYTES = 56 * 1024 * 1024
_MIX_STAGES = 3
_MIX_PIECES = SUBLANES

MOE_WINDOW = 512
MOE_TILE = 512


def _rms_norm(x, g):
    ms = jnp.mean(x * x, axis=-1, keepdims=True)
    return x * lax.rsqrt(ms + EPS) * g


def _gelu(x):
    return 0.5 * x * (1.0 + lax.erf(x * (2.0 ** -0.5)))


def _sigmoid(x):
    return 0.5 * jnp.tanh(0.5 * x) + 0.5


def _silu(x):
    h = 0.5 * x
    return h * jnp.tanh(h) + h


def _cast_plan(rows, nsteps):
    nblk = nsteps
    while rows % nblk or (rows // nblk) % BF16_ROWS:
        nblk //= 2
        assert nblk >= 1, rows
    return nblk, rows // nblk


def _cast_specs(items, nsteps, step_of):
    arrs, in_specs, out_specs, shapes = [], [], [], []
    for a, layer in items:
        a = a.reshape(a.shape[0], -1, a.shape[-1])
        nblk, blk = _cast_plan(a.shape[1], nsteps)
        blk_of = lambda *g, nblk=nblk: jnp.minimum(step_of(*g), nblk - 1)
        arrs.append(a)
        in_specs.append(pl.BlockSpec((None, blk, a.shape[2]),
                                     lambda *g, layer=layer, blk_of=blk_of: (layer, blk_of(*g), 0)))
        out_specs.append(pl.BlockSpec((blk, a.shape[2]), lambda *g, blk_of=blk_of: (blk_of(*g), 0)))
        shapes.append(jax.ShapeDtypeStruct(a.shape[1:], BF16))
    return arrs, in_specs, out_specs, shapes


def _cast_blocks(src_refs, dst_refs):
    for src, dst in zip(src_refs, dst_refs):
        dst[...] = src[...].astype(BF16)


_MIXER_INPUTS = 13


def _mixer_kernel(*refs, layer, ts, sub, cc, sc, n_cast):
    (x_ref, nw_all, win_f32_ref, cw_ref, cb_all, cng_all, cnb_all, sng_all, snb_all,
     sw_ref, sb_ref, wout_f32_ref, gavg_ref) = refs[:_MIXER_INPUTS]
    nw_ref, cb_ref, cng_ref, cnb_ref, sng_ref, snb_ref = (
        r.at[layer:layer + 1] for r in (nw_all, cb_all, cng_all, cnb_all, sng_all, snb_all))
    cast_in = refs[_MIXER_INPUTS:_MIXER_INPUTS + n_cast]
    o_ref = refs[_MIXER_INPUTS + n_cast]
    cast_out = refs[_MIXER_INPUTS + n_cast + 1:_MIXER_INPUTS + 2 * n_cast + 1]
    cbuf_ref, win_ref, wout_ref = refs[_MIXER_INPUTS + 2 * n_cast + 1:]
    s = pl.program_id(1)
    _cast_blocks(cast_in, cast_out)

    @pl.when(jnp.logical_and(pl.program_id(0) == 0, s == 0))
    def _():
        win_ref[...] = win_f32_ref[...].astype(BF16)
        wout_ref[...] = wout_f32_ref[...].astype(BF16)

    @pl.when(s == 0)
    def _():
        cbuf_ref[0:CONV_HALO, :] = jnp.zeros((CONV_HALO, cc), F32)

    row = lax.broadcasted_iota(jnp.int32, (CHUNK, CHUNK), 0)
    col = lax.broadcasted_iota(jnp.int32, (CHUNK, CHUNK), 1)
    tril = row >= col
    hd = sc // SGU_HEADS
    nq = -(-CONV_WIDTH // SUBLANES)

    nsub = ts // sub
    d_in = 2 * cc + 2 * sc
    cbw = d_in // _MIX_PIECES

    def sub_tile(j):
        r0 = j * sub
        x = x_ref[0, r0:r0 + sub, :]
        h = _rms_norm(x, nw_ref[...]).astype(BF16)
        zblk = []
        for k in range(_MIX_PIECES):
            zblk.append(jnp.dot(h, win_ref[:, k * cbw:(k + 1) * cbw], preferred_element_type=F32))
            yield
        z = jnp.concatenate(zblk, axis=1)
        a = z[:, :cc]
        gate = z[:, cc:2 * cc]
        u = z[:, 2 * cc:2 * cc + sc]
        v = z[:, 2 * cc + sc:]

        base = CONV_HALO + r0
        cbuf_ref[base:base + sub, :] = a * _sigmoid(gate)
        acc = jnp.broadcast_to(cb_ref[...], (sub, cc))
        for b in range(SUBLANES):
            zb = None
            for q in range(nq):
                d = SUBLANES * q + b
                if d >= CONV_WIDTH:
                    continue
                start = base - SUBLANES * (q + 1)
                term = (cw_ref[CONV_WIDTH - 1 - d]
                        * cbuf_ref[start:start + sub + SUBLANES, :])
                zb = term if zb is None else zb + term
            acc = acc + zb[SUBLANES - b:SUBLANES - b + sub, :]
            yield

        gavg = gavg_ref[...]
        acc_hi = acc.astype(BF16)
        acc_lo = (acc - acc_hi.astype(F32)).astype(BF16)
        mu = (jnp.dot(acc_hi, gavg, preferred_element_type=F32)
              + jnp.dot(acc_lo, gavg, preferred_element_type=F32))
        yield
        dev = acc - mu
        var = jnp.dot((dev * dev).astype(BF16), gavg, preferred_element_type=F32)
        yield
        cn = dev * lax.rsqrt(var + EPS) * cng_ref[...] + cnb_ref[...]
        c_out = _silu(cn).astype(BF16)
        yield

        u = _gelu(u)
        v = _gelu(v)
        yield
        g_cols = []
        for hh in range(SGU_HEADS):
            vh = v[:, hh * hd:(hh + 1) * hd]
            mu_h = jnp.mean(vh, axis=-1, keepdims=True)
            dh = vh - mu_h
            var_h = jnp.mean(dh * dh, axis=-1, keepdims=True)
            vn = (dh * lax.rsqrt(var_h + EPS) * sng_ref[:, hh * hd:(hh + 1) * hd]
                  + snb_ref[:, hh * hd:(hh + 1) * hd]).astype(BF16)
            ws = jnp.where(tril, sw_ref[hh], 0.0).astype(BF16)
            rows = []
            for ci in range(sub // CHUNK):
                sp = jnp.dot(ws, vn[ci * CHUNK:(ci + 1) * CHUNK, :], preferred_element_type=F32)
                rows.append(sp + sb_ref[hh])
            sp_h = rows[0] if len(rows) == 1 else jnp.concatenate(rows, axis=0)
            g_cols.append((u[:, hh * hd:(hh + 1) * hd] * sp_h).astype(BF16))
            if hh % 2 == 1:
                yield
        g_out = jnp.concatenate(g_cols, axis=1)

        yc = jnp.dot(c_out, wout_ref[0:cc, :], preferred_element_type=F32)
        yield
        yg = jnp.dot(g_out, wout_ref[cc:cc + sc, :], preferred_element_type=F32)
        o_ref[0, r0:r0 + sub, :] = x + (yc + yg)
        yield

    gens = [sub_tile(j) for j in range(nsub)]
    for tick in range(_MIX_PIECES * (nsub + _MIX_STAGES - 1)):
        for j in reversed(range(nsub)):
            if 0 <= tick - _MIX_PIECES * j < _MIX_PIECES * _MIX_STAGES:
                next(gens[j])

    cbuf_ref[0:CONV_HALO, :] = cbuf_ref[ts:ts + CONV_HALO, :]


def _mixer(x, layer, norm_mix, w_in, conv_w, conv_b, conv_ng, conv_nb, sgu_ng, sgu_nb, sgu_w, sgu_b,
           w_out, *, ts, sub, cast=()):
    B, S, D = x.shape
    L = w_in.shape[0]
    cc = conv_w.shape[-1]
    sc = sgu_ng.shape[-1]
    d_in = w_in.shape[-1]
    hd = sc // SGU_HEADS
    assert S % ts == 0 and ts % sub == 0 and sub % CHUNK == 0 and d_in == 2 * cc + 2 * sc
    ns = S // ts
    gs = cc // CONV_GROUPS
    gid = jnp.arange(cc) // gs
    gavg = jnp.where(gid[:, None] == gid[None, :], 1.0 / gs, 0.0).astype(BF16)
    sb_full = jnp.broadcast_to(sgu_b[:, :, :, None], (L, SGU_HEADS, CHUNK, hd))
    whole = lambda p: pl.BlockSpec(p.shape, lambda b, s: (0,) * p.ndim)
    layer_block = lambda *shape: pl.BlockSpec((None,) + shape, lambda b, s: (layer,) + (0,) * len(shape))
    cast_arrs, cast_in_specs, cast_out_specs, cast_shapes = _cast_specs(
        cast, B * ns, lambda b, s: b * ns + s)
    kern = functools.partial(_mixer_kernel, layer=layer, ts=ts, sub=sub, cc=cc, sc=sc, n_cast=len(cast))
    in_specs = [
        pl.BlockSpec((1, ts, D), lambda b, s: (b, s, 0)),
        whole(norm_mix),
        layer_block(D, d_in),
        layer_block(CONV_WIDTH, 1, cc),
        whole(conv_b), whole(conv_ng), whole(conv_nb),
        whole(sgu_ng), whole(sgu_nb),
        layer_block(SGU_HEADS, CHUNK, CHUNK),
        layer_block(SGU_HEADS, CHUNK, hd),
        layer_block(cc + sc, D),
        whole(gavg),
    ]
    assert len(in_specs) == _MIXER_INPUTS
    return pl.pallas_call(
        kern,
        grid=(B, ns),
        in_specs=in_specs + cast_in_specs,
        out_specs=[pl.BlockSpec((1, ts, D), lambda b, s: (b, s, 0))] + cast_out_specs,
        out_shape=[jax.ShapeDtypeStruct((B, S, D), F32)] + cast_shapes,
        scratch_shapes=[pltpu.VMEM((CONV_HALO + ts, cc), F32),
                        pltpu.VMEM((D, d_in), BF16), pltpu.VMEM((cc + sc, D), BF16)],
        compiler_params=pltpu.CompilerParams(
            dimension_semantics=("arbitrary", "arbitrary"),
            vmem_limit_bytes=VMEM_LIMIT_BYTES),
        name="mixer",
    )(x, norm_mix, w_in, conv_w, conv_b, conv_ng, conv_nb, sgu_ng, sgu_nb, sgu_w, sb_full, w_out,
      gavg, *cast_arrs)


def _swiglu_tile(h, wg_ref, wu_ref, wd_ref, ff_chunk):
    ff = wg_ref.shape[-1]
    out = None
    for c0 in range(0, ff, ff_chunk):
        c1 = min(ff, c0 + ff_chunk)
        g = jnp.dot(h, wg_ref[:, c0:c1], preferred_element_type=F32)
        u = jnp.dot(h, wu_ref[:, c0:c1], preferred_element_type=F32)
        a = (_silu(g) * u).astype(BF16)
        y = jnp.dot(a, wd_ref[c0:c1, :], preferred_element_type=F32)
        out = y if out is None else out + y
    return out


_FFN_INPUTS = 5


def _ffn_kernel(*refs, layer, ff_chunk, n_cast):
    x_ref, nw_all, wg_ref, wu_ref, wd_ref = refs[:_FFN_INPUTS]
    nw_ref = nw_all.at[layer:layer + 1]
    cast_in = refs[_FFN_INPUTS:_FFN_INPUTS + n_cast]
    o_ref = refs[_FFN_INPUTS + n_cast]
    cast_out = refs[_FFN_INPUTS + n_cast + 1:]
    _cast_blocks(cast_in, cast_out)
    x = x_ref[...]
    h = _rms_norm(x, nw_ref[...]).astype(BF16)
    o_ref[...] = x + _swiglu_tile(h, wg_ref, wu_ref, wd_ref, ff_chunk)


def _ffn(x2, layer, norm_ffn, wg, wu, wd, *, tm, ff_chunk, cast=()):
    T, D = x2.shape
    FF = wg.shape[-1]
    assert T % tm == 0 and wg.dtype == BF16
    cast_arrs, cast_in_specs, cast_out_specs, cast_shapes = _cast_specs(cast, T // tm, lambda i: i)
    kern = functools.partial(_ffn_kernel, layer=layer, ff_chunk=ff_chunk, n_cast=len(cast))
    in_specs = [
        pl.BlockSpec((tm, D), lambda i: (i, 0)),
        pl.BlockSpec(norm_ffn.shape, lambda i: (0, 0)),
        pl.BlockSpec((D, FF), lambda i: (0, 0)),
        pl.BlockSpec((D, FF), lambda i: (0, 0)),
        pl.BlockSpec((FF, D), lambda i: (0, 0)),
    ]
    assert len(in_specs) == _FFN_INPUTS
    return pl.pallas_call(
        kern,
        grid=(T // tm,),
        in_specs=in_specs + cast_in_specs,
        out_specs=[pl.BlockSpec((tm, D), lambda i: (i, 0))] + cast_out_specs,
        out_shape=[jax.ShapeDtypeStruct((T, D), F32)] + cast_shapes,
        compiler_params=pltpu.CompilerParams(
            dimension_semantics=("arbitrary",), vmem_limit_bytes=VMEM_LIMIT_BYTES),
        name="ffn",
    )(x2, norm_ffn, wg, wu, wd, *cast_arrs)


def _round_up(n, m):
    return (n + m - 1) // m * m


def _piece_sizes(largest):
    sizes = []
    s = largest
    while s >= BF16_ROWS:
        sizes.append(s)
        s //= 2
    assert sizes and sizes[-1] == BF16_ROWS
    return sizes


def _for_each_piece(count, sizes, fn):
    for s in sizes:
        offset = jnp.bitwise_and(count, -2 * s)

        @pl.when(jnp.bitwise_and(count, s) != 0)
        def _():
            fn(offset, s)


def _route_kernel(x_ref, nw_ref, rt_ref, hn_ref, dcol_ref, drow_ref, cnt_ref, *, win, n_experts):
    for k in range(x_ref.shape[0] // win):
        rows = slice(k * win, (k + 1) * win)
        _route_window(x_ref.at[rows], nw_ref, rt_ref, hn_ref.at[rows],
                      dcol_ref.at[rows], drow_ref.at[k], cnt_ref.at[k], n_experts)


def _route_window(x_ref, nw_ref, rt_ref, hn_ref, dcol_ref, drow_ref, cnt_ref, n_experts):
    x = x_ref[...]
    h = _rms_norm(x, nw_ref[...]).astype(BF16)
    hn_ref[...] = h
    logits = jnp.dot(h, rt_ref[...], preferred_element_type=F32)
    lane = lax.broadcasted_iota(jnp.int32, logits.shape, 1)
    lt = jnp.transpose(jnp.where(lane < n_experts, logits, -jnp.inf))[0:SUBLANES, :]
    win = lt.shape[1]
    sub = lax.broadcasted_iota(jnp.int32, lt.shape, 0)
    m1 = jnp.max(lt, axis=0, keepdims=True)
    i1 = jnp.min(jnp.where(lt == m1, sub, SUBLANES), axis=0, keepdims=True)
    rest = jnp.where(sub == i1, -jnp.inf, lt)
    m2 = jnp.max(rest, axis=0, keepdims=True)
    i2 = jnp.min(jnp.where(rest == m2, sub, SUBLANES), axis=0, keepdims=True)
    e2 = jnp.exp(m2 - m1)
    den = 1.0 + e2
    g1 = 1.0 / den
    g2 = e2 / den
    sel1 = sub == i1
    sel2 = sub == i2
    self = jnp.logical_or(sel1, sel2).astype(F32)
    tok = lax.broadcasted_iota(jnp.int32, lt.shape, 1)
    inc = self
    step = 1
    while step < win:
        inc = inc + jnp.where(tok >= step, pltpu.roll(inc, step, axis=1), 0.0)
        step *= 2
    rank = inc - self
    cnt = jnp.sum(self, axis=1, keepdims=True)
    cnt_pad = jnp.floor((cnt + (BF16_ROWS - 1.0)) * (1.0 / BF16_ROWS)) * BF16_ROWS
    seg = jnp.zeros_like(cnt_pad)
    for e in range(n_experts - 1):
        seg = seg + jnp.where(sub[:, 0:1] > e, cnt_pad[e:e + 1, :], 0.0)
    pos = seg + rank
    dest1 = jnp.sum(jnp.where(sel1, pos, 0.0), axis=0, keepdims=True)
    dest2 = jnp.sum(jnp.where(sel2, pos, 0.0), axis=0, keepdims=True)
    info = jnp.where(sub == 0, dest1, jnp.where(sub == 1, dest2,
                     jnp.where(sub == 2, g1, jnp.where(sub == 3, g2, -1.0))))
    drow_ref[...] = info
    cols = jnp.concatenate([info, jnp.broadcast_to(cnt, (SUBLANES, win)),
                            jnp.full((LANES - 2 * SUBLANES, win), -1.0, F32)], axis=0)
    cols = jnp.transpose(cols)
    dcol_ref[...] = cols
    cnt_ref[...] = cols[0:1, :].astype(jnp.int32)


def _route(x2, nw, router, *, win, per_step=4):
    T, D = x2.shape
    E = router.shape[-1]
    assert T % (win * per_step) == 0 and E <= SUBLANES
    nwin = T // win
    blk = win * per_step
    rt = jnp.zeros((D, LANES), F32).at[:, :E].set(router).astype(BF16)
    kern = functools.partial(_route_kernel, win=win, n_experts=E)
    return pl.pallas_call(
        kern,
        grid=(nwin // per_step,),
        in_specs=[
            pl.BlockSpec((blk, D), lambda w: (w, 0)),
            pl.BlockSpec((1, D), lambda w: (0, 0)),
            pl.BlockSpec((D, LANES), lambda w: (0, 0)),
        ],
        out_specs=[
            pl.BlockSpec((blk, D), lambda w: (w, 0)),
            pl.BlockSpec((blk, LANES), lambda w: (w, 0)),
            pl.BlockSpec((per_step, SUBLANES, win), lambda w: (w, 0, 0)),
            pl.BlockSpec((per_step, 1, LANES), lambda w: (w, 0, 0)),
        ],
        out_shape=[
            jax.ShapeDtypeStruct((T, D), BF16),
            jax.ShapeDtypeStruct((T, LANES), F32),
            jax.ShapeDtypeStruct((nwin, SUBLANES, win), F32),
            jax.ShapeDtypeStruct((nwin, 1, LANES), jnp.int32),
        ],
        compiler_params=pltpu.CompilerParams(
            dimension_semantics=("arbitrary",), vmem_limit_bytes=VMEM_LIMIT_BYTES),
        name="moe_route",
    )(x2, nw.reshape(1, D).astype(F32), rt)


def _compact_kernel(seg_ref, off_ref, cnt_ref, fill_ref, hn_ref, drow_ref, xs_ref,
                    stage_ref, zero_ref, sem, *, win, n_experts, rows, tm):
    w = pl.program_id(0)
    nwin = pl.num_programs(0)
    slot = w % 2
    d = hn_ref.shape[-1]
    sizes = _piece_sizes(win)

    def segments(ww, sl, start):
        for e in range(n_experts):
            src0 = seg_ref[ww * n_experts + e]
            dst0 = off_ref[ww * n_experts + e]

            def piece(offset, size):
                cp = pltpu.make_async_copy(
                    stage_ref.at[sl, pl.ds(pl.multiple_of(src0 + offset, BF16_ROWS), size)],
                    xs_ref.at[pl.ds(pl.multiple_of(dst0 + offset, BF16_ROWS), size)],
                    sem.at[sl, e])
                cp.start() if start else cp.wait()

            _for_each_piece(cnt_ref[ww * n_experts + e], sizes, piece)

    def zero_rows(dst, size, e, start):
        cp = pltpu.make_async_copy(zero_ref.at[pl.ds(0, size)],
                                   xs_ref.at[pl.ds(pl.multiple_of(dst, BF16_ROWS), size)],
                                   sem.at[2, e])
        cp.start() if start else cp.wait()

    @pl.when(w == 0)
    def _():
        zero_ref[...] = jnp.zeros(zero_ref.shape, BF16)

    d1 = drow_ref[0, 0:1, :]
    d2 = drow_ref[0, 1:2, :]
    g1 = drow_ref[0, 2:3, :]
    g2 = drow_ref[0, 3:4, :]
    r = lax.broadcasted_iota(jnp.int32, (rows, win), 0).astype(F32)
    m1 = r == d1
    m2 = r == d2
    onehot = jnp.logical_or(m1, m2).astype(BF16)
    xg = jnp.dot(onehot, hn_ref[...], preferred_element_type=F32)
    stage_ref[slot, :, 0:d] = xg.astype(BF16)
    gate = jnp.sum(jnp.where(m1, g1, 0.0) + jnp.where(m2, g2, 0.0), axis=-1, keepdims=True)
    hi = gate.astype(BF16).astype(F32)
    mid = (gate - hi).astype(BF16).astype(F32)
    lo = gate - hi - mid
    lane = lax.broadcasted_iota(jnp.int32, (rows, LANES), 1)
    terms = jnp.where(lane == 0, hi, jnp.where(lane == 1, mid, jnp.where(lane == 2, lo, 0.0)))
    stage_ref[slot, :, d:d + LANES] = terms.astype(BF16)

    segments(w, slot, True)

    @pl.when(w > 0)
    def _():
        segments(w - 1, 1 - slot, False)

    @pl.when(w == nwin - 1)
    def _():
        segments(w, slot, False)
        gap_sizes = _piece_sizes(tm // 2)
        for start in (True, False):
            for e in range(n_experts):
                _for_each_piece(fill_ref[n_experts + e], gap_sizes,
                                lambda offset, size, e=e: zero_rows(fill_ref[e] + offset, size, e, start))
        tail0 = fill_ref[2 * n_experts]
        ntail = fill_ref[2 * n_experts + 1]
        tail_rows = zero_ref.shape[0]

        def tail_start(j, c):
            zero_rows(tail0 + j * tail_rows, tail_rows, 0, True)
            return c

        def tail_wait(j, c):
            zero_rows(tail0 + j * tail_rows, tail_rows, 0, False)
            return c

        lax.fori_loop(0, ntail, tail_start, 0)
        lax.fori_loop(0, ntail, tail_wait, 0)


def _compact(seg, off, cnt_pad, fill, hn, drow, *, win, n_experts, cap, tm):
    T, D = hn.shape
    nwin = T // win
    rows = _round_up(TOP_K * win + n_experts * (BF16_ROWS - 1), BF16_ROWS)
    kern = functools.partial(_compact_kernel, win=win, n_experts=n_experts, rows=rows, tm=tm)
    return pl.pallas_call(
        kern,
        grid_spec=pltpu.PrefetchScalarGridSpec(
            num_scalar_prefetch=4,
            grid=(nwin,),
            in_specs=[
                pl.BlockSpec((win, D), lambda w, *_: (w, 0)),
                pl.BlockSpec((1, SUBLANES, win), lambda w, *_: (w, 0, 0)),
            ],
            out_specs=pl.BlockSpec(memory_space=pl.ANY),
            scratch_shapes=[pltpu.VMEM((2, rows, D + LANES), BF16),
                            pltpu.VMEM((tm // 2, D + LANES), BF16),
                            pltpu.SemaphoreType.DMA((3, n_experts))],
        ),
        out_shape=jax.ShapeDtypeStruct((cap, D + LANES), BF16),
        compiler_params=pltpu.CompilerParams(
            dimension_semantics=("arbitrary",), vmem_limit_bytes=VMEM_LIMIT_BYTES),
        name="moe_compact",
    )(seg, off, cnt_pad, fill, hn, drow)


def _expert_kernel(te_ref, tv_ref, xs_ref, wg_ref, wu_ref, wd_ref, ys_ref, *, ff_chunk):
    i = pl.program_id(0)
    d = ys_ref.shape[-1]

    @pl.when(tv_ref[i] > 0)
    def _():
        y = _swiglu_tile(xs_ref[:, 0:d], wg_ref.at[0], wu_ref.at[0], wd_ref.at[0], ff_chunk)
        gate = jnp.sum(xs_ref[:, d:d + LANES].astype(F32), axis=-1, keepdims=True)
        ys_ref[...] = (gate * y).astype(BF16)

    @pl.when(tv_ref[i] == 0)
    def _():
        ys_ref[...] = jnp.zeros(ys_ref.shape, BF16)


def _experts(tile_e, tile_v, xs, wg, wu, wd, *, tm, ff_chunk):
    cap = xs.shape[0]
    _, D, FF = wg.shape
    assert cap == tile_e.shape[0] * tm and xs.shape[1] == D + LANES
    kern = functools.partial(_expert_kernel, ff_chunk=ff_chunk)
    return pl.pallas_call(
        kern,
        grid_spec=pltpu.PrefetchScalarGridSpec(
            num_scalar_prefetch=2,
            grid=(cap // tm,),
            in_specs=[
                pl.BlockSpec((tm, D + LANES), lambda i, te, tv: (i, 0)),
                pl.BlockSpec((1, D, FF), lambda i, te, tv: (te[i], 0, 0)),
                pl.BlockSpec((1, D, FF), lambda i, te, tv: (te[i], 0, 0)),
                pl.BlockSpec((1, FF, D), lambda i, te, tv: (te[i], 0, 0)),
            ],
            out_specs=pl.BlockSpec((tm, D), lambda i, te, tv: (i, 0)),
        ),
        out_shape=jax.ShapeDtypeStruct((cap, D), BF16),
        compiler_params=pltpu.CompilerParams(
            dimension_semantics=("arbitrary",), vmem_limit_bytes=VMEM_LIMIT_BYTES),
        name="moe_experts",
    )(tile_e, tile_v, xs, wg.astype(BF16), wu.astype(BF16), wd.astype(BF16))


def _combine_kernel(seg_ref, off_ref, cnt_ref, x_ref, dcol_ref, fw_ref, ys_ref, o_ref,
                    stage_ref, sem, *, win, n_experts):
    w = pl.program_id(0)
    nwin = pl.num_programs(0)
    slot = w % 2
    rows = stage_ref.shape[1]
    sizes = _piece_sizes(win)

    def segments(ww, sl, start):
        for e in range(n_experts):
            src0 = off_ref[ww * n_experts + e]
            dst0 = seg_ref[ww * n_experts + e]

            def piece(offset, size):
                cp = pltpu.make_async_copy(
                    ys_ref.at[pl.ds(pl.multiple_of(src0 + offset, BF16_ROWS), size)],
                    stage_ref.at[sl, pl.ds(pl.multiple_of(dst0 + offset, BF16_ROWS), size)],
                    sem.at[sl, e])
                cp.start() if start else cp.wait()

            _for_each_piece(cnt_ref[ww * n_experts + e], sizes, piece)

    @pl.when(w == 0)
    def _():
        stage_ref[...] = jnp.zeros(stage_ref.shape, BF16)
        segments(0, 0, True)

    @pl.when(w + 1 < nwin)
    def _():
        segments(w + 1, 1 - slot, True)

    segments(w, slot, False)

    d1 = dcol_ref[:, 0:1]
    d2 = dcol_ref[:, 1:2]
    r = lax.broadcasted_iota(jnp.int32, (win, rows), 1).astype(F32)
    onehot = jnp.logical_or(r == d1, r == d2).astype(BF16)
    y = jnp.dot(onehot, stage_ref[slot], preferred_element_type=F32)
    o_ref[...] = _rms_norm(x_ref[...] + y, fw_ref[...])


def _combine(seg, off, cnt_pad, x2, dcol, fw, ys, *, win, n_experts):
    T, D = x2.shape
    rows = _round_up(TOP_K * win + n_experts * (BF16_ROWS - 1), BF16_ROWS)
    kern = functools.partial(_combine_kernel, win=win, n_experts=n_experts)
    return pl.pallas_call(
        kern,
        grid_spec=pltpu.PrefetchScalarGridSpec(
            num_scalar_prefetch=3,
            grid=(T // win,),
            in_specs=[
                pl.BlockSpec((win, D), lambda w, *_: (w, 0)),
                pl.BlockSpec((win, LANES), lambda w, *_: (w, 0)),
                pl.BlockSpec((1, D), lambda w, *_: (0, 0)),
                pl.BlockSpec(memory_space=pl.ANY),
            ],
            out_specs=pl.BlockSpec((win, D), lambda w, *_: (w, 0)),
            scratch_shapes=[pltpu.VMEM((2, rows, D), BF16),
                            pltpu.SemaphoreType.DMA((2, n_experts))],
        ),
        out_shape=jax.ShapeDtypeStruct((T, D), F32),
        compiler_params=pltpu.CompilerParams(
            dimension_semantics=("arbitrary",), vmem_limit_bytes=VMEM_LIMIT_BYTES),
        name="moe_combine",
    )(seg, off, cnt_pad, x2, dcol, fw.reshape(1, D).astype(F32), ys)


def _moe(x2, nw, router, wg, wu, wd, fw, *, win=MOE_WINDOW, tm=MOE_TILE):
    T, D = x2.shape
    E = router.shape[-1]
    nwin = T // win
    hn, dcol, drow, cnt = _route(x2, nw, router, win=win)

    cnt = cnt.reshape(nwin, LANES)[:, SUBLANES:SUBLANES + E]
    cnt_pad = _round_up(cnt, BF16_ROWS)
    seg = jnp.cumsum(cnt_pad, axis=1) - cnt_pad
    total = jnp.sum(cnt_pad, axis=0)
    region = _round_up(total, tm)
    r_end = jnp.cumsum(region)
    r_start = r_end - region
    off = r_start[None, :] + jnp.cumsum(cnt_pad, axis=0) - cnt_pad
    max_rows = TOP_K * T + nwin * E * (BF16_ROWS - 1)
    ntiles_max = max_rows // tm + E
    cap = ntiles_max * tm
    ti = jnp.arange(ntiles_max)
    t_end = r_end // tm
    tile_e = jnp.minimum(jnp.sum(ti[:, None] >= t_end[None, :], axis=1), E - 1).astype(jnp.int32)
    tile_v = (ti < t_end[-1]).astype(jnp.int32)
    tail = jnp.stack([r_end[-1], (cap - r_end[-1]) // (tm // 2)])
    fill = jnp.concatenate([r_start + total, region - total, tail])
    flat = lambda a: a.reshape(-1).astype(jnp.int32)
    seg, off, cnt_pad, fill = flat(seg), flat(off), flat(cnt_pad), flat(fill)

    xs = _compact(seg, off, cnt_pad, fill, hn, drow, win=win, n_experts=E, cap=cap, tm=tm)
    ys = _experts(tile_e, tile_v, xs, wg, wu, wd, tm=tm, ff_chunk=wg.shape[-1])
    return _combine(seg, off, cnt_pad, x2, dcol, fw, ys, win=win, n_experts=E)


def kernel(x, norm_mix, w_in, conv_w, conv_b, conv_ng, conv_nb, sgu_ng, sgu_nb, sgu_w, sgu_b, w_out,
           norm_ffn, ffn_wg, ffn_wu, ffn_wd, moe_router, moe_wg, moe_wu, moe_wd, norm_final):
    B, S, D = x.shape
    depth = norm_mix.shape[0]
    assert depth == 2, "trunk is one dense layer followed by one MoE layer"
    mix = functools.partial(_mixer, ts=512, sub=256)
    mixer_params = (norm_mix, w_in, conv_w, conv_b, conv_ng, conv_nb, sgu_ng, sgu_nb, sgu_w, sgu_b, w_out)
    x, ffn_wg_b, ffn_wu_b, ffn_wd_b, moe_wg_b = mix(
        x, 0, *mixer_params, cast=((ffn_wg, 0), (ffn_wu, 0), (ffn_wd, 0), (moe_wg, 0)))
    FF = ffn_wg.shape[-1]
    x2, moe_wu_b = _ffn(x.reshape(B * S, D), 0, norm_ffn, ffn_wg_b, ffn_wu_b, ffn_wd_b,
                        tm=512, ff_chunk=FF, cast=((moe_wu, 0),))
    x, moe_wd_b = mix(x2.reshape(B, S, D), 1, *mixer_params, cast=((moe_wd, 0),))
    y2 = _moe(x.reshape(B * S, D), norm_ffn[1], moe_router[0], moe_wg_b.reshape(moe_wg[0].shape),
              moe_wu_b.reshape(moe_wu[0].shape), moe_wd_b.reshape(moe_wd[0].shape), norm_final)
    return y2.reshape(B, S, D)
```

```python
import functools

import jax
import jax.numpy as jnp
from jax import lax
from jax.experimental import pallas as pl
from jax.experimental.pallas import tpu as pltpu

F32 = jnp.float32
BF16 = jnp.bfloat16

EPS = 1e-6
CONV_WIDTH = 31
CONV_GROUPS = 8
SGU_HEADS = 4
CHUNK = 128
TOP_K = 2

SUBLANES = 8
LANES = 128
BF16_ROWS = 16
CONV_HALO = 32
VMEM_LIMIT_BYTES = 58 * 1024 * 1024
_MIX_STAGES = 3
_MIX_PIECES = SUBLANES

MOE_WINDOW = 512
MOE_TILE = 512


def _rms_norm(x, g):
    ms = jnp.mean(x * x, axis=-1, keepdims=True)
    return x * lax.rsqrt(ms + EPS) * g


def _gelu(x):
    return 0.5 * x * (1.0 + lax.erf(x * (2.0 ** -0.5)))


def _sigmoid(x):
    return 0.5 * jnp.tanh(0.5 * x) + 0.5


def _silu(x):
    h = 0.5 * x
    return h * jnp.tanh(h) + h


def _cast_plan(rows, nsteps):
    nblk = nsteps
    while rows % nblk or (rows // nblk) % BF16_ROWS:
        nblk //= 2
        assert nblk >= 1, rows
    return nblk, rows // nblk


def _cast_specs(items, nsteps, step_of):
    arrs, in_specs, out_specs, shapes = [], [], [], []
    for a, layer in items:
        a = a.reshape(a.shape[0], -1, a.shape[-1])
        nblk, blk = _cast_plan(a.shape[1], nsteps)
        blk_of = lambda *g, nblk=nblk: jnp.minimum(step_of(*g), nblk - 1)
        arrs.append(a)
        in_specs.append(pl.BlockSpec((None, blk, a.shape[2]),
                                     lambda *g, layer=layer, blk_of=blk_of: (layer, blk_of(*g), 0)))
        out_specs.append(pl.BlockSpec((blk, a.shape[2]), lambda *g, blk_of=blk_of: (blk_of(*g), 0)))
        shapes.append(jax.ShapeDtypeStruct(a.shape[1:], BF16))
    return arrs, in_specs, out_specs, shapes


def _cast_blocks(src_refs, dst_refs):
    for src, dst in zip(src_refs, dst_refs):
        dst[...] = src[...].astype(BF16)


_MIXER_INPUTS = 13
_FFN_WEIGHT_INPUTS = 4
_FFN_COLS = 256


def _mixer_kernel(*refs, layer, ts, sub, cc, sc, n_cast, ffn_layer):
    (x_ref, nw_all, win_in_ref, cw_ref, cb_all, cng_all, cnb_all, sng_all, snb_all,
     sw_ref, sb_ref, wout_in_ref, gavg_ref) = refs[:_MIXER_INPUTS]
    nw_ref, cb_ref, cng_ref, cnb_ref, sng_ref, snb_ref = (
        r.at[layer:layer + 1] for r in (nw_all, cb_all, cng_all, cnb_all, sng_all, snb_all))
    n_in = _MIXER_INPUTS
    if ffn_layer is not None:
        nwf_all, wg_ref, wu_ref, wd_ref = refs[n_in:n_in + _FFN_WEIGHT_INPUTS]
        nwf_ref = nwf_all.at[ffn_layer:ffn_layer + 1]
        n_in += _FFN_WEIGHT_INPUTS
    cast_in = refs[n_in:n_in + n_cast]
    o_ref = refs[n_in + n_cast]
    cast_out = refs[n_in + n_cast + 1:n_in + 2 * n_cast + 1]
    scratch = refs[n_in + 2 * n_cast + 1:]
    cbuf_ref = scratch[0]
    s = pl.program_id(1)
    _cast_blocks(cast_in, cast_out)

    if win_in_ref.dtype == BF16:
        win_ref, wout_ref = win_in_ref, wout_in_ref
    else:
        win_ref, wout_ref = scratch[1:]

        @pl.when(jnp.logical_and(pl.program_id(0) == 0, s == 0))
        def _():
            win_ref[...] = win_in_ref[...].astype(BF16)
            wout_ref[...] = wout_in_ref[...].astype(BF16)

    @pl.when(s == 0)
    def _():
        cbuf_ref[0:CONV_HALO, :] = jnp.zeros((CONV_HALO, cc), F32)

    row = lax.broadcasted_iota(jnp.int32, (CHUNK, CHUNK), 0)
    col = lax.broadcasted_iota(jnp.int32, (CHUNK, CHUNK), 1)
    tril = row >= col
    hd = sc // SGU_HEADS
    nq = -(-CONV_WIDTH // SUBLANES)

    nsub = ts // sub
    d_in = 2 * cc + 2 * sc
    cbw = d_in // _MIX_PIECES

    def sub_tile(j):
        r0 = j * sub
        x = x_ref[0, r0:r0 + sub, :]
        if ffn_layer is not None:
            hf = _rms_norm(x, nwf_ref[...]).astype(BF16)
            ff = wg_ref.shape[-1]
            fcols = [(c0, min(ff, c0 + _FFN_COLS)) for c0 in range(0, ff, _FFN_COLS)]
            per = -(-len(fcols) // 3)
            gu = []
            for w_ref in (wg_ref, wu_ref):
                blocks = []
                for i, (c0, c1) in enumerate(fcols):
                    blocks.append(jnp.dot(hf, w_ref[:, c0:c1], preferred_element_type=F32))
                    if i % per == per - 1 or i == len(fcols) - 1:
                        yield
                gu.append(jnp.concatenate(blocks, axis=1))
            act = (_silu(gu[0]) * gu[1]).astype(BF16)
            dm = wd_ref.shape[-1]
            yblk = []
            for i in range(4):
                c0, c1 = i * dm // 4, (i + 1) * dm // 4
                yblk.append(jnp.dot(act, wd_ref[:, c0:c1], preferred_element_type=F32))
                if i % 2 == 1:
                    yield
            x = x + jnp.concatenate(yblk, axis=1)
        h = _rms_norm(x, nw_ref[...]).astype(BF16)
        zblk = []
        for k in range(_MIX_PIECES):
            zblk.append(jnp.dot(h, win_ref[:, k * cbw:(k + 1) * cbw], preferred_element_type=F32))
            yield
        z = jnp.concatenate(zblk, axis=1)
        a = z[:, :cc]
        gate = z[:, cc:2 * cc]
        u = z[:, 2 * cc:2 * cc + sc]
        v = z[:, 2 * cc + sc:]

        base = CONV_HALO + r0
        cbuf_ref[base:base + sub, :] = a * _sigmoid(gate)
        acc = jnp.broadcast_to(cb_ref[...], (sub, cc))
        for b in range(SUBLANES):
            zb = None
            for q in range(nq):
                d = SUBLANES * q + b
                if d >= CONV_WIDTH:
                    continue
                start = base - SUBLANES * (q + 1)
                term = (cw_ref[CONV_WIDTH - 1 - d]
                        * cbuf_ref[start:start + sub + SUBLANES, :])
                zb = term if zb is None else zb + term
            acc = acc + zb[SUBLANES - b:SUBLANES - b + sub, :]
            yield

        gavg = gavg_ref[...]
        acc_hi = acc.astype(BF16)
        acc_lo = (acc - acc_hi.astype(F32)).astype(BF16)
        mu = (jnp.dot(acc_hi, gavg, preferred_element_type=F32)
              + jnp.dot(acc_lo, gavg, preferred_element_type=F32))
        yield
        dev = acc - mu
        var = jnp.dot((dev * dev).astype(BF16), gavg, preferred_element_type=F32)
        yield
        cn = dev * lax.rsqrt(var + EPS) * cng_ref[...] + cnb_ref[...]
        c_out = _silu(cn).astype(BF16)
        yield

        u = _gelu(u)
        v = _gelu(v)
        yield
        g_cols = []
        for hh in range(SGU_HEADS):
            vh = v[:, hh * hd:(hh + 1) * hd]
            mu_h = jnp.mean(vh, axis=-1, keepdims=True)
            dh = vh - mu_h
            var_h = jnp.mean(dh * dh, axis=-1, keepdims=True)
            vn = (dh * lax.rsqrt(var_h + EPS) * sng_ref[:, hh * hd:(hh + 1) * hd]
                  + snb_ref[:, hh * hd:(hh + 1) * hd]).astype(BF16)
            ws = jnp.where(tril, sw_ref[hh], 0.0).astype(BF16)
            rows = []
            for ci in range(sub // CHUNK):
                sp = jnp.dot(ws, vn[ci * CHUNK:(ci + 1) * CHUNK, :], preferred_element_type=F32)
                rows.append(sp + sb_ref[hh])
            sp_h = rows[0] if len(rows) == 1 else jnp.concatenate(rows, axis=0)
            g_cols.append((u[:, hh * hd:(hh + 1) * hd] * sp_h).astype(BF16))
            if hh % 2 == 1:
                yield
        g_out = jnp.concatenate(g_cols, axis=1)

        yc = jnp.dot(c_out, wout_ref[0:cc, :], preferred_element_type=F32)
        yield
        yg = jnp.dot(g_out, wout_ref[cc:cc + sc, :], preferred_element_type=F32)
        o_ref[0, r0:r0 + sub, :] = x + (yc + yg)
        yield

    stages = _MIX_STAGES + (ffn_layer is not None)
    skew = _MIX_PIECES * (stages - 2)
    gens = [sub_tile(j) for j in range(nsub)]
    for tick in range(skew * (nsub - 1) + _MIX_PIECES * stages):
        for j in reversed(range(nsub)):
            if 0 <= tick - skew * j < _MIX_PIECES * stages:
                next(gens[j])

    cbuf_ref[0:CONV_HALO, :] = cbuf_ref[ts:ts + CONV_HALO, :]


def _mixer(x, layer, norm_mix, w_in, conv_w, conv_b, conv_ng, conv_nb, sgu_ng, sgu_nb, sgu_w, sgu_b,
           w_out, *, ts, sub, cast=(), proj=None, ffn=None):
    B, S, D = x.shape
    L = conv_w.shape[0]
    cc = conv_w.shape[-1]
    sc = sgu_ng.shape[-1]
    d_in = w_in.shape[-1]
    hd = sc // SGU_HEADS
    assert S % ts == 0 and ts % sub == 0 and sub % CHUNK == 0 and d_in == 2 * cc + 2 * sc
    ns = S // ts
    gs = cc // CONV_GROUPS
    gid = jnp.arange(cc) // gs
    gavg = jnp.where(gid[:, None] == gid[None, :], 1.0 / gs, 0.0).astype(BF16)
    sb_full = jnp.broadcast_to(sgu_b[:, :, :, None], (L, SGU_HEADS, CHUNK, hd))
    whole = lambda p: pl.BlockSpec(p.shape, lambda b, s: (0,) * p.ndim)
    layer_block = lambda *shape: pl.BlockSpec((None,) + shape, lambda b, s: (layer,) + (0,) * len(shape))
    cast_arrs, cast_in_specs, cast_out_specs, cast_shapes = _cast_specs(
        cast, B * ns, lambda b, s: b * ns + s)
    kern = functools.partial(_mixer_kernel, layer=layer, ts=ts, sub=sub, cc=cc, sc=sc, n_cast=len(cast),
                             ffn_layer=None if ffn is None else ffn[0])
    scratch = [pltpu.VMEM((CONV_HALO + ts, cc), F32)]
    if proj is None:
        w_in_spec, w_out_spec = layer_block(D, d_in), layer_block(cc + sc, D)
        scratch += [pltpu.VMEM((D, d_in), BF16), pltpu.VMEM((cc + sc, D), BF16)]
    else:
        w_in, w_out = proj
        assert w_in.dtype == BF16 and w_out.dtype == BF16
        w_in_spec, w_out_spec = whole(w_in), whole(w_out)
    ffn_args = () if ffn is None else tuple(ffn[1:])
    in_specs = [
        pl.BlockSpec((1, ts, D), lambda b, s: (b, s, 0)),
        whole(norm_mix),
        w_in_spec,
        layer_block(CONV_WIDTH, 1, cc),
        whole(conv_b), whole(conv_ng), whole(conv_nb),
        whole(sgu_ng), whole(sgu_nb),
        layer_block(SGU_HEADS, CHUNK, CHUNK),
        layer_block(SGU_HEADS, CHUNK, hd),
        w_out_spec,
        whole(gavg),
    ]
    assert len(in_specs) == _MIXER_INPUTS and len(ffn_args) in (0, _FFN_WEIGHT_INPUTS)
    in_specs += [whole(a) for a in ffn_args]
    return pl.pallas_call(
        kern,
        grid=(B, ns),
        in_specs=in_specs + cast_in_specs,
        out_specs=[pl.BlockSpec((1, ts, D), lambda b, s: (b, s, 0))] + cast_out_specs,
        out_shape=[jax.ShapeDtypeStruct((B, S, D), F32)] + cast_shapes,
        scratch_shapes=scratch,
        compiler_params=pltpu.CompilerParams(
            dimension_semantics=("arbitrary", "arbitrary"),
            vmem_limit_bytes=VMEM_LIMIT_BYTES),
        name="mixer",
    )(x, norm_mix, w_in, conv_w, conv_b, conv_ng, conv_nb, sgu_ng, sgu_nb, sgu_w, sb_full, w_out,
      gavg, *ffn_args, *cast_arrs)


def _swiglu_tile(h, wg_ref, wu_ref, wd_ref, ff_chunk):
    ff = wg_ref.shape[-1]
    out = None
    for c0 in range(0, ff, ff_chunk):
        c1 = min(ff, c0 + ff_chunk)
        g = jnp.dot(h, wg_ref[:, c0:c1], preferred_element_type=F32)
        u = jnp.dot(h, wu_ref[:, c0:c1], preferred_element_type=F32)
        a = (_silu(g) * u).astype(BF16)
        y = jnp.dot(a, wd_ref[c0:c1, :], preferred_element_type=F32)
        out = y if out is None else out + y
    return out


_FFN_INPUTS = 5


def _ffn_kernel(*refs, layer, ff_chunk, n_cast):
    x_ref, nw_all, wg_ref, wu_ref, wd_ref = refs[:_FFN_INPUTS]
    nw_ref = nw_all.at[layer:layer + 1]
    cast_in = refs[_FFN_INPUTS:_FFN_INPUTS + n_cast]
    o_ref = refs[_FFN_INPUTS + n_cast]
    cast_out = refs[_FFN_INPUTS + n_cast + 1:]
    _cast_blocks(cast_in, cast_out)
    x = x_ref[...]
    h = _rms_norm(x, nw_ref[...]).astype(BF16)
    o_ref[...] = x + _swiglu_tile(h, wg_ref, wu_ref, wd_ref, ff_chunk)


def _ffn(x2, layer, norm_ffn, wg, wu, wd, *, tm, ff_chunk, cast=()):
    T, D = x2.shape
    FF = wg.shape[-1]
    assert T % tm == 0 and wg.dtype == BF16
    cast_arrs, cast_in_specs, cast_out_specs, cast_shapes = _cast_specs(cast, T // tm, lambda i: i)
    kern = functools.partial(_ffn_kernel, layer=layer, ff_chunk=ff_chunk, n_cast=len(cast))
    in_specs = [
        pl.BlockSpec((tm, D), lambda i: (i, 0)),
        pl.BlockSpec(norm_ffn.shape, lambda i: (0, 0)),
        pl.BlockSpec((D, FF), lambda i: (0, 0)),
        pl.BlockSpec((D, FF), lambda i: (0, 0)),
        pl.BlockSpec((FF, D), lambda i: (0, 0)),
    ]
    assert len(in_specs) == _FFN_INPUTS
    return pl.pallas_call(
        kern,
        grid=(T // tm,),
        in_specs=in_specs + cast_in_specs,
        out_specs=[pl.BlockSpec((tm, D), lambda i: (i, 0))] + cast_out_specs,
        out_shape=[jax.ShapeDtypeStruct((T, D), F32)] + cast_shapes,
        compiler_params=pltpu.CompilerParams(
            dimension_semantics=("arbitrary",), vmem_limit_bytes=VMEM_LIMIT_BYTES),
        name="ffn",
    )(x2, norm_ffn, wg, wu, wd, *cast_arrs)


def _round_up(n, m):
    return (n + m - 1) // m * m


def _piece_sizes(largest):
    sizes = []
    s = largest
    while s >= BF16_ROWS:
        sizes.append(s)
        s //= 2
    assert sizes and sizes[-1] == BF16_ROWS
    return sizes


def _for_each_piece(count, sizes, fn):
    for s in sizes:
        offset = jnp.bitwise_and(count, -2 * s)

        @pl.when(jnp.bitwise_and(count, s) != 0)
        def _():
            fn(offset, s)


def _route_kernel(x_ref, nw_ref, rt_ref, hn_ref, dcol_ref, drow_ref, cnt_ref, *, win, n_experts):
    for k in range(x_ref.shape[0] // win):
        rows = slice(k * win, (k + 1) * win)
        _route_window(x_ref.at[rows], nw_ref, rt_ref, hn_ref.at[rows],
                      dcol_ref.at[rows], drow_ref.at[k], cnt_ref.at[k], n_experts)


def _route_window(x_ref, nw_ref, rt_ref, hn_ref, dcol_ref, drow_ref, cnt_ref, n_experts):
    x = x_ref[...]
    h = _rms_norm(x, nw_ref[...]).astype(BF16)
    hn_ref[...] = h
    logits = jnp.dot(h, rt_ref[...], preferred_element_type=F32)
    lane = lax.broadcasted_iota(jnp.int32, logits.shape, 1)
    lt = jnp.transpose(jnp.where(lane < n_experts, logits, -jnp.inf))[0:SUBLANES, :]
    win = lt.shape[1]
    sub = lax.broadcasted_iota(jnp.int32, lt.shape, 0)
    m1 = jnp.max(lt, axis=0, keepdims=True)
    i1 = jnp.min(jnp.where(lt == m1, sub, SUBLANES), axis=0, keepdims=True)
    rest = jnp.where(sub == i1, -jnp.inf, lt)
    m2 = jnp.max(rest, axis=0, keepdims=True)
    i2 = jnp.min(jnp.where(rest == m2, sub, SUBLANES), axis=0, keepdims=True)
    e2 = jnp.exp(m2 - m1)
    den = 1.0 + e2
    g1 = 1.0 / den
    g2 = e2 / den
    sel1 = sub == i1
    sel2 = sub == i2
    self = jnp.logical_or(sel1, sel2).astype(F32)
    tok = lax.broadcasted_iota(jnp.int32, lt.shape, 1)
    inc = self
    step = 1
    while step < win:
        inc = inc + jnp.where(tok >= step, pltpu.roll(inc, step, axis=1), 0.0)
        step *= 2
    rank = inc - self
    cnt = jnp.sum(self, axis=1, keepdims=True)
    cnt_pad = jnp.floor((cnt + (BF16_ROWS - 1.0)) * (1.0 / BF16_ROWS)) * BF16_ROWS
    seg = jnp.zeros_like(cnt_pad)
    for e in range(n_experts - 1):
        seg = seg + jnp.where(sub[:, 0:1] > e, cnt_pad[e:e + 1, :], 0.0)
    pos = seg + rank
    dest1 = jnp.sum(jnp.where(sel1, pos, 0.0), axis=0, keepdims=True)
    dest2 = jnp.sum(jnp.where(sel2, pos, 0.0), axis=0, keepdims=True)
    info = jnp.where(sub == 0, dest1, jnp.where(sub == 1, dest2,
                     jnp.where(sub == 2, g1, jnp.where(sub == 3, g2, -1.0))))
    drow_ref[...] = info
    cols = jnp.concatenate([info, jnp.broadcast_to(cnt, (SUBLANES, win)),
                            jnp.full((LANES - 2 * SUBLANES, win), -1.0, F32)], axis=0)
    cols = jnp.transpose(cols)
    dcol_ref[...] = cols
    cnt_ref[...] = cols[0:1, :].astype(jnp.int32)


def _route(x2, nw, router, *, win, per_step=4):
    T, D = x2.shape
    E = router.shape[-1]
    assert T % (win * per_step) == 0 and E <= SUBLANES
    nwin = T // win
    blk = win * per_step
    rt = jnp.zeros((D, LANES), F32).at[:, :E].set(router).astype(BF16)
    kern = functools.partial(_route_kernel, win=win, n_experts=E)
    return pl.pallas_call(
        kern,
        grid=(nwin // per_step,),
        in_specs=[
            pl.BlockSpec((blk, D), lambda w: (w, 0)),
            pl.BlockSpec((1, D), lambda w: (0, 0)),
            pl.BlockSpec((D, LANES), lambda w: (0, 0)),
        ],
        out_specs=[
            pl.BlockSpec((blk, D), lambda w: (w, 0)),
            pl.BlockSpec((blk, LANES), lambda w: (w, 0)),
            pl.BlockSpec((per_step, SUBLANES, win), lambda w: (w, 0, 0)),
            pl.BlockSpec((per_step, 1, LANES), lambda w: (w, 0, 0)),
        ],
        out_shape=[
            jax.ShapeDtypeStruct((T, D), BF16),
            jax.ShapeDtypeStruct((T, LANES), F32),
            jax.ShapeDtypeStruct((nwin, SUBLANES, win), F32),
            jax.ShapeDtypeStruct((nwin, 1, LANES), jnp.int32),
        ],
        compiler_params=pltpu.CompilerParams(
            dimension_semantics=("arbitrary",), vmem_limit_bytes=VMEM_LIMIT_BYTES),
        name="moe_route",
    )(x2, nw.reshape(1, D).astype(F32), rt)


def _compact_kernel(seg_ref, off_ref, cnt_ref, fill_ref, hn_ref, drow_ref, xs_ref,
                    stage_ref, zero_ref, sem, *, win, n_experts, rows, tm):
    w = pl.program_id(0)
    nwin = pl.num_programs(0)
    slot = w % 2
    d = hn_ref.shape[-1]
    sizes = _piece_sizes(win)

    def segments(ww, sl, start):
        for e in range(n_experts):
            src0 = seg_ref[ww * n_experts + e]
            dst0 = off_ref[ww * n_experts + e]

            def piece(offset, size):
                cp = pltpu.make_async_copy(
                    stage_ref.at[sl, pl.ds(pl.multiple_of(src0 + offset, BF16_ROWS), size)],
                    xs_ref.at[pl.ds(pl.multiple_of(dst0 + offset, BF16_ROWS), size)],
                    sem.at[sl, e])
                cp.start() if start else cp.wait()

            _for_each_piece(cnt_ref[ww * n_experts + e], sizes, piece)

    def zero_rows(dst, size, e, start):
        cp = pltpu.make_async_copy(zero_ref.at[pl.ds(0, size)],
                                   xs_ref.at[pl.ds(pl.multiple_of(dst, BF16_ROWS), size)],
                                   sem.at[2, e])
        cp.start() if start else cp.wait()

    @pl.when(w == 0)
    def _():
        zero_ref[...] = jnp.zeros(zero_ref.shape, BF16)

    d1 = drow_ref[0, 0:1, :]
    d2 = drow_ref[0, 1:2, :]
    g1 = drow_ref[0, 2:3, :]
    g2 = drow_ref[0, 3:4, :]
    r = lax.broadcasted_iota(jnp.int32, (rows, win), 0).astype(F32)
    m1 = r == d1
    m2 = r == d2
    onehot = jnp.logical_or(m1, m2).astype(BF16)
    xg = jnp.dot(onehot, hn_ref[...], preferred_element_type=F32)
    stage_ref[slot, :, 0:d] = xg.astype(BF16)
    gate = jnp.sum(jnp.where(m1, g1, 0.0) + jnp.where(m2, g2, 0.0), axis=-1, keepdims=True)
    hi = gate.astype(BF16).astype(F32)
    mid = (gate - hi).astype(BF16).astype(F32)
    lo = gate - hi - mid
    lane = lax.broadcasted_iota(jnp.int32, (rows, LANES), 1)
    terms = jnp.where(lane == 0, hi, jnp.where(lane == 1, mid, jnp.where(lane == 2, lo, 0.0)))
    stage_ref[slot, :, d:d + LANES] = terms.astype(BF16)

    segments(w, slot, True)

    @pl.when(w > 0)
    def _():
        segments(w - 1, 1 - slot, False)

    @pl.when(w == nwin - 1)
    def _():
        segments(w, slot, False)
        gap_sizes = _piece_sizes(tm // 2)
        for start in (True, False):
            for e in range(n_experts):
                _for_each_piece(fill_ref[n_experts + e], gap_sizes,
                                lambda offset, size, e=e: zero_rows(fill_ref[e] + offset, size, e, start))
        tail0 = fill_ref[2 * n_experts]
        ntail = fill_ref[2 * n_experts + 1]
        tail_rows = zero_ref.shape[0]

        def tail_start(j, c):
            zero_rows(tail0 + j * tail_rows, tail_rows, 0, True)
            return c

        def tail_wait(j, c):
            zero_rows(tail0 + j * tail_rows, tail_rows, 0, False)
            return c

        lax.fori_loop(0, ntail, tail_start, 0)
        lax.fori_loop(0, ntail, tail_wait, 0)


def _compact(seg, off, cnt_pad, fill, hn, drow, *, win, n_experts, cap, tm):
    T, D = hn.shape
    nwin = T // win
    rows = _round_up(TOP_K * win + n_experts * (BF16_ROWS - 1), BF16_ROWS)
    kern = functools.partial(_compact_kernel, win=win, n_experts=n_experts, rows=rows, tm=tm)
    return pl.pallas_call(
        kern,
        grid_spec=pltpu.PrefetchScalarGridSpec(
            num_scalar_prefetch=4,
            grid=(nwin,),
            in_specs=[
                pl.BlockSpec((win, D), lambda w, *_: (w, 0)),
                pl.BlockSpec((1, SUBLANES, win), lambda w, *_: (w, 0, 0)),
            ],
            out_specs=pl.BlockSpec(memory_space=pl.ANY),
            scratch_shapes=[pltpu.VMEM((2, rows, D + LANES), BF16),
                            pltpu.VMEM((tm // 2, D + LANES), BF16),
                            pltpu.SemaphoreType.DMA((3, n_experts))],
        ),
        out_shape=jax.ShapeDtypeStruct((cap, D + LANES), BF16),
        compiler_params=pltpu.CompilerParams(
            dimension_semantics=("arbitrary",), vmem_limit_bytes=VMEM_LIMIT_BYTES),
        name="moe_compact",
    )(seg, off, cnt_pad, fill, hn, drow)


def _expert_kernel(te_ref, tv_ref, xs_ref, wg_ref, wu_ref, wd_ref, ys_ref, *, ff_chunk):
    i = pl.program_id(0)
    d = ys_ref.shape[-1]

    @pl.when(tv_ref[i] > 0)
    def _():
        y = _swiglu_tile(xs_ref[:, 0:d], wg_ref.at[0], wu_ref.at[0], wd_ref.at[0], ff_chunk)
        gate = jnp.sum(xs_ref[:, d:d + LANES].astype(F32), axis=-1, keepdims=True)
        ys_ref[...] = (gate * y).astype(BF16)

    @pl.when(tv_ref[i] == 0)
    def _():
        ys_ref[...] = jnp.zeros(ys_ref.shape, BF16)


def _experts(tile_e, tile_v, xs, wg, wu, wd, *, tm, ff_chunk):
    cap = xs.shape[0]
    _, D, FF = wg.shape
    assert cap == tile_e.shape[0] * tm and xs.shape[1] == D + LANES
    kern = functools.partial(_expert_kernel, ff_chunk=ff_chunk)
    return pl.pallas_call(
        kern,
        grid_spec=pltpu.PrefetchScalarGridSpec(
            num_scalar_prefetch=2,
            grid=(cap // tm,),
            in_specs=[
                pl.BlockSpec((tm, D + LANES), lambda i, te, tv: (i, 0)),
                pl.BlockSpec((1, D, FF), lambda i, te, tv: (te[i], 0, 0)),
                pl.BlockSpec((1, D, FF), lambda i, te, tv: (te[i], 0, 0)),
                pl.BlockSpec((1, FF, D), lambda i, te, tv: (te[i], 0, 0)),
            ],
            out_specs=pl.BlockSpec((tm, D), lambda i, te, tv: (i, 0)),
        ),
        out_shape=jax.ShapeDtypeStruct((cap, D), BF16),
        compiler_params=pltpu.CompilerParams(
            dimension_semantics=("arbitrary",), vmem_limit_bytes=VMEM_LIMIT_BYTES),
        name="moe_experts",
    )(tile_e, tile_v, xs, wg.astype(BF16), wu.astype(BF16), wd.astype(BF16))


def _combine_kernel(seg_ref, off_ref, cnt_ref, x_ref, dcol_ref, fw_ref, ys_ref, o_ref,
                    stage_ref, sem, *, win, n_experts):
    w = pl.program_id(0)
    nwin = pl.num_programs(0)
    slot = w % 2
    rows = stage_ref.shape[1]
    sizes = _piece_sizes(win)

    def segments(ww, sl, start):
        for e in range(n_experts):
            src0 = off_ref[ww * n_experts + e]
            dst0 = seg_ref[ww * n_experts + e]

            def piece(offset, size):
                cp = pltpu.make_async_copy(
                    ys_ref.at[pl.ds(pl.multiple_of(src0 + offset, BF16_ROWS), size)],
                    stage_ref.at[sl, pl.ds(pl.multiple_of(dst0 + offset, BF16_ROWS), size)],
                    sem.at[sl, e])
                cp.start() if start else cp.wait()

            _for_each_piece(cnt_ref[ww * n_experts + e], sizes, piece)

    @pl.when(w == 0)
    def _():
        stage_ref[...] = jnp.zeros(stage_ref.shape, BF16)
        segments(0, 0, True)

    @pl.when(w + 1 < nwin)
    def _():
        segments(w + 1, 1 - slot, True)

    segments(w, slot, False)

    d1 = dcol_ref[:, 0:1]
    d2 = dcol_ref[:, 1:2]
    r = lax.broadcasted_iota(jnp.int32, (win, rows), 1).astype(F32)
    onehot = jnp.logical_or(r == d1, r == d2).astype(BF16)
    y = jnp.dot(onehot, stage_ref[slot], preferred_element_type=F32)
    o_ref[...] = _rms_norm(x_ref[...] + y, fw_ref[...])


def _combine(seg, off, cnt_pad, x2, dcol, fw, ys, *, win, n_experts):
    T, D = x2.shape
    rows = _round_up(TOP_K * win + n_experts * (BF16_ROWS - 1), BF16_ROWS)
    kern = functools.partial(_combine_kernel, win=win, n_experts=n_experts)
    return pl.pallas_call(
        kern,
        grid_spec=pltpu.PrefetchScalarGridSpec(
            num_scalar_prefetch=3,
            grid=(T // win,),
            in_specs=[
                pl.BlockSpec((win, D), lambda w, *_: (w, 0)),
                pl.BlockSpec((win, LANES), lambda w, *_: (w, 0)),
                pl.BlockSpec((1, D), lambda w, *_: (0, 0)),
                pl.BlockSpec(memory_space=pl.ANY),
            ],
            out_specs=pl.BlockSpec((win, D), lambda w, *_: (w, 0)),
            scratch_shapes=[pltpu.VMEM((2, rows, D), BF16),
                            pltpu.SemaphoreType.DMA((2, n_experts))],
        ),
        out_shape=jax.ShapeDtypeStruct((T, D), F32),
        compiler_params=pltpu.CompilerParams(
            dimension_semantics=("arbitrary",), vmem_limit_bytes=VMEM_LIMIT_BYTES),
        name="moe_combine",
    )(seg, off, cnt_pad, x2, dcol, fw.reshape(1, D).astype(F32), ys)


def _moe(x2, nw, router, wg, wu, wd, fw, *, win=MOE_WINDOW, tm=MOE_TILE):
    T, D = x2.shape
    E = router.shape[-1]
    nwin = T // win
    hn, dcol, drow, cnt = _route(x2, nw, router, win=win)

    cnt = cnt.reshape(nwin, LANES)[:, SUBLANES:SUBLANES + E]
    cnt_pad = _round_up(cnt, BF16_ROWS)
    seg = jnp.cumsum(cnt_pad, axis=1) - cnt_pad
    total = jnp.sum(cnt_pad, axis=0)
    region = _round_up(total, tm)
    r_end = jnp.cumsum(region)
    r_start = r_end - region
    off = r_start[None, :] + jnp.cumsum(cnt_pad, axis=0) - cnt_pad
    max_rows = TOP_K * T + nwin * E * (BF16_ROWS - 1)
    ntiles_max = max_rows // tm + E
    cap = ntiles_max * tm
    ti = jnp.arange(ntiles_max)
    t_end = r_end // tm
    tile_e = jnp.minimum(jnp.sum(ti[:, None] >= t_end[None, :], axis=1), E - 1).astype(jnp.int32)
    tile_v = (ti < t_end[-1]).astype(jnp.int32)
    tail = jnp.stack([r_end[-1], (cap - r_end[-1]) // (tm // 2)])
    fill = jnp.concatenate([r_start + total, region - total, tail])
    flat = lambda a: a.reshape(-1).astype(jnp.int32)
    seg, off, cnt_pad, fill = flat(seg), flat(off), flat(cnt_pad), flat(fill)

    xs = _compact(seg, off, cnt_pad, fill, hn, drow, win=win, n_experts=E, cap=cap, tm=tm)
    ys = _experts(tile_e, tile_v, xs, wg, wu, wd, tm=tm, ff_chunk=wg.shape[-1])
    return _combine(seg, off, cnt_pad, x2, dcol, fw, ys, win=win, n_experts=E)


def kernel(x, norm_mix, w_in, conv_w, conv_b, conv_ng, conv_nb, sgu_ng, sgu_nb, sgu_w, sgu_b, w_out,
           norm_ffn, ffn_wg, ffn_wu, ffn_wd, moe_router, moe_wg, moe_wu, moe_wd, norm_final):
    B, S, D = x.shape
    depth = norm_mix.shape[0]
    assert depth == 2, "trunk is one dense layer followed by one MoE layer"
    mix = functools.partial(_mixer, ts=512, sub=256)
    mixer_params = (norm_mix, w_in, conv_w, conv_b, conv_ng, conv_nb, sgu_ng, sgu_nb, sgu_w, sgu_b, w_out)
    x, ffn_wg_b, ffn_wu_b, ffn_wd_b, w_in1_b, w_out1_b, moe_wg_b = mix(
        x, 0, *mixer_params,
        cast=((ffn_wg, 0), (ffn_wu, 0), (ffn_wd, 0), (w_in, 1), (w_out, 1), (moe_wg, 0)))
    x, moe_wu_b, moe_wd_b = mix(x, 1, *mixer_params, proj=(w_in1_b, w_out1_b),
                                ffn=(0, norm_ffn, ffn_wg_b, ffn_wu_b, ffn_wd_b),
                                cast=((moe_wu, 0), (moe_wd, 0)))
    y2 = _moe(x.reshape(B * S, D), norm_ffn[1], moe_router[0], moe_wg_b.reshape(moe_wg[0].shape),
              moe_wu_b.reshape(moe_wu[0].shape), moe_wd_b.reshape(moe_wd[0].shape), norm_final)
    return y2.reshape(B, S, D)
```

```python
import functools

import jax
import jax.numpy as jnp
from jax import lax
from jax.experimental import pallas as pl
from jax.experimental.pallas import tpu as pltpu

F32 = jnp.float32
BF16 = jnp.bfloat16

EPS = 1e-6
CONV_WIDTH = 31
CONV_GROUPS = 8
SGU_HEADS = 4
CHUNK = 128
TOP_K = 2

SUBLANES = 8
LANES = 128
BF16_ROWS = 16
CONV_HALO = 32
VMEM_LIMIT_BYTES = 58 * 1024 * 1024
_MIX_STAGES = 3
_MIX_PIECES = SUBLANES

MOE_WINDOW = 512
MOE_TILE = 512


def _rms_norm(x, g):
    ms = jnp.mean(x * x, axis=-1, keepdims=True)
    return x * lax.rsqrt(ms + EPS) * g


def _gelu(x):
    return 0.5 * x * (1.0 + lax.erf(x * (2.0 ** -0.5)))


def _sigmoid(x):
    return 0.5 * jnp.tanh(0.5 * x) + 0.5


def _silu(x):
    h = 0.5 * x
    return h * jnp.tanh(h) + h


def _cast_plan(rows, nsteps):
    nblk = nsteps
    while rows % nblk or (rows // nblk) % BF16_ROWS:
        nblk //= 2
        assert nblk >= 1, rows
    return nblk, rows // nblk


def _cast_specs(items, nsteps, step_of):
    arrs, in_specs, out_specs, shapes = [], [], [], []
    for a, layer in items:
        a = a.reshape(a.shape[0], -1, a.shape[-1])
        nblk, blk = _cast_plan(a.shape[1], nsteps)
        blk_of = lambda *g, nblk=nblk: jnp.minimum(step_of(*g), nblk - 1)
        arrs.append(a)
        in_specs.append(pl.BlockSpec((None, blk, a.shape[2]),
                                     lambda *g, layer=layer, blk_of=blk_of: (layer, blk_of(*g), 0)))
        out_specs.append(pl.BlockSpec((blk, a.shape[2]), lambda *g, blk_of=blk_of: (blk_of(*g), 0)))
        shapes.append(jax.ShapeDtypeStruct(a.shape[1:], BF16))
    return arrs, in_specs, out_specs, shapes


def _cast_blocks(src_refs, dst_refs):
    for src, dst in zip(src_refs, dst_refs):
        dst[...] = src[...].astype(BF16)


_MIXER_INPUTS = 13
_FFN_INPUTS = 5
_FFN_COLS = 256


def _mixer_kernel(*refs, layer, ts, sub, cc, sc, n_cast, ffn_layer):
    (x_ref, nw_all, win_in_ref, cw_ref, cb_all, cng_all, cnb_all, sng_all, snb_all,
     sw_ref, sb_ref, wout_in_ref, gavg_ref) = refs[:_MIXER_INPUTS]
    nw_ref, cb_ref, cng_ref, cnb_ref, sng_ref, snb_ref = (
        r.at[layer:layer + 1] for r in (nw_all, cb_all, cng_all, cnb_all, sng_all, snb_all))
    n_in = _MIXER_INPUTS
    if ffn_layer is not None:
        xn_ref, nwf_all, wg_ref, wu_ref, wd_ref = refs[n_in:n_in + _FFN_INPUTS]
        nwf_ref = nwf_all.at[ffn_layer:ffn_layer + 1]
        n_in += _FFN_INPUTS
    cast_in = refs[n_in:n_in + n_cast]
    o_ref = refs[n_in + n_cast]
    cast_out = refs[n_in + n_cast + 1:n_in + 2 * n_cast + 1]
    scratch = refs[n_in + 2 * n_cast + 1:]
    cbuf_ref = scratch[0]
    s = pl.program_id(1)
    _cast_blocks(cast_in, cast_out)

    if win_in_ref.dtype == BF16:
        win_ref, wout_ref = win_in_ref, wout_in_ref
    else:
        win_ref, wout_ref = scratch[1:3]

        @pl.when(jnp.logical_and(pl.program_id(0) == 0, s == 0))
        def _():
            win_ref[...] = win_in_ref[...].astype(BF16)
            wout_ref[...] = wout_in_ref[...].astype(BF16)

    @pl.when(s == 0)
    def _():
        cbuf_ref[0:CONV_HALO, :] = jnp.zeros((CONV_HALO, cc), F32)

    row = lax.broadcasted_iota(jnp.int32, (CHUNK, CHUNK), 0)
    col = lax.broadcasted_iota(jnp.int32, (CHUNK, CHUNK), 1)
    tril = row >= col
    hd = sc // SGU_HEADS
    nq = -(-CONV_WIDTH // SUBLANES)

    nsub = ts // sub
    d_in = 2 * cc + 2 * sc
    cbw = d_in // _MIX_PIECES

    def ffn_stage(x):
        hf = _rms_norm(x, nwf_ref[...]).astype(BF16)
        ff = wg_ref.shape[-1]
        fcols = [(c0, min(ff, c0 + _FFN_COLS)) for c0 in range(0, ff, _FFN_COLS)]
        per = -(-len(fcols) // 3)
        gu = []
        for w_ref in (wg_ref, wu_ref):
            blocks = []
            for i, (c0, c1) in enumerate(fcols):
                blocks.append(jnp.dot(hf, w_ref[:, c0:c1], preferred_element_type=F32))
                if i % per == per - 1 or i == len(fcols) - 1:
                    yield
            gu.append(jnp.concatenate(blocks, axis=1))
        act = (_silu(gu[0]) * gu[1]).astype(BF16)
        dm = wd_ref.shape[-1]
        yblk = []
        for i in range(4):
            c0, c1 = i * dm // 4, (i + 1) * dm // 4
            yblk.append(jnp.dot(act, wd_ref[:, c0:c1], preferred_element_type=F32))
            if i % 2 == 1 and i < 3:
                yield
        return x + jnp.concatenate(yblk, axis=1)

    if ffn_layer is not None:
        xcarry_ref = scratch[-1]

        @pl.when(jnp.logical_and(pl.program_id(0) == 0, s == 0))
        def _():
            stage = ffn_stage(x_ref[0, 0:sub, :])
            try:
                while True:
                    next(stage)
            except StopIteration as done:
                xcarry_ref[...] = done.value

    def next_first_sub_tile():
        xcarry_ref[...] = yield from ffn_stage(xn_ref[0])
        yield

    def sub_tile(j):
        r0 = j * sub
        if ffn_layer is None:
            x = x_ref[0, r0:r0 + sub, :]
        elif j == 0:
            x = xcarry_ref[...]
        else:
            x = yield from ffn_stage(x_ref[0, r0:r0 + sub, :])
            yield
        h = _rms_norm(x, nw_ref[...]).astype(BF16)
        zblk = []
        for k in range(_MIX_PIECES):
            zblk.append(jnp.dot(h, win_ref[:, k * cbw:(k + 1) * cbw], preferred_element_type=F32))
            yield
        z = jnp.concatenate(zblk, axis=1)
        a = z[:, :cc]
        gate = z[:, cc:2 * cc]
        u = z[:, 2 * cc:2 * cc + sc]
        v = z[:, 2 * cc + sc:]

        base = CONV_HALO + r0
        cbuf_ref[base:base + sub, :] = a * _sigmoid(gate)
        acc = jnp.broadcast_to(cb_ref[...], (sub, cc))
        for b in range(SUBLANES):
            zb = None
            for q in range(nq):
                d = SUBLANES * q + b
                if d >= CONV_WIDTH:
                    continue
                start = base - SUBLANES * (q + 1)
                term = (cw_ref[CONV_WIDTH - 1 - d]
                        * cbuf_ref[start:start + sub + SUBLANES, :])
                zb = term if zb is None else zb + term
            acc = acc + zb[SUBLANES - b:SUBLANES - b + sub, :]
            yield

        gavg = gavg_ref[...]
        acc_hi = acc.astype(BF16)
        acc_lo = (acc - acc_hi.astype(F32)).astype(BF16)
        mu = (jnp.dot(acc_hi, gavg, preferred_element_type=F32)
              + jnp.dot(acc_lo, gavg, preferred_element_type=F32))
        yield
        dev = acc - mu
        var = jnp.dot((dev * dev).astype(BF16), gavg, preferred_element_type=F32)
        yield
        cn = dev * lax.rsqrt(var + EPS) * cng_ref[...] + cnb_ref[...]
        c_out = _silu(cn).astype(BF16)
        yield

        u = _gelu(u)
        v = _gelu(v)
        yield
        g_cols = []
        for hh in range(SGU_HEADS):
            vh = v[:, hh * hd:(hh + 1) * hd]
            mu_h = jnp.mean(vh, axis=-1, keepdims=True)
            dh = vh - mu_h
            var_h = jnp.mean(dh * dh, axis=-1, keepdims=True)
            vn = (dh * lax.rsqrt(var_h + EPS) * sng_ref[:, hh * hd:(hh + 1) * hd]
                  + snb_ref[:, hh * hd:(hh + 1) * hd]).astype(BF16)
            ws = jnp.where(tril, sw_ref[hh], 0.0).astype(BF16)
            rows = []
            for ci in range(sub // CHUNK):
                sp = jnp.dot(ws, vn[ci * CHUNK:(ci + 1) * CHUNK, :], preferred_element_type=F32)
                rows.append(sp + sb_ref[hh])
            sp_h = rows[0] if len(rows) == 1 else jnp.concatenate(rows, axis=0)
            g_cols.append((u[:, hh * hd:(hh + 1) * hd] * sp_h).astype(BF16))
            if hh % 2 == 1:
                yield
        g_out = jnp.concatenate(g_cols, axis=1)

        yc = jnp.dot(c_out, wout_ref[0:cc, :], preferred_element_type=F32)
        yield
        yg = jnp.dot(g_out, wout_ref[cc:cc + sc, :], preferred_element_type=F32)
        o_ref[0, r0:r0 + sub, :] = x + (yc + yg)
        yield

    P = _MIX_PIECES
    if ffn_layer is None:
        plan = [(sub_tile(j), P * j, P * _MIX_STAGES) for j in range(nsub)]
    else:
        plan = [(sub_tile(0), 0, P * _MIX_STAGES)]
        plan += [(sub_tile(j), P * (2 * j - 1), P * (_MIX_STAGES + 1)) for j in range(1, nsub)]
        plan += [(next_first_sub_tile(), P * (2 * nsub - 1), P)]
    for tick in range(max(t0 + n for _, t0, n in plan)):
        for gen, t0, n in reversed(plan):
            if 0 <= tick - t0 < n:
                next(gen)

    cbuf_ref[0:CONV_HALO, :] = cbuf_ref[ts:ts + CONV_HALO, :]


def _mixer(x, layer, norm_mix, w_in, conv_w, conv_b, conv_ng, conv_nb, sgu_ng, sgu_nb, sgu_w, sgu_b,
           w_out, *, ts, sub, cast=(), proj=None, ffn=None):
    B, S, D = x.shape
    L = conv_w.shape[0]
    cc = conv_w.shape[-1]
    sc = sgu_ng.shape[-1]
    d_in = w_in.shape[-1]
    hd = sc // SGU_HEADS
    assert S % ts == 0 and ts % sub == 0 and sub % CHUNK == 0 and d_in == 2 * cc + 2 * sc
    ns = S // ts
    gs = cc // CONV_GROUPS
    gid = jnp.arange(cc) // gs
    gavg = jnp.where(gid[:, None] == gid[None, :], 1.0 / gs, 0.0).astype(BF16)
    sb_full = jnp.broadcast_to(sgu_b[:, :, :, None], (L, SGU_HEADS, CHUNK, hd))
    whole = lambda p: pl.BlockSpec(p.shape, lambda b, s: (0,) * p.ndim)
    layer_block = lambda *shape: pl.BlockSpec((None,) + shape, lambda b, s: (layer,) + (0,) * len(shape))
    cast_arrs, cast_in_specs, cast_out_specs, cast_shapes = _cast_specs(
        cast, B * ns, lambda b, s: b * ns + s)
    kern = functools.partial(_mixer_kernel, layer=layer, ts=ts, sub=sub, cc=cc, sc=sc, n_cast=len(cast),
                             ffn_layer=None if ffn is None else ffn[0])
    scratch = [pltpu.VMEM((CONV_HALO + ts, cc), F32)]
    if proj is None:
        w_in_spec, w_out_spec = layer_block(D, d_in), layer_block(cc + sc, D)
        scratch += [pltpu.VMEM((D, d_in), BF16), pltpu.VMEM((cc + sc, D), BF16)]
    else:
        w_in, w_out = proj
        assert w_in.dtype == BF16 and w_out.dtype == BF16
        w_in_spec, w_out_spec = whole(w_in), whole(w_out)
    ffn_args, ffn_specs = (), []
    if ffn is not None:
        def next_first(b, s):
            step = jnp.minimum(b * ns + s + 1, B * ns - 1)
            return (step // ns, (step % ns) * (ts // sub), 0)

        ffn_args = (x,) + tuple(ffn[1:])
        ffn_specs = [pl.BlockSpec((1, sub, D), next_first)] + [whole(a) for a in ffn[1:]]
        scratch += [pltpu.VMEM((sub, D), F32)]
    in_specs = [
        pl.BlockSpec((1, ts, D), lambda b, s: (b, s, 0)),
        whole(norm_mix),
        w_in_spec,
        layer_block(CONV_WIDTH, 1, cc),
        whole(conv_b), whole(conv_ng), whole(conv_nb),
        whole(sgu_ng), whole(sgu_nb),
        layer_block(SGU_HEADS, CHUNK, CHUNK),
        layer_block(SGU_HEADS, CHUNK, hd),
        w_out_spec,
        whole(gavg),
    ]
    assert len(in_specs) == _MIXER_INPUTS and len(ffn_args) in (0, _FFN_INPUTS)
    in_specs += ffn_specs
    return pl.pallas_call(
        kern,
        grid=(B, ns),
        in_specs=in_specs + cast_in_specs,
        out_specs=[pl.BlockSpec((1, ts, D), lambda b, s: (b, s, 0))] + cast_out_specs,
        out_shape=[jax.ShapeDtypeStruct((B, S, D), F32)] + cast_shapes,
        scratch_shapes=scratch,
        compiler_params=pltpu.CompilerParams(
            dimension_semantics=("arbitrary", "arbitrary"),
            vmem_limit_bytes=VMEM_LIMIT_BYTES),
        name="mixer",
    )(x, norm_mix, w_in, conv_w, conv_b, conv_ng, conv_nb, sgu_ng, sgu_nb, sgu_w, sb_full, w_out,
      gavg, *ffn_args, *cast_arrs)


def _swiglu_tile(h, wg_ref, wu_ref, wd_ref, ff_chunk):
    ff = wg_ref.shape[-1]
    out = None
    for c0 in range(0, ff, ff_chunk):
        c1 = min(ff, c0 + ff_chunk)
        g = jnp.dot(h, wg_ref[:, c0:c1], preferred_element_type=F32)
        u = jnp.dot(h, wu_ref[:, c0:c1], preferred_element_type=F32)
        a = (_silu(g) * u).astype(BF16)
        y = jnp.dot(a, wd_ref[c0:c1, :], preferred_element_type=F32)
        out = y if out is None else out + y
    return out


def _round_up(n, m):
    return (n + m - 1) // m * m


def _piece_sizes(largest):
    sizes = []
    s = largest
    while s >= BF16_ROWS:
        sizes.append(s)
        s //= 2
    assert sizes and sizes[-1] == BF16_ROWS
    return sizes


def _for_each_piece(count, sizes, fn):
    for s in sizes:
        offset = jnp.bitwise_and(count, -2 * s)

        @pl.when(jnp.bitwise_and(count, s) != 0)
        def _():
            fn(offset, s)


def _route_kernel(x_ref, nw_ref, rt_ref, hn_ref, dcol_ref, drow_ref, cnt_ref, *, win, n_experts):
    for k in range(x_ref.shape[0] // win):
        rows = slice(k * win, (k + 1) * win)
        _route_window(x_ref.at[rows], nw_ref, rt_ref, hn_ref.at[rows],
                      dcol_ref.at[rows], drow_ref.at[k], cnt_ref.at[k], n_experts)


def _route_window(x_ref, nw_ref, rt_ref, hn_ref, dcol_ref, drow_ref, cnt_ref, n_experts):
    x = x_ref[...]
    h = _rms_norm(x, nw_ref[...]).astype(BF16)
    hn_ref[...] = h
    logits = jnp.dot(h, rt_ref[...], preferred_element_type=F32)
    lane = lax.broadcasted_iota(jnp.int32, logits.shape, 1)
    lt = jnp.transpose(jnp.where(lane < n_experts, logits, -jnp.inf))[0:SUBLANES, :]
    win = lt.shape[1]
    sub = lax.broadcasted_iota(jnp.int32, lt.shape, 0)
    m1 = jnp.max(lt, axis=0, keepdims=True)
    i1 = jnp.min(jnp.where(lt == m1, sub, SUBLANES), axis=0, keepdims=True)
    rest = jnp.where(sub == i1, -jnp.inf, lt)
    m2 = jnp.max(rest, axis=0, keepdims=True)
    i2 = jnp.min(jnp.where(rest == m2, sub, SUBLANES), axis=0, keepdims=True)
    e2 = jnp.exp(m2 - m1)
    den = 1.0 + e2
    g1 = 1.0 / den
    g2 = e2 / den
    sel1 = sub == i1
    sel2 = sub == i2
    self = jnp.logical_or(sel1, sel2).astype(F32)
    tok = lax.broadcasted_iota(jnp.int32, lt.shape, 1)
    inc = self
    step = 1
    while step < win:
        inc = inc + jnp.where(tok >= step, pltpu.roll(inc, step, axis=1), 0.0)
        step *= 2
    rank = inc - self
    cnt = jnp.sum(self, axis=1, keepdims=True)
    cnt_pad = jnp.floor((cnt + (BF16_ROWS - 1.0)) * (1.0 / BF16_ROWS)) * BF16_ROWS
    seg = jnp.zeros_like(cnt_pad)
    for e in range(n_experts - 1):
        seg = seg + jnp.where(sub[:, 0:1] > e, cnt_pad[e:e + 1, :], 0.0)
    pos = seg + rank
    dest1 = jnp.sum(jnp.where(sel1, pos, 0.0), axis=0, keepdims=True)
    dest2 = jnp.sum(jnp.where(sel2, pos, 0.0), axis=0, keepdims=True)
    info = jnp.where(sub == 0, dest1, jnp.where(sub == 1, dest2,
                     jnp.where(sub == 2, g1, jnp.where(sub == 3, g2, -1.0))))
    drow_ref[...] = info
    cols = jnp.concatenate([info, jnp.broadcast_to(cnt, (SUBLANES, win)),
                            jnp.full((LANES - 2 * SUBLANES, win), -1.0, F32)], axis=0)
    cols = jnp.transpose(cols)
    dcol_ref[...] = cols
    cnt_ref[...] = cols[0:1, :].astype(jnp.int32)


def _route(x2, nw, router, *, win, per_step=4):
    T, D = x2.shape
    E = router.shape[-1]
    assert T % (win * per_step) == 0 and E <= SUBLANES
    nwin = T // win
    blk = win * per_step
    rt = jnp.zeros((D, LANES), F32).at[:, :E].set(router).astype(BF16)
    kern = functools.partial(_route_kernel, win=win, n_experts=E)
    return pl.pallas_call(
        kern,
        grid=(nwin // per_step,),
        in_specs=[
            pl.BlockSpec((blk, D), lambda w: (w, 0)),
            pl.BlockSpec((1, D), lambda w: (0, 0)),
            pl.BlockSpec((D, LANES), lambda w: (0, 0)),
        ],
        out_specs=[
            pl.BlockSpec((blk, D), lambda w: (w, 0)),
            pl.BlockSpec((blk, LANES), lambda w: (w, 0)),
            pl.BlockSpec((per_step, SUBLANES, win), lambda w: (w, 0, 0)),
            pl.BlockSpec((per_step, 1, LANES), lambda w: (w, 0, 0)),
        ],
        out_shape=[
            jax.ShapeDtypeStruct((T, D), BF16),
            jax.ShapeDtypeStruct((T, LANES), F32),
            jax.ShapeDtypeStruct((nwin, SUBLANES, win), F32),
            jax.ShapeDtypeStruct((nwin, 1, LANES), jnp.int32),
        ],
        compiler_params=pltpu.CompilerParams(
            dimension_semantics=("arbitrary",), vmem_limit_bytes=VMEM_LIMIT_BYTES),
        name="moe_route",
    )(x2, nw.reshape(1, D).astype(F32), rt)


def _compact_kernel(seg_ref, off_ref, cnt_ref, fill_ref, hn_ref, drow_ref, *rest,
                    win, n_experts, rows, tm, n_cast):
    cast_in, xs_ref, cast_out = rest[:n_cast], rest[n_cast], rest[n_cast + 1:2 * n_cast + 1]
    stage_ref, zero_ref, sem = rest[2 * n_cast + 1:]
    _cast_blocks(cast_in, cast_out)
    w = pl.program_id(0)
    nwin = pl.num_programs(0)
    slot = w % 2
    d = hn_ref.shape[-1]
    sizes = _piece_sizes(win)

    def segments(ww, sl, start):
        for e in range(n_experts):
            src0 = seg_ref[ww * n_experts + e]
            dst0 = off_ref[ww * n_experts + e]

            def piece(offset, size):
                cp = pltpu.make_async_copy(
                    stage_ref.at[sl, pl.ds(pl.multiple_of(src0 + offset, BF16_ROWS), size)],
                    xs_ref.at[pl.ds(pl.multiple_of(dst0 + offset, BF16_ROWS), size)],
                    sem.at[sl, e])
                cp.start() if start else cp.wait()

            _for_each_piece(cnt_ref[ww * n_experts + e], sizes, piece)

    def zero_rows(dst, size, e, start):
        cp = pltpu.make_async_copy(zero_ref.at[pl.ds(0, size)],
                                   xs_ref.at[pl.ds(pl.multiple_of(dst, BF16_ROWS), size)],
                                   sem.at[2, e])
        cp.start() if start else cp.wait()

    @pl.when(w == 0)
    def _():
        zero_ref[...] = jnp.zeros(zero_ref.shape, BF16)

    d1 = drow_ref[0, 0:1, :]
    d2 = drow_ref[0, 1:2, :]
    g1 = drow_ref[0, 2:3, :]
    g2 = drow_ref[0, 3:4, :]
    r = lax.broadcasted_iota(jnp.int32, (rows, win), 0).astype(F32)
    m1 = r == d1
    m2 = r == d2
    onehot = jnp.logical_or(m1, m2).astype(BF16)
    xg = jnp.dot(onehot, hn_ref[...], preferred_element_type=F32)
    stage_ref[slot, :, 0:d] = xg.astype(BF16)
    gate = jnp.sum(jnp.where(m1, g1, 0.0) + jnp.where(m2, g2, 0.0), axis=-1, keepdims=True)
    hi = gate.astype(BF16).astype(F32)
    mid = (gate - hi).astype(BF16).astype(F32)
    lo = gate - hi - mid
    lane = lax.broadcasted_iota(jnp.int32, (rows, LANES), 1)
    terms = jnp.where(lane == 0, hi, jnp.where(lane == 1, mid, jnp.where(lane == 2, lo, 0.0)))
    stage_ref[slot, :, d:d + LANES] = terms.astype(BF16)

    segments(w, slot, True)

    @pl.when(w > 0)
    def _():
        segments(w - 1, 1 - slot, False)

    @pl.when(w == nwin - 1)
    def _():
        segments(w, slot, False)
        gap_sizes = _piece_sizes(tm // 2)
        for start in (True, False):
            for e in range(n_experts):
                _for_each_piece(fill_ref[n_experts + e], gap_sizes,
                                lambda offset, size, e=e: zero_rows(fill_ref[e] + offset, size, e, start))
        tail0 = fill_ref[2 * n_experts]
        ntail = fill_ref[2 * n_experts + 1]
        tail_rows = zero_ref.shape[0]

        def tail_start(j, c):
            zero_rows(tail0 + j * tail_rows, tail_rows, 0, True)
            return c

        def tail_wait(j, c):
            zero_rows(tail0 + j * tail_rows, tail_rows, 0, False)
            return c

        lax.fori_loop(0, ntail, tail_start, 0)
        lax.fori_loop(0, ntail, tail_wait, 0)


def _compact(seg, off, cnt_pad, fill, hn, drow, *, win, n_experts, cap, tm, cast=()):
    T, D = hn.shape
    nwin = T // win
    rows = _round_up(TOP_K * win + n_experts * (BF16_ROWS - 1), BF16_ROWS)
    cast_arrs, cast_in_specs, cast_out_specs, cast_shapes = _cast_specs(cast, nwin, lambda w, *_: w)
    kern = functools.partial(_compact_kernel, win=win, n_experts=n_experts, rows=rows, tm=tm,
                             n_cast=len(cast))
    return pl.pallas_call(
        kern,
        grid_spec=pltpu.PrefetchScalarGridSpec(
            num_scalar_prefetch=4,
            grid=(nwin,),
            in_specs=[
                pl.BlockSpec((win, D), lambda w, *_: (w, 0)),
                pl.BlockSpec((1, SUBLANES, win), lambda w, *_: (w, 0, 0)),
            ] + cast_in_specs,
            out_specs=[pl.BlockSpec(memory_space=pl.ANY)] + cast_out_specs,
            scratch_shapes=[pltpu.VMEM((2, rows, D + LANES), BF16),
                            pltpu.VMEM((tm // 2, D + LANES), BF16),
                            pltpu.SemaphoreType.DMA((3, n_experts))],
        ),
        out_shape=[jax.ShapeDtypeStruct((cap, D + LANES), BF16)] + cast_shapes,
        compiler_params=pltpu.CompilerParams(
            dimension_semantics=("arbitrary",), vmem_limit_bytes=VMEM_LIMIT_BYTES),
        name="moe_compact",
    )(seg, off, cnt_pad, fill, hn, drow, *cast_arrs)


def _expert_kernel(te_ref, tv_ref, xs_ref, wg_ref, wu_ref, wd_ref, ys_ref, *, ff_chunk):
    i = pl.program_id(0)
    d = ys_ref.shape[-1]

    @pl.when(tv_ref[i] > 0)
    def _():
        y = _swiglu_tile(xs_ref[:, 0:d], wg_ref.at[0], wu_ref.at[0], wd_ref.at[0], ff_chunk)
        gate = jnp.sum(xs_ref[:, d:d + LANES].astype(F32), axis=-1, keepdims=True)
        ys_ref[...] = (gate * y).astype(BF16)

    @pl.when(tv_ref[i] == 0)
    def _():
        ys_ref[...] = jnp.zeros(ys_ref.shape, BF16)


def _experts(tile_e, tile_v, xs, wg, wu, wd, *, tm, ff_chunk):
    cap = xs.shape[0]
    _, D, FF = wg.shape
    assert cap == tile_e.shape[0] * tm and xs.shape[1] == D + LANES
    kern = functools.partial(_expert_kernel, ff_chunk=ff_chunk)
    return pl.pallas_call(
        kern,
        grid_spec=pltpu.PrefetchScalarGridSpec(
            num_scalar_prefetch=2,
            grid=(cap // tm,),
            in_specs=[
                pl.BlockSpec((tm, D + LANES), lambda i, te, tv: (i, 0)),
                pl.BlockSpec((1, D, FF), lambda i, te, tv: (te[i], 0, 0)),
                pl.BlockSpec((1, D, FF), lambda i, te, tv: (te[i], 0, 0)),
                pl.BlockSpec((1, FF, D), lambda i, te, tv: (te[i], 0, 0)),
            ],
            out_specs=pl.BlockSpec((tm, D), lambda i, te, tv: (i, 0)),
        ),
        out_shape=jax.ShapeDtypeStruct((cap, D), BF16),
        compiler_params=pltpu.CompilerParams(
            dimension_semantics=("arbitrary",), vmem_limit_bytes=VMEM_LIMIT_BYTES),
        name="moe_experts",
    )(tile_e, tile_v, xs, wg.astype(BF16), wu.astype(BF16), wd.astype(BF16))


def _combine_kernel(seg_ref, off_ref, cnt_ref, x_ref, dcol_ref, fw_ref, ys_ref, o_ref,
                    stage_ref, sem, *, win, n_experts):
    w = pl.program_id(0)
    nwin = pl.num_programs(0)
    slot = w % 2
    rows = stage_ref.shape[1]
    sizes = _piece_sizes(win)

    def segments(ww, sl, start):
        for e in range(n_experts):
            src0 = off_ref[ww * n_experts + e]
            dst0 = seg_ref[ww * n_experts + e]

            def piece(offset, size):
                cp = pltpu.make_async_copy(
                    ys_ref.at[pl.ds(pl.multiple_of(src0 + offset, BF16_ROWS), size)],
                    stage_ref.at[sl, pl.ds(pl.multiple_of(dst0 + offset, BF16_ROWS), size)],
                    sem.at[sl, e])
                cp.start() if start else cp.wait()

            _for_each_piece(cnt_ref[ww * n_experts + e], sizes, piece)

    @pl.when(w == 0)
    def _():
        stage_ref[...] = jnp.zeros(stage_ref.shape, BF16)
        segments(0, 0, True)

    @pl.when(w + 1 < nwin)
    def _():
        segments(w + 1, 1 - slot, True)

    segments(w, slot, False)

    d1 = dcol_ref[:, 0:1]
    d2 = dcol_ref[:, 1:2]
    r = lax.broadcasted_iota(jnp.int32, (win, rows), 1).astype(F32)
    onehot = jnp.logical_or(r == d1, r == d2).astype(BF16)
    y = jnp.dot(onehot, stage_ref[slot], preferred_element_type=F32)
    o_ref[...] = _rms_norm(x_ref[...] + y, fw_ref[...])


def _combine(seg, off, cnt_pad, x2, dcol, fw, ys, *, win, n_experts):
    T, D = x2.shape
    rows = _round_up(TOP_K * win + n_experts * (BF16_ROWS - 1), BF16_ROWS)
    kern = functools.partial(_combine_kernel, win=win, n_experts=n_experts)
    return pl.pallas_call(
        kern,
        grid_spec=pltpu.PrefetchScalarGridSpec(
            num_scalar_prefetch=3,
            grid=(T // win,),
            in_specs=[
                pl.BlockSpec((win, D), lambda w, *_: (w, 0)),
                pl.BlockSpec((win, LANES), lambda w, *_: (w, 0)),
                pl.BlockSpec((1, D), lambda w, *_: (0, 0)),
                pl.BlockSpec(memory_space=pl.ANY),
            ],
            out_specs=pl.BlockSpec((win, D), lambda w, *_: (w, 0)),
            scratch_shapes=[pltpu.VMEM((2, rows, D), BF16),
                            pltpu.SemaphoreType.DMA((2, n_experts))],
        ),
        out_shape=jax.ShapeDtypeStruct((T, D), F32),
        compiler_params=pltpu.CompilerParams(
            dimension_semantics=("arbitrary",), vmem_limit_bytes=VMEM_LIMIT_BYTES),
        name="moe_combine",
    )(seg, off, cnt_pad, x2, dcol, fw.reshape(1, D).astype(F32), ys)


def _moe(x2, nw, router, wg, wu_f32, wd, fw, *, win=MOE_WINDOW, tm=MOE_TILE):
    T, D = x2.shape
    E = router.shape[-1]
    nwin = T // win
    hn, dcol, drow, cnt = _route(x2, nw, router, win=win)

    cnt = cnt.reshape(nwin, LANES)[:, SUBLANES:SUBLANES + E]
    cnt_pad = _round_up(cnt, BF16_ROWS)
    seg = jnp.cumsum(cnt_pad, axis=1) - cnt_pad
    total = jnp.sum(cnt_pad, axis=0)
    region = _round_up(total, tm)
    r_end = jnp.cumsum(region)
    r_start = r_end - region
    off = r_start[None, :] + jnp.cumsum(cnt_pad, axis=0) - cnt_pad
    max_rows = TOP_K * T + nwin * E * (BF16_ROWS - 1)
    ntiles_max = max_rows // tm + E
    cap = ntiles_max * tm
    ti = jnp.arange(ntiles_max)
    t_end = r_end // tm
    tile_e = jnp.minimum(jnp.sum(ti[:, None] >= t_end[None, :], axis=1), E - 1).astype(jnp.int32)
    tile_v = (ti < t_end[-1]).astype(jnp.int32)
    tail = jnp.stack([r_end[-1], (cap - r_end[-1]) // (tm // 2)])
    fill = jnp.concatenate([r_start + total, region - total, tail])
    flat = lambda a: a.reshape(-1).astype(jnp.int32)
    seg, off, cnt_pad, fill = flat(seg), flat(off), flat(cnt_pad), flat(fill)

    xs, wu = _compact(seg, off, cnt_pad, fill, hn, drow, win=win, n_experts=E, cap=cap, tm=tm,
                      cast=((wu_f32, 0),))
    ys = _experts(tile_e, tile_v, xs, wg, wu.reshape(wg.shape), wd, tm=tm, ff_chunk=wg.shape[-1])
    return _combine(seg, off, cnt_pad, x2, dcol, fw, ys, win=win, n_experts=E)


def kernel(x, norm_mix, w_in, conv_w, conv_b, conv_ng, conv_nb, sgu_ng, sgu_nb, sgu_w, sgu_b, w_out,
           norm_ffn, ffn_wg, ffn_wu, ffn_wd, moe_router, moe_wg, moe_wu, moe_wd, norm_final):
    B, S, D = x.shape
    depth = norm_mix.shape[0]
    assert depth == 2, "trunk is one dense layer followed by one MoE layer"
    mix = functools.partial(_mixer, ts=512, sub=256)
    mixer_params = (norm_mix, w_in, conv_w, conv_b, conv_ng, conv_nb, sgu_ng, sgu_nb, sgu_w, sgu_b, w_out)
    x, ffn_wg_b, ffn_wu_b, ffn_wd_b, w_in1_b, w_out1_b, moe_wg_b = mix(
        x, 0, *mixer_params,
        cast=((ffn_wg, 0), (ffn_wu, 0), (ffn_wd, 0), (w_in, 1), (w_out, 1), (moe_wg, 0)))
    x, moe_wd_b = mix(x, 1, *mixer_params, proj=(w_in1_b, w_out1_b),
                      ffn=(0, norm_ffn, ffn_wg_b, ffn_wu_b, ffn_wd_b), cast=((moe_wd, 0),))
    y2 = _moe(x.reshape(B * S, D), norm_ffn[1], moe_router[0], moe_wg_b.reshape(moe_wg[0].shape),
              moe_wu, moe_wd_b.reshape(moe_wd[0].shape), norm_final)
    return y2.reshape(B, S, D)
```

```python
import functools

import jax
import jax.numpy as jnp
from jax import lax
from jax.experimental import pallas as pl
from jax.experimental.pallas import tpu as pltpu

F32 = jnp.float32
BF16 = jnp.bfloat16

EPS = 1e-6
CONV_WIDTH = 31
CONV_GROUPS = 8
SGU_HEADS = 4
CHUNK = 128
TOP_K = 2

SUBLANES = 8
LANES = 128
BF16_ROWS = 16
CONV_HALO = 32
VMEM_LIMIT_BYTES = 58 * 1024 * 1024
_MIX_STAGES = 3
_MIX_PIECES = SUBLANES

MOE_WINDOW = 512
MOE_TILE = 512


def _rms_norm(x, g):
    ms = jnp.mean(x * x, axis=-1, keepdims=True)
    return x * lax.rsqrt(ms + EPS) * g


def _gelu(x):
    return 0.5 * x * (1.0 + lax.erf(x * (2.0 ** -0.5)))


def _sigmoid(x):
    return 0.5 * jnp.tanh(0.5 * x) + 0.5


def _silu(x):
    h = 0.5 * x
    return h * jnp.tanh(h) + h


def _cast_plan(rows, nsteps):
    nblk = nsteps
    while rows % nblk or (rows // nblk) % BF16_ROWS:
        nblk //= 2
        assert nblk >= 1, rows
    return nblk, rows // nblk


def _cast_specs(items, nsteps, step_of):
    arrs, in_specs, out_specs, shapes = [], [], [], []
    for a, layer in items:
        a = a.reshape(a.shape[0], -1, a.shape[-1])
        nblk, blk = _cast_plan(a.shape[1], nsteps)
        blk_of = lambda *g, nblk=nblk: jnp.minimum(step_of(*g), nblk - 1)
        arrs.append(a)
        in_specs.append(pl.BlockSpec((None, blk, a.shape[2]),
                                     lambda *g, layer=layer, blk_of=blk_of: (layer, blk_of(*g), 0)))
        out_specs.append(pl.BlockSpec((blk, a.shape[2]), lambda *g, blk_of=blk_of: (blk_of(*g), 0)))
        shapes.append(jax.ShapeDtypeStruct(a.shape[1:], BF16))
    return arrs, in_specs, out_specs, shapes


def _cast_blocks(src_refs, dst_refs):
    for src, dst in zip(src_refs, dst_refs):
        dst[...] = src[...].astype(BF16)


_MIXER_INPUTS = 13
_FFN_INPUTS = 5
_FFN_COLS = 256


def _mixer_kernel(*refs, layer, ts, sub, cc, sc, n_cast, ffn_layer):
    (x_ref, nw_all, win_in_ref, cw_ref, cb_all, cng_all, cnb_all, sng_all, snb_all,
     sw_ref, sb_ref, wout_in_ref, gavg_ref) = refs[:_MIXER_INPUTS]
    nw_ref, cb_ref, cng_ref, cnb_ref, sng_ref, snb_ref = (
        r.at[layer:layer + 1] for r in (nw_all, cb_all, cng_all, cnb_all, sng_all, snb_all))
    n_in = _MIXER_INPUTS
    if ffn_layer is not None:
        xn_ref, nwf_all, wg_ref, wu_ref, wd_ref = refs[n_in:n_in + _FFN_INPUTS]
        nwf_ref = nwf_all.at[ffn_layer:ffn_layer + 1]
        n_in += _FFN_INPUTS
    cast_in = refs[n_in:n_in + n_cast]
    o_ref = refs[n_in + n_cast]
    cast_out = refs[n_in + n_cast + 1:n_in + 2 * n_cast + 1]
    scratch = refs[n_in + 2 * n_cast + 1:]
    cbuf_ref = scratch[0]
    s = pl.program_id(1)
    _cast_blocks(cast_in, cast_out)

    if win_in_ref.dtype == BF16:
        win_ref, wout_ref = win_in_ref, wout_in_ref
    else:
        win_ref, wout_ref = scratch[1:3]

        @pl.when(jnp.logical_and(pl.program_id(0) == 0, s == 0))
        def _():
            win_ref[...] = win_in_ref[...].astype(BF16)
            wout_ref[...] = wout_in_ref[...].astype(BF16)

    @pl.when(s == 0)
    def _():
        cbuf_ref[0:CONV_HALO, :] = jnp.zeros((CONV_HALO, cc), F32)

    row = lax.broadcasted_iota(jnp.int32, (CHUNK, CHUNK), 0)
    col = lax.broadcasted_iota(jnp.int32, (CHUNK, CHUNK), 1)
    tril = row >= col
    hd = sc // SGU_HEADS
    nq = -(-CONV_WIDTH // SUBLANES)

    nsub = ts // sub
    d_in = 2 * cc + 2 * sc
    cbw = d_in // _MIX_PIECES

    def ffn_stage(x):
        hf = _rms_norm(x, nwf_ref[...]).astype(BF16)
        ff = wg_ref.shape[-1]
        fcols = [(c0, min(ff, c0 + _FFN_COLS)) for c0 in range(0, ff, _FFN_COLS)]
        per = -(-len(fcols) // 3)
        gu = []
        for w_ref in (wg_ref, wu_ref):
            blocks = []
            for i, (c0, c1) in enumerate(fcols):
                blocks.append(jnp.dot(hf, w_ref[:, c0:c1], preferred_element_type=F32))
                if i % per == per - 1 or i == len(fcols) - 1:
                    yield
            gu.append(jnp.concatenate(blocks, axis=1))
        act = (_silu(gu[0]) * gu[1]).astype(BF16)
        dm = wd_ref.shape[-1]
        yblk = []
        for i in range(4):
            c0, c1 = i * dm // 4, (i + 1) * dm // 4
            yblk.append(jnp.dot(act, wd_ref[:, c0:c1], preferred_element_type=F32))
            if i % 2 == 1 and i < 3:
                yield
        return x + jnp.concatenate(yblk, axis=1)

    if ffn_layer is not None:
        xcarry_ref = scratch[-1]

        @pl.when(jnp.logical_and(pl.program_id(0) == 0, s == 0))
        def _():
            stage = ffn_stage(x_ref[0, 0:sub, :])
            try:
                while True:
                    next(stage)
            except StopIteration as done:
                xcarry_ref[...] = done.value

    def next_first_sub_tile():
        xcarry_ref[...] = yield from ffn_stage(xn_ref[0])
        yield

    def sub_tile(j):
        r0 = j * sub
        if ffn_layer is None:
            x = x_ref[0, r0:r0 + sub, :]
        elif j == 0:
            x = xcarry_ref[...]
        else:
            x = yield from ffn_stage(x_ref[0, r0:r0 + sub, :])
            yield
        h = _rms_norm(x, nw_ref[...]).astype(BF16)
        zblk = []
        for k in range(_MIX_PIECES):
            zblk.append(jnp.dot(h, win_ref[:, k * cbw:(k + 1) * cbw], preferred_element_type=F32))
            yield
        z = jnp.concatenate(zblk, axis=1)
        a = z[:, :cc]
        gate = z[:, cc:2 * cc]
        u = z[:, 2 * cc:2 * cc + sc]
        v = z[:, 2 * cc + sc:]

        base = CONV_HALO + r0
        cbuf_ref[base:base + sub, :] = a * _sigmoid(gate)
        acc = jnp.broadcast_to(cb_ref[...], (sub, cc))
        for b in range(SUBLANES):
            zb = None
            for q in range(nq):
                d = SUBLANES * q + b
                if d >= CONV_WIDTH:
                    continue
                start = base - SUBLANES * (q + 1)
                term = (cw_ref[CONV_WIDTH - 1 - d]
                        * cbuf_ref[start:start + sub + SUBLANES, :])
                zb = term if zb is None else zb + term
            acc = acc + zb[SUBLANES - b:SUBLANES - b + sub, :]
            yield

        gavg = gavg_ref[...]
        acc_hi = acc.astype(BF16)
        acc_lo = (acc - acc_hi.astype(F32)).astype(BF16)
        mu = (jnp.dot(acc_hi, gavg, preferred_element_type=F32)
              + jnp.dot(acc_lo, gavg, preferred_element_type=F32))
        yield
        dev = acc - mu
        var = jnp.dot((dev * dev).astype(BF16), gavg, preferred_element_type=F32)
        yield
        cn = dev * lax.rsqrt(var + EPS) * cng_ref[...] + cnb_ref[...]
        c_out = _silu(cn).astype(BF16)
        yield

        u = _gelu(u)
        v = _gelu(v)
        yield
        g_cols = []
        for hh in range(SGU_HEADS):
            vh = v[:, hh * hd:(hh + 1) * hd]
            mu_h = jnp.mean(vh, axis=-1, keepdims=True)
            dh = vh - mu_h
            var_h = jnp.mean(dh * dh, axis=-1, keepdims=True)
            vn = (dh * lax.rsqrt(var_h + EPS) * sng_ref[:, hh * hd:(hh + 1) * hd]
                  + snb_ref[:, hh * hd:(hh + 1) * hd]).astype(BF16)
            ws = jnp.where(tril, sw_ref[hh], 0.0).astype(BF16)
            rows = []
            for ci in range(sub // CHUNK):
                sp = jnp.dot(ws, vn[ci * CHUNK:(ci + 1) * CHUNK, :], preferred_element_type=F32)
                rows.append(sp + sb_ref[hh])
            sp_h = rows[0] if len(rows) == 1 else jnp.concatenate(rows, axis=0)
            g_cols.append((u[:, hh * hd:(hh + 1) * hd] * sp_h).astype(BF16))
            if hh % 2 == 1:
                yield
        g_out = jnp.concatenate(g_cols, axis=1)

        yc = jnp.dot(c_out, wout_ref[0:cc, :], preferred_element_type=F32)
        yield
        yg = jnp.dot(g_out, wout_ref[cc:cc + sc, :], preferred_element_type=F32)
        o_ref[0, r0:r0 + sub, :] = x + (yc + yg)
        yield

    P = _MIX_PIECES
    if ffn_layer is None:
        plan = [(sub_tile(j), P * j, P * _MIX_STAGES) for j in range(nsub)]
    else:
        plan = [(sub_tile(0), 0, P * _MIX_STAGES)]
        plan += [(sub_tile(j), P * (2 * j - 1), P * (_MIX_STAGES + 1)) for j in range(1, nsub)]
        plan += [(next_first_sub_tile(), P * (2 * nsub - 1), P)]
    for tick in range(max(t0 + n for _, t0, n in plan)):
        for gen, t0, n in reversed(plan):
            if 0 <= tick - t0 < n:
                next(gen)

    cbuf_ref[0:CONV_HALO, :] = cbuf_ref[ts:ts + CONV_HALO, :]


def _mixer(x, layer, norm_mix, w_in, conv_w, conv_b, conv_ng, conv_nb, sgu_ng, sgu_nb, sgu_w, sgu_b,
           w_out, *, ts, sub, cast=(), proj=None, ffn=None):
    B, S, D = x.shape
    L = conv_w.shape[0]
    cc = conv_w.shape[-1]
    sc = sgu_ng.shape[-1]
    d_in = w_in.shape[-1]
    hd = sc // SGU_HEADS
    assert S % ts == 0 and ts % sub == 0 and sub % CHUNK == 0 and d_in == 2 * cc + 2 * sc
    ns = S // ts
    gs = cc // CONV_GROUPS
    gid = jnp.arange(cc) // gs
    gavg = jnp.where(gid[:, None] == gid[None, :], 1.0 / gs, 0.0).astype(BF16)
    sb_full = jnp.broadcast_to(sgu_b[:, :, :, None], (L, SGU_HEADS, CHUNK, hd))
    whole = lambda p: pl.BlockSpec(p.shape, lambda b, s: (0,) * p.ndim)
    layer_block = lambda *shape: pl.BlockSpec((None,) + shape, lambda b, s: (layer,) + (0,) * len(shape))
    cast_arrs, cast_in_specs, cast_out_specs, cast_shapes = _cast_specs(
        cast, B * ns, lambda b, s: b * ns + s)
    kern = functools.partial(_mixer_kernel, layer=layer, ts=ts, sub=sub, cc=cc, sc=sc, n_cast=len(cast),
                             ffn_layer=None if ffn is None else ffn[0])
    scratch = [pltpu.VMEM((CONV_HALO + ts, cc), F32)]
    if proj is None:
        w_in_spec, w_out_spec = layer_block(D, d_in), layer_block(cc + sc, D)
        scratch += [pltpu.VMEM((D, d_in), BF16), pltpu.VMEM((cc + sc, D), BF16)]
    else:
        w_in, w_out = proj
        assert w_in.dtype == BF16 and w_out.dtype == BF16
        w_in_spec, w_out_spec = whole(w_in), whole(w_out)
    ffn_args, ffn_specs = (), []
    if ffn is not None:
        def next_first(b, s):
            step = jnp.minimum(b * ns + s + 1, B * ns - 1)
            return (step // ns, (step % ns) * (ts // sub), 0)

        ffn_args = (x,) + tuple(ffn[1:])
        ffn_specs = [pl.BlockSpec((1, sub, D), next_first)] + [whole(a) for a in ffn[1:]]
        scratch += [pltpu.VMEM((sub, D), F32)]
    in_specs = [
        pl.BlockSpec((1, ts, D), lambda b, s: (b, s, 0)),
        whole(norm_mix),
        w_in_spec,
        layer_block(CONV_WIDTH, 1, cc),
        whole(conv_b), whole(conv_ng), whole(conv_nb),
        whole(sgu_ng), whole(sgu_nb),
        layer_block(SGU_HEADS, CHUNK, CHUNK),
        layer_block(SGU_HEADS, CHUNK, hd),
        w_out_spec,
        whole(gavg),
    ]
    assert len(in_specs) == _MIXER_INPUTS and len(ffn_args) in (0, _FFN_INPUTS)
    in_specs += ffn_specs
    return pl.pallas_call(
        kern,
        grid=(B, ns),
        in_specs=in_specs + cast_in_specs,
        out_specs=[pl.BlockSpec((1, ts, D), lambda b, s: (b, s, 0))] + cast_out_specs,
        out_shape=[jax.ShapeDtypeStruct((B, S, D), F32)] + cast_shapes,
        scratch_shapes=scratch,
        compiler_params=pltpu.CompilerParams(
            dimension_semantics=("arbitrary", "arbitrary"),
            vmem_limit_bytes=VMEM_LIMIT_BYTES),
        name="mixer",
    )(x, norm_mix, w_in, conv_w, conv_b, conv_ng, conv_nb, sgu_ng, sgu_nb, sgu_w, sb_full, w_out,
      gavg, *ffn_args, *cast_arrs)


def _swiglu_tile(h, wg_ref, wu_ref, wd_ref, ff_chunk):
    ff = wg_ref.shape[-1]
    out = None
    for c0 in range(0, ff, ff_chunk):
        c1 = min(ff, c0 + ff_chunk)
        g = jnp.dot(h, wg_ref[:, c0:c1], preferred_element_type=F32)
        u = jnp.dot(h, wu_ref[:, c0:c1], preferred_element_type=F32)
        a = (_silu(g) * u).astype(BF16)
        y = jnp.dot(a, wd_ref[c0:c1, :], preferred_element_type=F32)
        out = y if out is None else out + y
    return out


def _round_up(n, m):
    return (n + m - 1) // m * m


def _piece_sizes(largest):
    sizes = []
    s = largest
    while s >= BF16_ROWS:
        sizes.append(s)
        s //= 2
    assert sizes and sizes[-1] == BF16_ROWS
    return sizes


def _for_each_piece(count, sizes, fn):
    for s in sizes:
        offset = jnp.bitwise_and(count, -2 * s)

        @pl.when(jnp.bitwise_and(count, s) != 0)
        def _():
            fn(offset, s)


def _route_kernel(x_ref, nw_ref, rt_ref, hn_ref, dcol_ref, drow_ref, cnt_ref, *, win, n_experts):
    for k in range(x_ref.shape[0] // win):
        rows = slice(k * win, (k + 1) * win)
        _route_window(x_ref.at[rows], nw_ref, rt_ref, hn_ref.at[rows],
                      dcol_ref.at[rows], drow_ref.at[k], cnt_ref.at[k], n_experts)


def _route_window(x_ref, nw_ref, rt_ref, hn_ref, dcol_ref, drow_ref, cnt_ref, n_experts):
    x = x_ref[...]
    h = _rms_norm(x, nw_ref[...]).astype(BF16)
    hn_ref[...] = h
    logits = jnp.dot(h, rt_ref[...], preferred_element_type=F32)
    lane = lax.broadcasted_iota(jnp.int32, logits.shape, 1)
    lt = jnp.transpose(jnp.where(lane < n_experts, logits, -jnp.inf))[0:SUBLANES, :]
    win = lt.shape[1]
    sub = lax.broadcasted_iota(jnp.int32, lt.shape, 0)
    m1 = jnp.max(lt, axis=0, keepdims=True)
    i1 = jnp.min(jnp.where(lt == m1, sub, SUBLANES), axis=0, keepdims=True)
    rest = jnp.where(sub == i1, -jnp.inf, lt)
    m2 = jnp.max(rest, axis=0, keepdims=True)
    i2 = jnp.min(jnp.where(rest == m2, sub, SUBLANES), axis=0, keepdims=True)
    e2 = jnp.exp(m2 - m1)
    den = 1.0 + e2
    g1 = 1.0 / den
    g2 = e2 / den
    sel1 = sub == i1
    sel2 = sub == i2
    self = jnp.logical_or(sel1, sel2).astype(F32)
    tok = lax.broadcasted_iota(jnp.int32, lt.shape, 1)
    inc = self
    step = 1
    while step < win:
        inc = inc + jnp.where(tok >= step, pltpu.roll(inc, step, axis=1), 0.0)
        step *= 2
    rank = inc - self
    cnt = jnp.sum(self, axis=1, keepdims=True)
    cnt_pad = jnp.floor((cnt + (BF16_ROWS - 1.0)) * (1.0 / BF16_ROWS)) * BF16_ROWS
    seg = jnp.zeros_like(cnt_pad)
    for e in range(n_experts - 1):
        seg = seg + jnp.where(sub[:, 0:1] > e, cnt_pad[e:e + 1, :], 0.0)
    pos = seg + rank
    dest1 = jnp.sum(jnp.where(sel1, pos, 0.0), axis=0, keepdims=True)
    dest2 = jnp.sum(jnp.where(sel2, pos, 0.0), axis=0, keepdims=True)
    info = jnp.where(sub == 0, dest1, jnp.where(sub == 1, dest2,
                     jnp.where(sub == 2, g1, jnp.where(sub == 3, g2, -1.0))))
    drow_ref[...] = info
    cols = jnp.concatenate([info, jnp.broadcast_to(cnt, (SUBLANES, win)),
                            jnp.full((LANES - 2 * SUBLANES, win), -1.0, F32)], axis=0)
    cols = jnp.transpose(cols)
    dcol_ref[...] = cols
    cnt_ref[...] = cols[0:1, :].astype(jnp.int32)


def _route(x2, nw, router, *, win, per_step=4):
    T, D = x2.shape
    E = router.shape[-1]
    assert T % (win * per_step) == 0 and E <= SUBLANES
    nwin = T // win
    blk = win * per_step
    rt = jnp.zeros((D, LANES), F32).at[:, :E].set(router).astype(BF16)
    kern = functools.partial(_route_kernel, win=win, n_experts=E)
    return pl.pallas_call(
        kern,
        grid=(nwin // per_step,),
        in_specs=[
            pl.BlockSpec((blk, D), lambda w: (w, 0)),
            pl.BlockSpec((1, D), lambda w: (0, 0)),
            pl.BlockSpec((D, LANES), lambda w: (0, 0)),
        ],
        out_specs=[
            pl.BlockSpec((blk, D), lambda w: (w, 0)),
            pl.BlockSpec((blk, LANES), lambda w: (w, 0)),
            pl.BlockSpec((per_step, SUBLANES, win), lambda w: (w, 0, 0)),
            pl.BlockSpec((per_step, 1, LANES), lambda w: (w, 0, 0)),
        ],
        out_shape=[
            jax.ShapeDtypeStruct((T, D), BF16),
            jax.ShapeDtypeStruct((T, LANES), F32),
            jax.ShapeDtypeStruct((nwin, SUBLANES, win), F32),
            jax.ShapeDtypeStruct((nwin, 1, LANES), jnp.int32),
        ],
        compiler_params=pltpu.CompilerParams(
            dimension_semantics=("arbitrary",), vmem_limit_bytes=VMEM_LIMIT_BYTES),
        name="moe_route",
    )(x2, nw.reshape(1, D).astype(F32), rt)


def _compact_kernel(seg_ref, off_ref, cnt_ref, fill_ref, hn_ref, drow_ref, xs_ref,
                    stage_ref, zero_ref, sem, *, win, n_experts, rows, tm):
    w = pl.program_id(0)
    nwin = pl.num_programs(0)
    slot = w % 2
    d = hn_ref.shape[-1]
    sizes = _piece_sizes(win)

    def segments(ww, sl, start):
        for e in range(n_experts):
            src0 = seg_ref[ww * n_experts + e]
            dst0 = off_ref[ww * n_experts + e]

            def piece(offset, size):
                cp = pltpu.make_async_copy(
                    stage_ref.at[sl, pl.ds(pl.multiple_of(src0 + offset, BF16_ROWS), size)],
                    xs_ref.at[pl.ds(pl.multiple_of(dst0 + offset, BF16_ROWS), size)],
                    sem.at[sl, e])
                cp.start() if start else cp.wait()

            _for_each_piece(cnt_ref[ww * n_experts + e], sizes, piece)

    def zero_rows(dst, size, e, start):
        cp = pltpu.make_async_copy(zero_ref.at[pl.ds(0, size)],
                                   xs_ref.at[pl.ds(pl.multiple_of(dst, BF16_ROWS), size)],
                                   sem.at[2, e])
        cp.start() if start else cp.wait()

    @pl.when(w == 0)
    def _():
        zero_ref[...] = jnp.zeros(zero_ref.shape, BF16)

    d1 = drow_ref[0, 0:1, :]
    d2 = drow_ref[0, 1:2, :]
    g1 = drow_ref[0, 2:3, :]
    g2 = drow_ref[0, 3:4, :]
    r = lax.broadcasted_iota(jnp.int32, (rows, win), 0).astype(F32)
    m1 = r == d1
    m2 = r == d2
    onehot = jnp.logical_or(m1, m2).astype(BF16)
    xg = jnp.dot(onehot, hn_ref[...], preferred_element_type=F32)
    stage_ref[slot, :, 0:d] = xg.astype(BF16)
    gate = jnp.sum(jnp.where(m1, g1, 0.0) + jnp.where(m2, g2, 0.0), axis=-1, keepdims=True)
    hi = gate.astype(BF16).astype(F32)
    mid = (gate - hi).astype(BF16).astype(F32)
    lo = gate - hi - mid
    lane = lax.broadcasted_iota(jnp.int32, (rows, LANES), 1)
    terms = jnp.where(lane == 0, hi, jnp.where(lane == 1, mid, jnp.where(lane == 2, lo, 0.0)))
    stage_ref[slot, :, d:d + LANES] = terms.astype(BF16)

    segments(w, slot, True)

    @pl.when(w > 0)
    def _():
        segments(w - 1, 1 - slot, False)

    @pl.when(w == nwin - 1)
    def _():
        segments(w, slot, False)
        gap_sizes = _piece_sizes(tm // 2)
        for start in (True, False):
            for e in range(n_experts):
                _for_each_piece(fill_ref[n_experts + e], gap_sizes,
                                lambda offset, size, e=e: zero_rows(fill_ref[e] + offset, size, e, start))
        tail0 = fill_ref[2 * n_experts]
        ntail = fill_ref[2 * n_experts + 1]
        tail_rows = zero_ref.shape[0]

        def tail_start(j, c):
            zero_rows(tail0 + j * tail_rows, tail_rows, 0, True)
            return c

        def tail_wait(j, c):
            zero_rows(tail0 + j * tail_rows, tail_rows, 0, False)
            return c

        lax.fori_loop(0, ntail, tail_start, 0)
        lax.fori_loop(0, ntail, tail_wait, 0)


def _compact(seg, off, cnt_pad, fill, hn, drow, *, win, n_experts, cap, tm):
    T, D = hn.shape
    nwin = T // win
    rows = _round_up(TOP_K * win + n_experts * (BF16_ROWS - 1), BF16_ROWS)
    kern = functools.partial(_compact_kernel, win=win, n_experts=n_experts, rows=rows, tm=tm)
    return pl.pallas_call(
        kern,
        grid_spec=pltpu.PrefetchScalarGridSpec(
            num_scalar_prefetch=4,
            grid=(nwin,),
            in_specs=[
                pl.BlockSpec((win, D), lambda w, *_: (w, 0)),
                pl.BlockSpec((1, SUBLANES, win), lambda w, *_: (w, 0, 0)),
            ],
            out_specs=pl.BlockSpec(memory_space=pl.ANY),
            scratch_shapes=[pltpu.VMEM((2, rows, D + LANES), BF16),
                            pltpu.VMEM((tm // 2, D + LANES), BF16),
                            pltpu.SemaphoreType.DMA((3, n_experts))],
        ),
        out_shape=jax.ShapeDtypeStruct((cap, D + LANES), BF16),
        compiler_params=pltpu.CompilerParams(
            dimension_semantics=("arbitrary",), vmem_limit_bytes=VMEM_LIMIT_BYTES),
        name="moe_compact",
    )(seg, off, cnt_pad, fill, hn, drow)


def _expert_kernel(te_ref, tv_ref, xs_ref, wg_ref, wu_ref, wd_ref, ys_ref, *, ff_chunk):
    i = pl.program_id(0)
    d = ys_ref.shape[-1]

    @pl.when(tv_ref[i] > 0)
    def _():
        y = _swiglu_tile(xs_ref[:, 0:d], wg_ref.at[0], wu_ref.at[0], wd_ref.at[0], ff_chunk)
        gate = jnp.sum(xs_ref[:, d:d + LANES].astype(F32), axis=-1, keepdims=True)
        ys_ref[...] = (gate * y).astype(BF16)

    @pl.when(tv_ref[i] == 0)
    def _():
        ys_ref[...] = jnp.zeros(ys_ref.shape, BF16)


def _experts(tile_e, tile_v, xs, wg, wu, wd, *, tm, ff_chunk):
    cap = xs.shape[0]
    _, D, FF = wg.shape
    assert cap == tile_e.shape[0] * tm and xs.shape[1] == D + LANES
    kern = functools.partial(_expert_kernel, ff_chunk=ff_chunk)
    return pl.pallas_call(
        kern,
        grid_spec=pltpu.PrefetchScalarGridSpec(
            num_scalar_prefetch=2,
            grid=(cap // tm,),
            in_specs=[
                pl.BlockSpec((tm, D + LANES), lambda i, te, tv: (i, 0)),
                pl.BlockSpec((1, D, FF), lambda i, te, tv: (te[i], 0, 0)),
                pl.BlockSpec((1, D, FF), lambda i, te, tv: (te[i], 0, 0)),
                pl.BlockSpec((1, FF, D), lambda i, te, tv: (te[i], 0, 0)),
            ],
            out_specs=pl.BlockSpec((tm, D), lambda i, te, tv: (i, 0)),
        ),
        out_shape=jax.ShapeDtypeStruct((cap, D), BF16),
        compiler_params=pltpu.CompilerParams(
            dimension_semantics=("arbitrary",), vmem_limit_bytes=VMEM_LIMIT_BYTES),
        name="moe_experts",
    )(tile_e, tile_v, xs, wg.astype(BF16), wu.astype(BF16), wd.astype(BF16))


def _combine_kernel(seg_ref, off_ref, cnt_ref, x_ref, dcol_ref, fw_ref, ys_ref, o_ref,
                    stage_ref, sem, *, win, n_experts):
    w = pl.program_id(0)
    nwin = pl.num_programs(0)
    slot = w % 2
    rows = stage_ref.shape[1]
    sizes = _piece_sizes(win)

    def segments(ww, sl, start):
        for e in range(n_experts):
            src0 = off_ref[ww * n_experts + e]
            dst0 = seg_ref[ww * n_experts + e]

            def piece(offset, size):
                cp = pltpu.make_async_copy(
                    ys_ref.at[pl.ds(pl.multiple_of(src0 + offset, BF16_ROWS), size)],
                    stage_ref.at[sl, pl.ds(pl.multiple_of(dst0 + offset, BF16_ROWS), size)],
                    sem.at[sl, e])
                cp.start() if start else cp.wait()

            _for_each_piece(cnt_ref[ww * n_experts + e], sizes, piece)

    @pl.when(w == 0)
    def _():
        stage_ref[...] = jnp.zeros(stage_ref.shape, BF16)
        segments(0, 0, True)

    @pl.when(w + 1 < nwin)
    def _():
        segments(w + 1, 1 - slot, True)

    segments(w, slot, False)

    d1 = dcol_ref[:, 0:1]
    d2 = dcol_ref[:, 1:2]
    r = lax.broadcasted_iota(jnp.int32, (win, rows), 1).astype(F32)
    onehot = jnp.logical_or(r == d1, r == d2).astype(BF16)
    y = jnp.dot(onehot, stage_ref[slot], preferred_element_type=F32)
    o_ref[...] = _rms_norm(x_ref[...] + y, fw_ref[...])


def _combine(seg, off, cnt_pad, x2, dcol, fw, ys, *, win, n_experts):
    T, D = x2.shape
    rows = _round_up(TOP_K * win + n_experts * (BF16_ROWS - 1), BF16_ROWS)
    kern = functools.partial(_combine_kernel, win=win, n_experts=n_experts)
    return pl.pallas_call(
        kern,
        grid_spec=pltpu.PrefetchScalarGridSpec(
            num_scalar_prefetch=3,
            grid=(T // win,),
            in_specs=[
                pl.BlockSpec((win, D), lambda w, *_: (w, 0)),
                pl.BlockSpec((win, LANES), lambda w, *_: (w, 0)),
                pl.BlockSpec((1, D), lambda w, *_: (0, 0)),
                pl.BlockSpec(memory_space=pl.ANY),
            ],
            out_specs=pl.BlockSpec((win, D), lambda w, *_: (w, 0)),
            scratch_shapes=[pltpu.VMEM((2, rows, D), BF16),
                            pltpu.SemaphoreType.DMA((2, n_experts))],
        ),
        out_shape=jax.ShapeDtypeStruct((T, D), F32),
        compiler_params=pltpu.CompilerParams(
            dimension_semantics=("arbitrary",), vmem_limit_bytes=VMEM_LIMIT_BYTES),
        name="moe_combine",
    )(seg, off, cnt_pad, x2, dcol, fw.reshape(1, D).astype(F32), ys)


def _moe(x2, nw, router, wg, wu, wd, fw, *, win=MOE_WINDOW, tm=MOE_TILE):
    T, D = x2.shape
    E = router.shape[-1]
    nwin = T // win
    hn, dcol, drow, cnt = _route(x2, nw, router, win=win)

    cnt = cnt.reshape(nwin, LANES)[:, SUBLANES:SUBLANES + E]
    cnt_pad = _round_up(cnt, BF16_ROWS)
    seg = jnp.cumsum(cnt_pad, axis=1) - cnt_pad
    total = jnp.sum(cnt_pad, axis=0)
    region = _round_up(total, tm)
    r_end = jnp.cumsum(region)
    r_start = r_end - region
    off = r_start[None, :] + jnp.cumsum(cnt_pad, axis=0) - cnt_pad
    max_rows = TOP_K * T + nwin * E * (BF16_ROWS - 1)
    ntiles_max = max_rows // tm + E
    cap = ntiles_max * tm
    ti = jnp.arange(ntiles_max)
    t_end = r_end // tm
    tile_e = jnp.minimum(jnp.sum(ti[:, None] >= t_end[None, :], axis=1), E - 1).astype(jnp.int32)
    tile_v = (ti < t_end[-1]).astype(jnp.int32)
    tail = jnp.stack([r_end[-1], (cap - r_end[-1]) // (tm // 2)])
    fill = jnp.concatenate([r_start + total, region - total, tail])
    flat = lambda a: a.reshape(-1).astype(jnp.int32)
    seg, off, cnt_pad, fill = flat(seg), flat(off), flat(cnt_pad), flat(fill)

    xs = _compact(seg, off, cnt_pad, fill, hn, drow, win=win, n_experts=E, cap=cap, tm=tm)
    ys = _experts(tile_e, tile_v, xs, wg, wu, wd, tm=tm, ff_chunk=wg.shape[-1])
    return _combine(seg, off, cnt_pad, x2, dcol, fw, ys, win=win, n_experts=E)


def kernel(x, norm_mix, w_in, conv_w, conv_b, conv_ng, conv_nb, sgu_ng, sgu_nb, sgu_w, sgu_b, w_out,
           norm_ffn, ffn_wg, ffn_wu, ffn_wd, moe_router, moe_wg, moe_wu, moe_wd, norm_final):
    B, S, D = x.shape
    depth = norm_mix.shape[0]
    assert depth == 2, "trunk is one dense layer followed by one MoE layer"
    mix = functools.partial(_mixer, ts=512, sub=256)
    mixer_params = (norm_mix, w_in, conv_w, conv_b, conv_ng, conv_nb, sgu_ng, sgu_nb, sgu_w, sgu_b, w_out)
    x, ffn_wg_b, ffn_wu_b, ffn_wd_b, w_in1_b, w_out1_b, moe_wg_b, moe_wu_b = mix(
        x, 0, *mixer_params,
        cast=((ffn_wg, 0), (ffn_wu, 0), (ffn_wd, 0), (w_in, 1), (w_out, 1), (moe_wg, 0), (moe_wu, 0)))
    x, moe_wd_b = mix(x, 1, *mixer_params, proj=(w_in1_b, w_out1_b),
                      ffn=(0, norm_ffn, ffn_wg_b, ffn_wu_b, ffn_wd_b), cast=((moe_wd, 0),))
    y2 = _moe(x.reshape(B * S, D), norm_ffn[1], moe_router[0], moe_wg_b.reshape(moe_wg[0].shape),
              moe_wu_b.reshape(moe_wu[0].shape), moe_wd_b.reshape(moe_wd[0].shape), norm_final)
    return y2.reshape(B, S, D)
```

```python
import functools

import jax
import jax.numpy as jnp
from jax import lax
from jax.experimental import pallas as pl
from jax.experimental.pallas import tpu as pltpu

F32 = jnp.float32
BF16 = jnp.bfloat16

EPS = 1e-6
CONV_WIDTH = 31
CONV_GROUPS = 8
SGU_HEADS = 4
CHUNK = 128
TOP_K = 2

SUBLANES = 8
LANES = 128
BF16_ROWS = 16
CONV_HALO = 32
VMEM_LIMIT_BYTES = 58 * 1024 * 1024
_MIX_STAGES = 3
_MIX_PIECES = SUBLANES

MOE_WINDOW = 512
MOE_TILE = 512


def _rms_norm(x, g):
    ms = jnp.mean(x * x, axis=-1, keepdims=True)
    return x * lax.rsqrt(ms + EPS) * g


def _gelu(x):
    return 0.5 * x * (1.0 + lax.erf(x * (2.0 ** -0.5)))


def _sigmoid(x):
    return 0.5 * jnp.tanh(0.5 * x) + 0.5


def _silu(x):
    h = 0.5 * x
    return h * jnp.tanh(h) + h


def _cast_plan(rows, nsteps):
    nblk = nsteps
    while rows % nblk or (rows // nblk) % BF16_ROWS:
        nblk //= 2
        assert nblk >= 1, rows
    return nblk, rows // nblk


def _cast_specs(items, nsteps, step_of):
    arrs, in_specs, out_specs, shapes = [], [], [], []
    for a, layer in items:
        a = a.reshape(a.shape[0], -1, a.shape[-1])
        nblk, blk = _cast_plan(a.shape[1], nsteps)
        blk_of = lambda *g, nblk=nblk: jnp.minimum(step_of(*g), nblk - 1)
        arrs.append(a)
        in_specs.append(pl.BlockSpec((None, blk, a.shape[2]),
                                     lambda *g, layer=layer, blk_of=blk_of: (layer, blk_of(*g), 0)))
        out_specs.append(pl.BlockSpec((blk, a.shape[2]), lambda *g, blk_of=blk_of: (blk_of(*g), 0)))
        shapes.append(jax.ShapeDtypeStruct(a.shape[1:], BF16))
    return arrs, in_specs, out_specs, shapes


def _cast_blocks(src_refs, dst_refs):
    for src, dst in zip(src_refs, dst_refs):
        dst[...] = src[...].astype(BF16)


_MIXER_INPUTS = 13
_FFN_INPUTS = 5
_FFN_COLS = 256


def _mixer_kernel(*refs, layer, ts, sub, cc, sc, n_cast, ffn_layer):
    (x_ref, nw_all, win_in_ref, cw_ref, cb_all, cng_all, cnb_all, sng_all, snb_all,
     sw_ref, sb_ref, wout_in_ref, gavg_ref) = refs[:_MIXER_INPUTS]
    nw_ref, cb_ref, cng_ref, cnb_ref, sng_ref, snb_ref = (
        r.at[layer:layer + 1] for r in (nw_all, cb_all, cng_all, cnb_all, sng_all, snb_all))
    n_in = _MIXER_INPUTS
    if ffn_layer is not None:
        xn_ref, nwf_all, wg_ref, wu_ref, wd_ref = refs[n_in:n_in + _FFN_INPUTS]
        nwf_ref = nwf_all.at[ffn_layer:ffn_layer + 1]
        n_in += _FFN_INPUTS
    cast_in = refs[n_in:n_in + n_cast]
    o_ref = refs[n_in + n_cast]
    cast_out = refs[n_in + n_cast + 1:n_in + 2 * n_cast + 1]
    scratch = refs[n_in + 2 * n_cast + 1:]
    cbuf_ref = scratch[0]
    s = pl.program_id(1)
    _cast_blocks(cast_in, cast_out)

    if win_in_ref.dtype == BF16:
        win_ref, wout_ref = win_in_ref, wout_in_ref
    else:
        win_ref, wout_ref = scratch[1:3]

        @pl.when(jnp.logical_and(pl.program_id(0) == 0, s == 0))
        def _():
            win_ref[...] = win_in_ref[...].astype(BF16)
            wout_ref[...] = wout_in_ref[...].astype(BF16)

    @pl.when(s == 0)
    def _():
        cbuf_ref[0:CONV_HALO, :] = jnp.zeros((CONV_HALO, cc), F32)

    row = lax.broadcasted_iota(jnp.int32, (CHUNK, CHUNK), 0)
    col = lax.broadcasted_iota(jnp.int32, (CHUNK, CHUNK), 1)
    tril = row >= col
    hd = sc // SGU_HEADS
    nq = -(-CONV_WIDTH // SUBLANES)

    nsub = ts // sub
    d_in = 2 * cc + 2 * sc
    cbw = d_in // _MIX_PIECES

    def ffn_stage(x):
        hf = _rms_norm(x, nwf_ref[...]).astype(BF16)
        ff = wg_ref.shape[-1]
        fcols = [(c0, min(ff, c0 + _FFN_COLS)) for c0 in range(0, ff, _FFN_COLS)]
        per = -(-len(fcols) // 3)
        gu = []
        for w_ref in (wg_ref, wu_ref):
            blocks = []
            for i, (c0, c1) in enumerate(fcols):
                blocks.append(jnp.dot(hf, w_ref[:, c0:c1], preferred_element_type=F32))
                if i % per == per - 1 or i == len(fcols) - 1:
                    yield
            gu.append(jnp.concatenate(blocks, axis=1))
        act = (_silu(gu[0]) * gu[1]).astype(BF16)
        dm = wd_ref.shape[-1]
        yblk = []
        for i in range(4):
            c0, c1 = i * dm // 4, (i + 1) * dm // 4
            yblk.append(jnp.dot(act, wd_ref[:, c0:c1], preferred_element_type=F32))
            if i % 2 == 1 and i < 3:
                yield
        return x + jnp.concatenate(yblk, axis=1)

    if ffn_layer is not None:
        xcarry_ref = scratch[-1]

        @pl.when(jnp.logical_and(pl.program_id(0) == 0, s == 0))
        def _():
            stage = ffn_stage(x_ref[0, 0:sub, :])
            try:
                while True:
                    next(stage)
            except StopIteration as done:
                xcarry_ref[...] = done.value

    def next_first_sub_tile():
        xcarry_ref[...] = yield from ffn_stage(xn_ref[0])
        yield

    def sub_tile(j):
        r0 = j * sub
        if ffn_layer is None:
            x = x_ref[0, r0:r0 + sub, :]
        elif j == 0:
            x = xcarry_ref[...]
        else:
            x = yield from ffn_stage(x_ref[0, r0:r0 + sub, :])
            yield
        h = _rms_norm(x, nw_ref[...]).astype(BF16)
        zblk = []
        for k in range(_MIX_PIECES):
            zblk.append(jnp.dot(h, win_ref[:, k * cbw:(k + 1) * cbw], preferred_element_type=F32))
            yield
        z = jnp.concatenate(zblk, axis=1)
        a = z[:, :cc]
        gate = z[:, cc:2 * cc]
        u = z[:, 2 * cc:2 * cc + sc]
        v = z[:, 2 * cc + sc:]

        base = CONV_HALO + r0
        cbuf_ref[base:base + sub, :] = a * _sigmoid(gate)
        acc = jnp.broadcast_to(cb_ref[...], (sub, cc))
        for b in range(SUBLANES):
            zb = None
            for q in range(nq):
                d = SUBLANES * q + b
                if d >= CONV_WIDTH:
                    continue
                start = base - SUBLANES * (q + 1)
                term = (cw_ref[CONV_WIDTH - 1 - d]
                        * cbuf_ref[start:start + sub + SUBLANES, :])
                zb = term if zb is None else zb + term
            acc = acc + zb[SUBLANES - b:SUBLANES - b + sub, :]
            yield

        gavg = gavg_ref[...]
        acc_hi = acc.astype(BF16)
        acc_lo = (acc - acc_hi.astype(F32)).astype(BF16)
        mu = (jnp.dot(acc_hi, gavg, preferred_element_type=F32)
              + jnp.dot(acc_lo, gavg, preferred_element_type=F32))
        yield
        dev = acc - mu
        var = jnp.dot((dev * dev).astype(BF16), gavg, preferred_element_type=F32)
        yield
        cn = dev * lax.rsqrt(var + EPS) * cng_ref[...] + cnb_ref[...]
        c_out = _silu(cn).astype(BF16)
        yield

        u = _gelu(u)
        v = _gelu(v)
        yield
        g_cols = []
        for hh in range(SGU_HEADS):
            vh = v[:, hh * hd:(hh + 1) * hd]
            mu_h = jnp.mean(vh, axis=-1, keepdims=True)
            dh = vh - mu_h
            var_h = jnp.mean(dh * dh, axis=-1, keepdims=True)
            vn = (dh * lax.rsqrt(var_h + EPS) * sng_ref[:, hh * hd:(hh + 1) * hd]
                  + snb_ref[:, hh * hd:(hh + 1) * hd]).astype(BF16)
            ws = jnp.where(tril, sw_ref[hh], 0.0).astype(BF16)
            rows = []
            for ci in range(sub // CHUNK):
                sp = jnp.dot(ws, vn[ci * CHUNK:(ci + 1) * CHUNK, :], preferred_element_type=F32)
                rows.append(sp + sb_ref[hh])
            sp_h = rows[0] if len(rows) == 1 else jnp.concatenate(rows, axis=0)
            g_cols.append((u[:, hh * hd:(hh + 1) * hd] * sp_h).astype(BF16))
            if hh % 2 == 1:
                yield
        g_out = jnp.concatenate(g_cols, axis=1)

        yc = jnp.dot(c_out, wout_ref[0:cc, :], preferred_element_type=F32)
        yield
        yg = jnp.dot(g_out, wout_ref[cc:cc + sc, :], preferred_element_type=F32)
        o_ref[0, r0:r0 + sub, :] = x + (yc + yg)
        yield

    P = _MIX_PIECES
    if ffn_layer is None:
        plan = [(sub_tile(j), P * j, P * _MIX_STAGES) for j in range(nsub)]
    else:
        plan = [(sub_tile(0), 0, P * _MIX_STAGES)]
        plan += [(sub_tile(j), P * (2 * j - 1), P * (_MIX_STAGES + 1)) for j in range(1, nsub)]
        plan += [(next_first_sub_tile(), P * (2 * nsub - 1), P)]
    for tick in range(max(t0 + n for _, t0, n in plan)):
        for gen, t0, n in reversed(plan):
            if 0 <= tick - t0 < n:
                next(gen)

    cbuf_ref[0:CONV_HALO, :] = cbuf_ref[ts:ts + CONV_HALO, :]


def _mixer(x, layer, norm_mix, w_in, conv_w, conv_b, conv_ng, conv_nb, sgu_ng, sgu_nb, sgu_w, sgu_b,
           w_out, *, ts, sub, cast=(), proj=None, ffn=None):
    B, S, D = x.shape
    L = conv_w.shape[0]
    cc = conv_w.shape[-1]
    sc = sgu_ng.shape[-1]
    d_in = w_in.shape[-1]
    hd = sc // SGU_HEADS
    assert S % ts == 0 and ts % sub == 0 and sub % CHUNK == 0 and d_in == 2 * cc + 2 * sc
    ns = S // ts
    gs = cc // CONV_GROUPS
    gid = jnp.arange(cc) // gs
    gavg = jnp.where(gid[:, None] == gid[None, :], 1.0 / gs, 0.0).astype(BF16)
    sb_full = jnp.broadcast_to(sgu_b[:, :, :, None], (L, SGU_HEADS, CHUNK, hd))
    whole = lambda p: pl.BlockSpec(p.shape, lambda b, s: (0,) * p.ndim)
    layer_block = lambda *shape: pl.BlockSpec((None,) + shape, lambda b, s: (layer,) + (0,) * len(shape))
    cast_arrs, cast_in_specs, cast_out_specs, cast_shapes = _cast_specs(
        cast, B * ns, lambda b, s: b * ns + s)
    kern = functools.partial(_mixer_kernel, layer=layer, ts=ts, sub=sub, cc=cc, sc=sc, n_cast=len(cast),
                             ffn_layer=None if ffn is None else ffn[0])
    scratch = [pltpu.VMEM((CONV_HALO + ts, cc), F32)]
    if proj is None:
        w_in_spec, w_out_spec = layer_block(D, d_in), layer_block(cc + sc, D)
        scratch += [pltpu.VMEM((D, d_in), BF16), pltpu.VMEM((cc + sc, D), BF16)]
    else:
        w_in, w_out = proj
        assert w_in.dtype == BF16 and w_out.dtype == BF16
        w_in_spec, w_out_spec = whole(w_in), whole(w_out)
    ffn_args, ffn_specs = (), []
    if ffn is not None:
        def next_first(b, s):
            step = jnp.minimum(b * ns + s + 1, B * ns - 1)
            return (step // ns, (step % ns) * (ts // sub), 0)

        ffn_args = (x,) + tuple(ffn[1:])
        ffn_specs = [pl.BlockSpec((1, sub, D), next_first)] + [whole(a) for a in ffn[1:]]
        scratch += [pltpu.VMEM((sub, D), F32)]
    in_specs = [
        pl.BlockSpec((1, ts, D), lambda b, s: (b, s, 0)),
        whole(norm_mix),
        w_in_spec,
        layer_block(CONV_WIDTH, 1, cc),
        whole(conv_b), whole(conv_ng), whole(conv_nb),
        whole(sgu_ng), whole(sgu_nb),
        layer_block(SGU_HEADS, CHUNK, CHUNK),
        layer_block(SGU_HEADS, CHUNK, hd),
        w_out_spec,
        whole(gavg),
    ]
    assert len(in_specs) == _MIXER_INPUTS and len(ffn_args) in (0, _FFN_INPUTS)
    in_specs += ffn_specs
    return pl.pallas_call(
        kern,
        grid=(B, ns),
        in_specs=in_specs + cast_in_specs,
        out_specs=[pl.BlockSpec((1, ts, D), lambda b, s: (b, s, 0))] + cast_out_specs,
        out_shape=[jax.ShapeDtypeStruct((B, S, D), F32)] + cast_shapes,
        scratch_shapes=scratch,
        compiler_params=pltpu.CompilerParams(
            dimension_semantics=("arbitrary", "arbitrary"),
            vmem_limit_bytes=VMEM_LIMIT_BYTES),
        name="mixer",
    )(x, norm_mix, w_in, conv_w, conv_b, conv_ng, conv_nb, sgu_ng, sgu_nb, sgu_w, sb_full, w_out,
      gavg, *ffn_args, *cast_arrs)


def _swiglu_tile(h, wg_ref, wu_ref, wd_ref, ff_chunk):
    ff = wg_ref.shape[-1]
    out = None
    for c0 in range(0, ff, ff_chunk):
        c1 = min(ff, c0 + ff_chunk)
        g = jnp.dot(h, wg_ref[:, c0:c1], preferred_element_type=F32)
        u = jnp.dot(h, wu_ref[:, c0:c1], preferred_element_type=F32)
        a = (_silu(g) * u).astype(BF16)
        y = jnp.dot(a, wd_ref[c0:c1, :], preferred_element_type=F32)
        out = y if out is None else out + y
    return out


def _round_up(n, m):
    return (n + m - 1) // m * m


def _piece_sizes(largest):
    sizes = []
    s = largest
    while s >= BF16_ROWS:
        sizes.append(s)
        s //= 2
    assert sizes and sizes[-1] == BF16_ROWS
    return sizes


def _for_each_piece(count, sizes, fn):
    for s in sizes:
        offset = jnp.bitwise_and(count, -2 * s)

        @pl.when(jnp.bitwise_and(count, s) != 0)
        def _():
            fn(offset, s)


def _route_kernel(x_ref, nw_ref, rt_ref, hn_ref, dcol_ref, drow_ref, cnt_ref, *, win, n_experts):
    for k in range(x_ref.shape[0] // win):
        rows = slice(k * win, (k + 1) * win)
        _route_window(x_ref.at[rows], nw_ref, rt_ref, hn_ref.at[rows],
                      dcol_ref.at[rows], drow_ref.at[k], cnt_ref.at[k], n_experts)


def _route_window(x_ref, nw_ref, rt_ref, hn_ref, dcol_ref, drow_ref, cnt_ref, n_experts):
    x = x_ref[...]
    h = _rms_norm(x, nw_ref[...]).astype(BF16)
    hn_ref[...] = h
    logits = jnp.dot(h, rt_ref[...], preferred_element_type=F32)
    lane = lax.broadcasted_iota(jnp.int32, logits.shape, 1)
    lt = jnp.transpose(jnp.where(lane < n_experts, logits, -jnp.inf))[0:SUBLANES, :]
    win = lt.shape[1]
    sub = lax.broadcasted_iota(jnp.int32, lt.shape, 0)
    m1 = jnp.max(lt, axis=0, keepdims=True)
    i1 = jnp.min(jnp.where(lt == m1, sub, SUBLANES), axis=0, keepdims=True)
    rest = jnp.where(sub == i1, -jnp.inf, lt)
    m2 = jnp.max(rest, axis=0, keepdims=True)
    i2 = jnp.min(jnp.where(rest == m2, sub, SUBLANES), axis=0, keepdims=True)
    e2 = jnp.exp(m2 - m1)
    den = 1.0 + e2
    g1 = 1.0 / den
    g2 = e2 / den
    sel1 = sub == i1
    sel2 = sub == i2
    self = jnp.logical_or(sel1, sel2).astype(F32)
    tok = lax.broadcasted_iota(jnp.int32, lt.shape, 1)
    inc = self
    step = 1
    while step < win:
        inc = inc + jnp.where(tok >= step, pltpu.roll(inc, step, axis=1), 0.0)
        step *= 2
    rank = inc - self
    cnt = jnp.sum(self, axis=1, keepdims=True)
    cnt_pad = jnp.floor((cnt + (BF16_ROWS - 1.0)) * (1.0 / BF16_ROWS)) * BF16_ROWS
    seg = jnp.zeros_like(cnt_pad)
    for e in range(n_experts - 1):
        seg = seg + jnp.where(sub[:, 0:1] > e, cnt_pad[e:e + 1, :], 0.0)
    pos = seg + rank
    dest1 = jnp.sum(jnp.where(sel1, pos, 0.0), axis=0, keepdims=True)
    dest2 = jnp.sum(jnp.where(sel2, pos, 0.0), axis=0, keepdims=True)
    info = jnp.where(sub == 0, dest1, jnp.where(sub == 1, dest2,
                     jnp.where(sub == 2, g1, jnp.where(sub == 3, g2, -1.0))))
    drow_ref[...] = info
    cols = jnp.concatenate([info, jnp.broadcast_to(cnt, (SUBLANES, win)),
                            jnp.full((LANES - 2 * SUBLANES, win), -1.0, F32)], axis=0)
    cols = jnp.transpose(cols)
    dcol_ref[...] = cols
    cnt_ref[...] = cols[0:1, :].astype(jnp.int32)


def _route(x2, nw, router, *, win, per_step=4):
    T, D = x2.shape
    E = router.shape[-1]
    assert T % (win * per_step) == 0 and E <= SUBLANES
    nwin = T // win
    blk = win * per_step
    rt = jnp.zeros((D, LANES), F32).at[:, :E].set(router).astype(BF16)
    kern = functools.partial(_route_kernel, win=win, n_experts=E)
    return pl.pallas_call(
        kern,
        grid=(nwin // per_step,),
        in_specs=[
            pl.BlockSpec((blk, D), lambda w: (w, 0)),
            pl.BlockSpec((1, D), lambda w: (0, 0)),
            pl.BlockSpec((D, LANES), lambda w: (0, 0)),
        ],
        out_specs=[
            pl.BlockSpec((blk, D), lambda w: (w, 0)),
            pl.BlockSpec((blk, LANES), lambda w: (w, 0)),
            pl.BlockSpec((per_step, SUBLANES, win), lambda w: (w, 0, 0)),
            pl.BlockSpec((per_step, 1, LANES), lambda w: (w, 0, 0)),
        ],
        out_shape=[
            jax.ShapeDtypeStruct((T, D), BF16),
            jax.ShapeDtypeStruct((T, LANES), F32),
            jax.ShapeDtypeStruct((nwin, SUBLANES, win), F32),
            jax.ShapeDtypeStruct((nwin, 1, LANES), jnp.int32),
        ],
        compiler_params=pltpu.CompilerParams(
            dimension_semantics=("arbitrary",), vmem_limit_bytes=VMEM_LIMIT_BYTES),
        name="moe_route",
    )(x2, nw.reshape(1, D).astype(F32), rt)


def _compact_kernel(seg_ref, off_ref, cnt_ref, fill_ref, hn_ref, drow_ref, xs_ref,
                    stage_ref, zero_ref, sem, *, win, n_experts, rows, tm):
    w = pl.program_id(0)
    nwin = pl.num_programs(0)
    slot = w % 2
    d = hn_ref.shape[-1]
    sizes = _piece_sizes(win)

    def segments(ww, sl, start):
        for e in range(n_experts):
            src0 = seg_ref[ww * n_experts + e]
            dst0 = off_ref[ww * n_experts + e]

            def piece(offset, size):
                cp = pltpu.make_async_copy(
                    stage_ref.at[sl, pl.ds(pl.multiple_of(src0 + offset, BF16_ROWS), size)],
                    xs_ref.at[pl.ds(pl.multiple_of(dst0 + offset, BF16_ROWS), size)],
                    sem.at[sl, e])
                cp.start() if start else cp.wait()

            _for_each_piece(cnt_ref[ww * n_experts + e], sizes, piece)

    def zero_rows(dst, size, e, start):
        cp = pltpu.make_async_copy(zero_ref.at[pl.ds(0, size)],
                                   xs_ref.at[pl.ds(pl.multiple_of(dst, BF16_ROWS), size)],
                                   sem.at[2, e])
        cp.start() if start else cp.wait()

    @pl.when(w == 0)
    def _():
        zero_ref[...] = jnp.zeros(zero_ref.shape, BF16)

    d1 = drow_ref[0, 0:1, :]
    d2 = drow_ref[0, 1:2, :]
    g1 = drow_ref[0, 2:3, :]
    g2 = drow_ref[0, 3:4, :]
    r = lax.broadcasted_iota(jnp.int32, (rows, win), 0).astype(F32)
    m1 = r == d1
    m2 = r == d2
    onehot = jnp.logical_or(m1, m2).astype(BF16)
    xg = jnp.dot(onehot, hn_ref[...], preferred_element_type=F32)
    stage_ref[slot, :, 0:d] = xg.astype(BF16)
    gate = jnp.sum(jnp.where(m1, g1, 0.0) + jnp.where(m2, g2, 0.0), axis=-1, keepdims=True)
    hi = gate.astype(BF16).astype(F32)
    mid = (gate - hi).astype(BF16).astype(F32)
    lo = gate - hi - mid
    lane = lax.broadcasted_iota(jnp.int32, (rows, LANES), 1)
    terms = jnp.where(lane == 0, hi, jnp.where(lane == 1, mid, jnp.where(lane == 2, lo, 0.0)))
    stage_ref[slot, :, d:d + LANES] = terms.astype(BF16)

    segments(w, slot, True)

    @pl.when(w > 0)
    def _():
        segments(w - 1, 1 - slot, False)

    @pl.when(w == nwin - 1)
    def _():
        segments(w, slot, False)
        gap_sizes = _piece_sizes(tm // 2)
        for start in (True, False):
            for e in range(n_experts):
                _for_each_piece(fill_ref[n_experts + e], gap_sizes,
                                lambda offset, size, e=e: zero_rows(fill_ref[e] + offset, size, e, start))
        tail0 = fill_ref[2 * n_experts]
        ntail = fill_ref[2 * n_experts + 1]
        tail_rows = zero_ref.shape[0]

        def tail_start(j, c):
            zero_rows(tail0 + j * tail_rows, tail_rows, 0, True)
            return c

        def tail_wait(j, c):
            zero_rows(tail0 + j * tail_rows, tail_rows, 0, False)
            return c

        lax.fori_loop(0, ntail, tail_start, 0)
        lax.fori_loop(0, ntail, tail_wait, 0)


def _compact(seg, off, cnt_pad, fill, hn, drow, *, win, n_experts, cap, tm):
    T, D = hn.shape
    nwin = T // win
    rows = _round_up(TOP_K * win + n_experts * (BF16_ROWS - 1), BF16_ROWS)
    kern = functools.partial(_compact_kernel, win=win, n_experts=n_experts, rows=rows, tm=tm)
    return pl.pallas_call(
        kern,
        grid_spec=pltpu.PrefetchScalarGridSpec(
            num_scalar_prefetch=4,
            grid=(nwin,),
            in_specs=[
                pl.BlockSpec((win, D), lambda w, *_: (w, 0)),
                pl.BlockSpec((1, SUBLANES, win), lambda w, *_: (w, 0, 0)),
            ],
            out_specs=pl.BlockSpec(memory_space=pl.ANY),
            scratch_shapes=[pltpu.VMEM((2, rows, D + LANES), BF16),
                            pltpu.VMEM((tm // 2, D + LANES), BF16),
                            pltpu.SemaphoreType.DMA((3, n_experts))],
        ),
        out_shape=jax.ShapeDtypeStruct((cap, D + LANES), BF16),
        compiler_params=pltpu.CompilerParams(
            dimension_semantics=("arbitrary",), vmem_limit_bytes=VMEM_LIMIT_BYTES),
        name="moe_compact",
    )(seg, off, cnt_pad, fill, hn, drow)


def _expert_kernel(te_ref, tv_ref, xs_ref, wg_ref, wu_ref, wd_ref, ys_ref, *, ff_chunk):
    i = pl.program_id(0)
    tm, d = ys_ref.shape
    half = tm // 2

    def rows(lo, hi):
        y = _swiglu_tile(xs_ref[lo:hi, 0:d], wg_ref.at[0], wu_ref.at[0], wd_ref.at[0], ff_chunk)
        gate = jnp.sum(xs_ref[lo:hi, d:d + LANES].astype(F32), axis=-1, keepdims=True)
        ys_ref[lo:hi, :] = (gate * y).astype(BF16)

    @pl.when(tv_ref[i] == 2)
    def _():
        rows(0, tm)

    @pl.when(tv_ref[i] == 1)
    def _():
        rows(0, half)
        ys_ref[half:tm, :] = jnp.zeros((tm - half, d), BF16)

    @pl.when(tv_ref[i] == 0)
    def _():
        ys_ref[...] = jnp.zeros(ys_ref.shape, BF16)


def _experts(tile_e, tile_v, xs, wg, wu, wd, *, tm, ff_chunk):
    cap = xs.shape[0]
    _, D, FF = wg.shape
    assert cap == tile_e.shape[0] * tm and xs.shape[1] == D + LANES
    kern = functools.partial(_expert_kernel, ff_chunk=ff_chunk)
    return pl.pallas_call(
        kern,
        grid_spec=pltpu.PrefetchScalarGridSpec(
            num_scalar_prefetch=2,
            grid=(cap // tm,),
            in_specs=[
                pl.BlockSpec((tm, D + LANES), lambda i, te, tv: (i, 0)),
                pl.BlockSpec((1, D, FF), lambda i, te, tv: (te[i], 0, 0)),
                pl.BlockSpec((1, D, FF), lambda i, te, tv: (te[i], 0, 0)),
                pl.BlockSpec((1, FF, D), lambda i, te, tv: (te[i], 0, 0)),
            ],
            out_specs=pl.BlockSpec((tm, D), lambda i, te, tv: (i, 0)),
        ),
        out_shape=jax.ShapeDtypeStruct((cap, D), BF16),
        compiler_params=pltpu.CompilerParams(
            dimension_semantics=("arbitrary",), vmem_limit_bytes=VMEM_LIMIT_BYTES),
        name="moe_experts",
    )(tile_e, tile_v, xs, wg.astype(BF16), wu.astype(BF16), wd.astype(BF16))


def _combine_kernel(seg_ref, off_ref, cnt_ref, x_ref, dcol_ref, fw_ref, ys_ref, o_ref,
                    stage_ref, sem, *, win, n_experts):
    w = pl.program_id(0)
    nwin = pl.num_programs(0)
    slot = w % 2
    rows = stage_ref.shape[1]
    sizes = _piece_sizes(win)

    def segments(ww, sl, start):
        for e in range(n_experts):
            src0 = off_ref[ww * n_experts + e]
            dst0 = seg_ref[ww * n_experts + e]

            def piece(offset, size):
                cp = pltpu.make_async_copy(
                    ys_ref.at[pl.ds(pl.multiple_of(src0 + offset, BF16_ROWS), size)],
                    stage_ref.at[sl, pl.ds(pl.multiple_of(dst0 + offset, BF16_ROWS), size)],
                    sem.at[sl, e])
                cp.start() if start else cp.wait()

            _for_each_piece(cnt_ref[ww * n_experts + e], sizes, piece)

    @pl.when(w == 0)
    def _():
        stage_ref[...] = jnp.zeros(stage_ref.shape, BF16)
        segments(0, 0, True)

    @pl.when(w + 1 < nwin)
    def _():
        segments(w + 1, 1 - slot, True)

    segments(w, slot, False)

    d1 = dcol_ref[:, 0:1]
    d2 = dcol_ref[:, 1:2]
    r = lax.broadcasted_iota(jnp.int32, (win, rows), 1).astype(F32)
    onehot = jnp.logical_or(r == d1, r == d2).astype(BF16)
    y = jnp.dot(onehot, stage_ref[slot], preferred_element_type=F32)
    o_ref[...] = _rms_norm(x_ref[...] + y, fw_ref[...])


def _combine(seg, off, cnt_pad, x2, dcol, fw, ys, *, win, n_experts):
    T, D = x2.shape
    rows = _round_up(TOP_K * win + n_experts * (BF16_ROWS - 1), BF16_ROWS)
    kern = functools.partial(_combine_kernel, win=win, n_experts=n_experts)
    return pl.pallas_call(
        kern,
        grid_spec=pltpu.PrefetchScalarGridSpec(
            num_scalar_prefetch=3,
            grid=(T // win,),
            in_specs=[
                pl.BlockSpec((win, D), lambda w, *_: (w, 0)),
                pl.BlockSpec((win, LANES), lambda w, *_: (w, 0)),
                pl.BlockSpec((1, D), lambda w, *_: (0, 0)),
                pl.BlockSpec(memory_space=pl.ANY),
            ],
            out_specs=pl.BlockSpec((win, D), lambda w, *_: (w, 0)),
            scratch_shapes=[pltpu.VMEM((2, rows, D), BF16),
                            pltpu.SemaphoreType.DMA((2, n_experts))],
        ),
        out_shape=jax.ShapeDtypeStruct((T, D), F32),
        compiler_params=pltpu.CompilerParams(
            dimension_semantics=("arbitrary",), vmem_limit_bytes=VMEM_LIMIT_BYTES),
        name="moe_combine",
    )(seg, off, cnt_pad, x2, dcol, fw.reshape(1, D).astype(F32), ys)


def _moe(x2, nw, router, wg, wu, wd, fw, *, win=MOE_WINDOW, tm=MOE_TILE):
    T, D = x2.shape
    E = router.shape[-1]
    nwin = T // win
    hn, dcol, drow, cnt = _route(x2, nw, router, win=win)

    cnt = cnt.reshape(nwin, LANES)[:, SUBLANES:SUBLANES + E]
    cnt_pad = _round_up(cnt, BF16_ROWS)
    seg = jnp.cumsum(cnt_pad, axis=1) - cnt_pad
    total = jnp.sum(cnt_pad, axis=0)
    region = _round_up(total, tm)
    r_end = jnp.cumsum(region)
    r_start = r_end - region
    off = r_start[None, :] + jnp.cumsum(cnt_pad, axis=0) - cnt_pad
    max_rows = TOP_K * T + nwin * E * (BF16_ROWS - 1)
    ntiles_max = max_rows // tm + E
    cap = ntiles_max * tm
    ti = jnp.arange(ntiles_max)
    t_end = r_end // tm
    tile_e = jnp.minimum(jnp.sum(ti[:, None] >= t_end[None, :], axis=1), E - 1).astype(jnp.int32)
    live = jnp.clip(total[tile_e] - (ti - (r_start // tm)[tile_e]) * tm, 0, tm)
    tile_v = jnp.where(ti < t_end[-1], 1 + (live > tm // 2), 0).astype(jnp.int32)
    tail = jnp.stack([r_end[-1], (cap - r_end[-1]) // (tm // 2)])
    fill = jnp.concatenate([r_start + total, region - total, tail])
    flat = lambda a: a.reshape(-1).astype(jnp.int32)
    seg, off, cnt_pad, fill = flat(seg), flat(off), flat(cnt_pad), flat(fill)

    xs = _compact(seg, off, cnt_pad, fill, hn, drow, win=win, n_experts=E, cap=cap, tm=tm)
    ys = _experts(tile_e, tile_v, xs, wg, wu, wd, tm=tm, ff_chunk=wg.shape[-1])
    return _combine(seg, off, cnt_pad, x2, dcol, fw, ys, win=win, n_experts=E)


def kernel(x, norm_mix, w_in, conv_w, conv_b, conv_ng, conv_nb, sgu_ng, sgu_nb, sgu_w, sgu_b, w_out,
           norm_ffn, ffn_wg, ffn_wu, ffn_wd, moe_router, moe_wg, moe_wu, moe_wd, norm_final):
    B, S, D = x.shape
    depth = norm_mix.shape[0]
    assert depth == 2, "trunk is one dense layer followed by one MoE layer"
    mix = functools.partial(_mixer, ts=512, sub=256)
    mixer_params = (norm_mix, w_in, conv_w, conv_b, conv_ng, conv_nb, sgu_ng, sgu_nb, sgu_w, sgu_b, w_out)
    x, ffn_wg_b, ffn_wu_b, ffn_wd_b, w_in1_b, w_out1_b, moe_wg_b, moe_wu_b = mix(
        x, 0, *mixer_params,
        cast=((ffn_wg, 0), (ffn_wu, 0), (ffn_wd, 0), (w_in, 1), (w_out, 1), (moe_wg, 0), (moe_wu, 0)))
    x, moe_wd_b = mix(x, 1, *mixer_params, proj=(w_in1_b, w_out1_b),
                      ffn=(0, norm_ffn, ffn_wg_b, ffn_wu_b, ffn_wd_b), cast=((moe_wd, 0),))
    y2 = _moe(x.reshape(B * S, D), norm_ffn[1], moe_router[0], moe_wg_b.reshape(moe_wg[0].shape),
              moe_wu_b.reshape(moe_wu[0].shape), moe_wd_b.reshape(moe_wd[0].shape), norm_final)
    return y2.reshape(B, S, D)
```

```python
import functools

import jax
import jax.numpy as jnp
from jax import lax
from jax.experimental import pallas as pl
from jax.experimental.pallas import tpu as pltpu

F32 = jnp.float32
BF16 = jnp.bfloat16

EPS = 1e-6
CONV_WIDTH = 31
CONV_GROUPS = 8
SGU_HEADS = 4
CHUNK = 128
TOP_K = 2

SUBLANES = 8
LANES = 128
BF16_ROWS = 16
CONV_HALO = 32
VMEM_LIMIT_BYTES = 58 * 1024 * 1024
MIXER_TILE = 512
MIXER_SUB_TILE = 256
_MIX_STAGES = 3
_MIX_PIECES = SUBLANES

MOE_WINDOW = 512
MOE_TILE = 512


def _rms_norm(x, g):
    ms = jnp.mean(x * x, axis=-1, keepdims=True)
    return x * lax.rsqrt(ms + EPS) * g


def _gelu(x):
    return 0.5 * x * (1.0 + lax.erf(x * (2.0 ** -0.5)))


def _sigmoid(x):
    return 0.5 * jnp.tanh(0.5 * x) + 0.5


def _silu(x):
    h = 0.5 * x
    return h * jnp.tanh(h) + h


def _cast_plan(rows, nsteps):
    nblk = nsteps
    while rows % nblk or (rows // nblk) % BF16_ROWS:
        nblk //= 2
        assert nblk >= 1, rows
    return nblk, rows // nblk


def _cast_specs(items, nsteps, step_of):
    arrs, in_specs, out_specs, shapes = [], [], [], []
    for a, layer in items:
        a = a.reshape(a.shape[0], -1, a.shape[-1])
        nblk, blk = _cast_plan(a.shape[1], nsteps)
        blk_of = lambda *g, nblk=nblk: jnp.minimum(step_of(*g), nblk - 1)
        arrs.append(a)
        in_specs.append(pl.BlockSpec((None, blk, a.shape[2]),
                                     lambda *g, layer=layer, blk_of=blk_of: (layer, blk_of(*g), 0)))
        out_specs.append(pl.BlockSpec((blk, a.shape[2]), lambda *g, blk_of=blk_of: (blk_of(*g), 0)))
        shapes.append(jax.ShapeDtypeStruct(a.shape[1:], BF16))
    return arrs, in_specs, out_specs, shapes


def _cast_blocks(src_refs, dst_refs):
    for src, dst in zip(src_refs, dst_refs):
        dst[...] = src[...].astype(BF16)


_MIXER_INPUTS = 13
_FFN_INPUTS = 5
_FFN_COLS = 256


def _mixer_kernel(*refs, layer, ts, sub, cc, sc, n_cast, ffn_layer):
    (x_ref, nw_all, win_in_ref, cw_ref, cb_all, cng_all, cnb_all, sng_all, snb_all,
     sw_ref, sb_ref, wout_in_ref, gavg_ref) = refs[:_MIXER_INPUTS]
    nw_ref, cb_ref, cng_ref, cnb_ref, sng_ref, snb_ref = (
        r.at[layer:layer + 1] for r in (nw_all, cb_all, cng_all, cnb_all, sng_all, snb_all))
    n_in = _MIXER_INPUTS
    if ffn_layer is not None:
        xn_ref, nwf_all, wg_ref, wu_ref, wd_ref = refs[n_in:n_in + _FFN_INPUTS]
        nwf_ref = nwf_all.at[ffn_layer:ffn_layer + 1]
        n_in += _FFN_INPUTS
    cast_in = refs[n_in:n_in + n_cast]
    o_ref = refs[n_in + n_cast]
    cast_out = refs[n_in + n_cast + 1:n_in + 2 * n_cast + 1]
    scratch = refs[n_in + 2 * n_cast + 1:]
    cbuf_ref = scratch[0]
    s = pl.program_id(1)
    _cast_blocks(cast_in, cast_out)

    if win_in_ref.dtype == BF16:
        win_ref, wout_ref = win_in_ref, wout_in_ref
    else:
        win_ref, wout_ref = scratch[1:3]

        @pl.when(jnp.logical_and(pl.program_id(0) == 0, s == 0))
        def _():
            win_ref[...] = win_in_ref[...].astype(BF16)
            wout_ref[...] = wout_in_ref[...].astype(BF16)

    @pl.when(s == 0)
    def _():
        cbuf_ref[0:CONV_HALO, :] = jnp.zeros((CONV_HALO, cc), F32)

    row = lax.broadcasted_iota(jnp.int32, (CHUNK, CHUNK), 0)
    col = lax.broadcasted_iota(jnp.int32, (CHUNK, CHUNK), 1)
    tril = row >= col
    hd = sc // SGU_HEADS
    nq = -(-CONV_WIDTH // SUBLANES)

    nsub = ts // sub
    d_in = 2 * cc + 2 * sc
    cbw = d_in // _MIX_PIECES

    def ffn_stage(x):
        hf = _rms_norm(x, nwf_ref[...]).astype(BF16)
        ff = wg_ref.shape[-1]
        fcols = [(c0, min(ff, c0 + _FFN_COLS)) for c0 in range(0, ff, _FFN_COLS)]
        per = -(-len(fcols) // 3)
        gu = []
        for w_ref in (wg_ref, wu_ref):
            blocks = []
            for i, (c0, c1) in enumerate(fcols):
                blocks.append(jnp.dot(hf, w_ref[:, c0:c1], preferred_element_type=F32))
                if i % per == per - 1 or i == len(fcols) - 1:
                    yield
            gu.append(jnp.concatenate(blocks, axis=1))
        act = (_silu(gu[0]) * gu[1]).astype(BF16)
        dm = wd_ref.shape[-1]
        yblk = []
        for i in range(4):
            c0, c1 = i * dm // 4, (i + 1) * dm // 4
            yblk.append(jnp.dot(act, wd_ref[:, c0:c1], preferred_element_type=F32))
            if i % 2 == 1 and i < 3:
                yield
        return x + jnp.concatenate(yblk, axis=1)

    if ffn_layer is not None:
        xcarry_ref = scratch[-1]

        @pl.when(jnp.logical_and(pl.program_id(0) == 0, s == 0))
        def _():
            stage = ffn_stage(x_ref[0, 0:sub, :])
            try:
                while True:
                    next(stage)
            except StopIteration as done:
                xcarry_ref[...] = done.value

    def next_first_sub_tile():
        xcarry_ref[...] = yield from ffn_stage(xn_ref[0])
        yield

    def sub_tile(j):
        r0 = j * sub
        if ffn_layer is None:
            x = x_ref[0, r0:r0 + sub, :]
        elif j == 0:
            x = xcarry_ref[...]
        else:
            x = yield from ffn_stage(x_ref[0, r0:r0 + sub, :])
            yield
        h = _rms_norm(x, nw_ref[...]).astype(BF16)
        zblk = []
        for k in range(_MIX_PIECES):
            zblk.append(jnp.dot(h, win_ref[:, k * cbw:(k + 1) * cbw], preferred_element_type=F32))
            yield
        z = jnp.concatenate(zblk, axis=1)
        a = z[:, :cc]
        gate = z[:, cc:2 * cc]
        u = z[:, 2 * cc:2 * cc + sc]
        v = z[:, 2 * cc + sc:]

        base = CONV_HALO + r0
        cbuf_ref[base:base + sub, :] = a * _sigmoid(gate)
        acc = jnp.broadcast_to(cb_ref[...], (sub, cc))
        for b in range(SUBLANES):
            zb = None
            for q in range(nq):
                d = SUBLANES * q + b
                if d >= CONV_WIDTH:
                    continue
                start = base - SUBLANES * (q + 1)
                term = (cw_ref[CONV_WIDTH - 1 - d]
                        * cbuf_ref[start:start + sub + SUBLANES, :])
                zb = term if zb is None else zb + term
            acc = acc + zb[SUBLANES - b:SUBLANES - b + sub, :]
            yield

        gavg = gavg_ref[...]
        acc_hi = acc.astype(BF16)
        acc_lo = (acc - acc_hi.astype(F32)).astype(BF16)
        mu = (jnp.dot(acc_hi, gavg, preferred_element_type=F32)
              + jnp.dot(acc_lo, gavg, preferred_element_type=F32))
        yield
        dev = acc - mu
        var = jnp.dot((dev * dev).astype(BF16), gavg, preferred_element_type=F32)
        yield
        cn = dev * lax.rsqrt(var + EPS) * cng_ref[...] + cnb_ref[...]
        c_out = _silu(cn).astype(BF16)
        yield

        u = _gelu(u)
        v = _gelu(v)
        yield
        g_cols = []
        for hh in range(SGU_HEADS):
            vh = v[:, hh * hd:(hh + 1) * hd]
            mu_h = jnp.mean(vh, axis=-1, keepdims=True)
            dh = vh - mu_h
            var_h = jnp.mean(dh * dh, axis=-1, keepdims=True)
            vn = (dh * lax.rsqrt(var_h + EPS) * sng_ref[:, hh * hd:(hh + 1) * hd]
                  + snb_ref[:, hh * hd:(hh + 1) * hd]).astype(BF16)
            ws = jnp.where(tril, sw_ref[hh], 0.0).astype(BF16)
            rows = []
            for ci in range(sub // CHUNK):
                sp = jnp.dot(ws, vn[ci * CHUNK:(ci + 1) * CHUNK, :], preferred_element_type=F32)
                rows.append(sp + sb_ref[hh])
            sp_h = rows[0] if len(rows) == 1 else jnp.concatenate(rows, axis=0)
            g_cols.append((u[:, hh * hd:(hh + 1) * hd] * sp_h).astype(BF16))
            if hh % 2 == 1:
                yield
        g_out = jnp.concatenate(g_cols, axis=1)

        yc = jnp.dot(c_out, wout_ref[0:cc, :], preferred_element_type=F32)
        yield
        yg = jnp.dot(g_out, wout_ref[cc:cc + sc, :], preferred_element_type=F32)
        o_ref[0, r0:r0 + sub, :] = x + (yc + yg)
        yield

    P = _MIX_PIECES
    if ffn_layer is None:
        plan = [(sub_tile(j), P * j, P * _MIX_STAGES) for j in range(nsub)]
    else:
        plan = [(sub_tile(0), 0, P * _MIX_STAGES)]
        plan += [(sub_tile(j), P * (2 * j - 1), P * (_MIX_STAGES + 1)) for j in range(1, nsub)]
        plan += [(next_first_sub_tile(), P * (2 * nsub - 1), P)]
    for tick in range(max(t0 + n for _, t0, n in plan)):
        for gen, t0, n in reversed(plan):
            if 0 <= tick - t0 < n:
                next(gen)

    cbuf_ref[0:CONV_HALO, :] = cbuf_ref[ts:ts + CONV_HALO, :]


def _mixer(x, layer, norm_mix, w_in, conv_w, conv_b, conv_ng, conv_nb, sgu_ng, sgu_nb, sgu_w, sgu_b,
           w_out, *, ts, sub, cast=(), proj=None, ffn=None):
    B, S, D = x.shape
    L = conv_w.shape[0]
    cc = conv_w.shape[-1]
    sc = sgu_ng.shape[-1]
    d_in = w_in.shape[-1]
    hd = sc // SGU_HEADS
    assert S % ts == 0 and ts % sub == 0 and sub % CHUNK == 0 and d_in == 2 * cc + 2 * sc
    ns = S // ts
    gs = cc // CONV_GROUPS
    gid = jnp.arange(cc) // gs
    gavg = jnp.where(gid[:, None] == gid[None, :], 1.0 / gs, 0.0).astype(BF16)
    sb_full = jnp.broadcast_to(sgu_b[:, :, :, None], (L, SGU_HEADS, CHUNK, hd))
    whole = lambda p: pl.BlockSpec(p.shape, lambda b, s: (0,) * p.ndim)
    layer_block = lambda *shape: pl.BlockSpec((None,) + shape, lambda b, s: (layer,) + (0,) * len(shape))
    cast_arrs, cast_in_specs, cast_out_specs, cast_shapes = _cast_specs(
        cast, B * ns, lambda b, s: b * ns + s)
    kern = functools.partial(_mixer_kernel, layer=layer, ts=ts, sub=sub, cc=cc, sc=sc, n_cast=len(cast),
                             ffn_layer=None if ffn is None else ffn[0])
    scratch = [pltpu.VMEM((CONV_HALO + ts, cc), F32)]
    if proj is None:
        w_in_spec, w_out_spec = layer_block(D, d_in), layer_block(cc + sc, D)
        scratch += [pltpu.VMEM((D, d_in), BF16), pltpu.VMEM((cc + sc, D), BF16)]
    else:
        w_in, w_out = proj
        assert w_in.dtype == BF16 and w_out.dtype == BF16
        w_in_spec, w_out_spec = whole(w_in), whole(w_out)
    ffn_args, ffn_specs = (), []
    if ffn is not None:
        def next_first(b, s):
            step = jnp.minimum(b * ns + s + 1, B * ns - 1)
            return (step // ns, (step % ns) * (ts // sub), 0)

        ffn_args = (x,) + tuple(ffn[1:])
        ffn_specs = [pl.BlockSpec((1, sub, D), next_first)] + [whole(a) for a in ffn[1:]]
        scratch += [pltpu.VMEM((sub, D), F32)]
    in_specs = [
        pl.BlockSpec((1, ts, D), lambda b, s: (b, s, 0)),
        whole(norm_mix),
        w_in_spec,
        layer_block(CONV_WIDTH, 1, cc),
        whole(conv_b), whole(conv_ng), whole(conv_nb),
        whole(sgu_ng), whole(sgu_nb),
        layer_block(SGU_HEADS, CHUNK, CHUNK),
        layer_block(SGU_HEADS, CHUNK, hd),
        w_out_spec,
        whole(gavg),
    ]
    assert len(in_specs) == _MIXER_INPUTS and len(ffn_args) in (0, _FFN_INPUTS)
    in_specs += ffn_specs
    return pl.pallas_call(
        kern,
        grid=(B, ns),
        in_specs=in_specs + cast_in_specs,
        out_specs=[pl.BlockSpec((1, ts, D), lambda b, s: (b, s, 0))] + cast_out_specs,
        out_shape=[jax.ShapeDtypeStruct((B, S, D), F32)] + cast_shapes,
        scratch_shapes=scratch,
        compiler_params=pltpu.CompilerParams(
            dimension_semantics=("arbitrary", "arbitrary"),
            vmem_limit_bytes=VMEM_LIMIT_BYTES),
        name="mixer",
    )(x, norm_mix, w_in, conv_w, conv_b, conv_ng, conv_nb, sgu_ng, sgu_nb, sgu_w, sb_full, w_out,
      gavg, *ffn_args, *cast_arrs)


def _swiglu_tile(h, wg_ref, wu_ref, wd_ref):
    g = jnp.dot(h, wg_ref[...], preferred_element_type=F32)
    u = jnp.dot(h, wu_ref[...], preferred_element_type=F32)
    a = (_silu(g) * u).astype(BF16)
    return jnp.dot(a, wd_ref[...], preferred_element_type=F32)


def _round_up(n, m):
    return (n + m - 1) // m * m


def _piece_sizes(largest):
    sizes = []
    s = largest
    while s >= BF16_ROWS:
        sizes.append(s)
        s //= 2
    assert sizes and sizes[-1] == BF16_ROWS
    return sizes


def _for_each_piece(count, sizes, fn):
    for s in sizes:
        offset = jnp.bitwise_and(count, -2 * s)

        @pl.when(jnp.bitwise_and(count, s) != 0)
        def _():
            fn(offset, s)


def _route_kernel(x_ref, nw_ref, rt_ref, hn_ref, dcol_ref, drow_ref, cnt_ref, *, win, n_experts):
    for k in range(x_ref.shape[0] // win):
        rows = slice(k * win, (k + 1) * win)
        _route_window(x_ref.at[rows], nw_ref, rt_ref, hn_ref.at[rows],
                      dcol_ref.at[rows], drow_ref.at[k], cnt_ref.at[k], n_experts)


def _route_window(x_ref, nw_ref, rt_ref, hn_ref, dcol_ref, drow_ref, cnt_ref, n_experts):
    x = x_ref[...]
    h = _rms_norm(x, nw_ref[...]).astype(BF16)
    hn_ref[...] = h
    logits = jnp.dot(h, rt_ref[...], preferred_element_type=F32)
    lane = lax.broadcasted_iota(jnp.int32, logits.shape, 1)
    lt = jnp.transpose(jnp.where(lane < n_experts, logits, -jnp.inf))[0:SUBLANES, :]
    win = lt.shape[1]
    sub = lax.broadcasted_iota(jnp.int32, lt.shape, 0)
    m1 = jnp.max(lt, axis=0, keepdims=True)
    i1 = jnp.min(jnp.where(lt == m1, sub, SUBLANES), axis=0, keepdims=True)
    rest = jnp.where(sub == i1, -jnp.inf, lt)
    m2 = jnp.max(rest, axis=0, keepdims=True)
    i2 = jnp.min(jnp.where(rest == m2, sub, SUBLANES), axis=0, keepdims=True)
    e2 = jnp.exp(m2 - m1)
    den = 1.0 + e2
    g1 = 1.0 / den
    g2 = e2 / den
    sel1 = sub == i1
    sel2 = sub == i2
    self = jnp.logical_or(sel1, sel2).astype(F32)
    tok = lax.broadcasted_iota(jnp.int32, lt.shape, 1)
    inc = self
    step = 1
    while step < win:
        inc = inc + jnp.where(tok >= step, pltpu.roll(inc, step, axis=1), 0.0)
        step *= 2
    rank = inc - self
    cnt = jnp.sum(self, axis=1, keepdims=True)
    cnt_pad = jnp.floor((cnt + (BF16_ROWS - 1.0)) * (1.0 / BF16_ROWS)) * BF16_ROWS
    seg = jnp.zeros_like(cnt_pad)
    for e in range(n_experts - 1):
        seg = seg + jnp.where(sub[:, 0:1] > e, cnt_pad[e:e + 1, :], 0.0)
    pos = seg + rank
    dest1 = jnp.sum(jnp.where(sel1, pos, 0.0), axis=0, keepdims=True)
    dest2 = jnp.sum(jnp.where(sel2, pos, 0.0), axis=0, keepdims=True)
    info = jnp.where(sub == 0, dest1, jnp.where(sub == 1, dest2,
                     jnp.where(sub == 2, g1, jnp.where(sub == 3, g2, -1.0))))
    drow_ref[...] = info
    cols = jnp.concatenate([info, jnp.broadcast_to(cnt, (SUBLANES, win)),
                            jnp.full((LANES - 2 * SUBLANES, win), -1.0, F32)], axis=0)
    cols = jnp.transpose(cols)
    dcol_ref[...] = cols
    cnt_ref[...] = cols[0:1, :].astype(jnp.int32)


def _route(x2, nw, router, *, win, per_step=4):
    T, D = x2.shape
    E = router.shape[-1]
    assert T % (win * per_step) == 0 and E <= SUBLANES
    nwin = T // win
    blk = win * per_step
    rt = jnp.zeros((D, LANES), F32).at[:, :E].set(router).astype(BF16)
    kern = functools.partial(_route_kernel, win=win, n_experts=E)
    return pl.pallas_call(
        kern,
        grid=(nwin // per_step,),
        in_specs=[
            pl.BlockSpec((blk, D), lambda w: (w, 0)),
            pl.BlockSpec((1, D), lambda w: (0, 0)),
            pl.BlockSpec((D, LANES), lambda w: (0, 0)),
        ],
        out_specs=[
            pl.BlockSpec((blk, D), lambda w: (w, 0)),
            pl.BlockSpec((blk, LANES), lambda w: (w, 0)),
            pl.BlockSpec((per_step, SUBLANES, win), lambda w: (w, 0, 0)),
            pl.BlockSpec((per_step, 1, LANES), lambda w: (w, 0, 0)),
        ],
        out_shape=[
            jax.ShapeDtypeStruct((T, D), BF16),
            jax.ShapeDtypeStruct((T, LANES), F32),
            jax.ShapeDtypeStruct((nwin, SUBLANES, win), F32),
            jax.ShapeDtypeStruct((nwin, 1, LANES), jnp.int32),
        ],
        compiler_params=pltpu.CompilerParams(
            dimension_semantics=("arbitrary",), vmem_limit_bytes=VMEM_LIMIT_BYTES),
        name="moe_route",
    )(x2, nw.reshape(1, D).astype(F32), rt)


def _compact_kernel(seg_ref, off_ref, cnt_ref, fill_ref, hn_ref, drow_ref, xs_ref,
                    stage_ref, zero_ref, sem, *, win, n_experts, rows, tm):
    w = pl.program_id(0)
    nwin = pl.num_programs(0)
    slot = w % 2
    d = hn_ref.shape[-1]
    sizes = _piece_sizes(win)

    def segments(ww, sl, start):
        for e in range(n_experts):
            src0 = seg_ref[ww * n_experts + e]
            dst0 = off_ref[ww * n_experts + e]

            def piece(offset, size):
                cp = pltpu.make_async_copy(
                    stage_ref.at[sl, pl.ds(pl.multiple_of(src0 + offset, BF16_ROWS), size)],
                    xs_ref.at[pl.ds(pl.multiple_of(dst0 + offset, BF16_ROWS), size)],
                    sem.at[sl, e])
                cp.start() if start else cp.wait()

            _for_each_piece(cnt_ref[ww * n_experts + e], sizes, piece)

    def zero_rows(dst, size, e, start):
        cp = pltpu.make_async_copy(zero_ref.at[pl.ds(0, size)],
                                   xs_ref.at[pl.ds(pl.multiple_of(dst, BF16_ROWS), size)],
                                   sem.at[2, e])
        cp.start() if start else cp.wait()

    @pl.when(w == 0)
    def _():
        zero_ref[...] = jnp.zeros(zero_ref.shape, BF16)

    d1 = drow_ref[0, 0:1, :]
    d2 = drow_ref[0, 1:2, :]
    g1 = drow_ref[0, 2:3, :]
    g2 = drow_ref[0, 3:4, :]
    r = lax.broadcasted_iota(jnp.int32, (rows, win), 0).astype(F32)
    m1 = r == d1
    m2 = r == d2
    onehot = jnp.logical_or(m1, m2).astype(BF16)
    xg = jnp.dot(onehot, hn_ref[...], preferred_element_type=F32)
    stage_ref[slot, :, 0:d] = xg.astype(BF16)
    gate = jnp.sum(jnp.where(m1, g1, 0.0) + jnp.where(m2, g2, 0.0), axis=-1, keepdims=True)
    hi = gate.astype(BF16).astype(F32)
    mid = (gate - hi).astype(BF16).astype(F32)
    lo = gate - hi - mid
    lane = lax.broadcasted_iota(jnp.int32, (rows, LANES), 1)
    terms = jnp.where(lane == 0, hi, jnp.where(lane == 1, mid, jnp.where(lane == 2, lo, 0.0)))
    stage_ref[slot, :, d:d + LANES] = terms.astype(BF16)

    segments(w, slot, True)

    @pl.when(w > 0)
    def _():
        segments(w - 1, 1 - slot, False)

    @pl.when(w == nwin - 1)
    def _():
        segments(w, slot, False)
        gap_sizes = _piece_sizes(tm // 2)
        for start in (True, False):
            for e in range(n_experts):
                _for_each_piece(fill_ref[n_experts + e], gap_sizes,
                                lambda offset, size, e=e: zero_rows(fill_ref[e] + offset, size, e, start))
        tail0 = fill_ref[2 * n_experts]
        ntail = fill_ref[2 * n_experts + 1]
        tail_rows = zero_ref.shape[0]

        def tail_start(j, c):
            zero_rows(tail0 + j * tail_rows, tail_rows, 0, True)
            return c

        def tail_wait(j, c):
            zero_rows(tail0 + j * tail_rows, tail_rows, 0, False)
            return c

        lax.fori_loop(0, ntail, tail_start, 0)
        lax.fori_loop(0, ntail, tail_wait, 0)


def _compact(seg, off, cnt_pad, fill, hn, drow, *, win, n_experts, cap, tm):
    T, D = hn.shape
    nwin = T // win
    rows = _round_up(TOP_K * win + n_experts * (BF16_ROWS - 1), BF16_ROWS)
    kern = functools.partial(_compact_kernel, win=win, n_experts=n_experts, rows=rows, tm=tm)
    return pl.pallas_call(
        kern,
        grid_spec=pltpu.PrefetchScalarGridSpec(
            num_scalar_prefetch=4,
            grid=(nwin,),
            in_specs=[
                pl.BlockSpec((win, D), lambda w, *_: (w, 0)),
                pl.BlockSpec((1, SUBLANES, win), lambda w, *_: (w, 0, 0)),
            ],
            out_specs=pl.BlockSpec(memory_space=pl.ANY),
            scratch_shapes=[pltpu.VMEM((2, rows, D + LANES), BF16),
                            pltpu.VMEM((tm // 2, D + LANES), BF16),
                            pltpu.SemaphoreType.DMA((3, n_experts))],
        ),
        out_shape=jax.ShapeDtypeStruct((cap, D + LANES), BF16),
        compiler_params=pltpu.CompilerParams(
            dimension_semantics=("arbitrary",), vmem_limit_bytes=VMEM_LIMIT_BYTES),
        name="moe_compact",
    )(seg, off, cnt_pad, fill, hn, drow)


def _expert_kernel(te_ref, tv_ref, xs_ref, wg_ref, wu_ref, wd_ref, ys_ref):
    i = pl.program_id(0)
    tm, d = ys_ref.shape
    half = tm // 2

    def rows(lo, hi):
        y = _swiglu_tile(xs_ref[lo:hi, 0:d], wg_ref.at[0], wu_ref.at[0], wd_ref.at[0])
        gate = jnp.sum(xs_ref[lo:hi, d:d + LANES].astype(F32), axis=-1, keepdims=True)
        ys_ref[lo:hi, :] = (gate * y).astype(BF16)

    @pl.when(tv_ref[i] == 2)
    def _():
        rows(0, tm)

    @pl.when(tv_ref[i] == 1)
    def _():
        rows(0, half)
        ys_ref[half:tm, :] = jnp.zeros((tm - half, d), BF16)

    @pl.when(tv_ref[i] == 0)
    def _():
        ys_ref[...] = jnp.zeros(ys_ref.shape, BF16)


def _experts(tile_e, tile_v, xs, wg, wu, wd, *, tm):
    cap = xs.shape[0]
    _, D, FF = wg.shape
    assert cap == tile_e.shape[0] * tm and xs.shape[1] == D + LANES and wg.dtype == BF16
    return pl.pallas_call(
        _expert_kernel,
        grid_spec=pltpu.PrefetchScalarGridSpec(
            num_scalar_prefetch=2,
            grid=(cap // tm,),
            in_specs=[
                pl.BlockSpec((tm, D + LANES), lambda i, te, tv: (i, 0)),
                pl.BlockSpec((1, D, FF), lambda i, te, tv: (te[i], 0, 0)),
                pl.BlockSpec((1, D, FF), lambda i, te, tv: (te[i], 0, 0)),
                pl.BlockSpec((1, FF, D), lambda i, te, tv: (te[i], 0, 0)),
            ],
            out_specs=pl.BlockSpec((tm, D), lambda i, te, tv: (i, 0)),
        ),
        out_shape=jax.ShapeDtypeStruct((cap, D), BF16),
        compiler_params=pltpu.CompilerParams(
            dimension_semantics=("arbitrary",), vmem_limit_bytes=VMEM_LIMIT_BYTES),
        name="moe_experts",
    )(tile_e, tile_v, xs, wg, wu, wd)


def _combine_kernel(seg_ref, off_ref, cnt_ref, x_ref, dcol_ref, fw_ref, ys_ref, o_ref,
                    stage_ref, sem, *, win, n_experts):
    w = pl.program_id(0)
    nwin = pl.num_programs(0)
    slot = w % 2
    rows = stage_ref.shape[1]
    sizes = _piece_sizes(win)

    def segments(ww, sl, start):
        for e in range(n_experts):
            src0 = off_ref[ww * n_experts + e]
            dst0 = seg_ref[ww * n_experts + e]

            def piece(offset, size):
                cp = pltpu.make_async_copy(
                    ys_ref.at[pl.ds(pl.multiple_of(src0 + offset, BF16_ROWS), size)],
                    stage_ref.at[sl, pl.ds(pl.multiple_of(dst0 + offset, BF16_ROWS), size)],
                    sem.at[sl, e])
                cp.start() if start else cp.wait()

            _for_each_piece(cnt_ref[ww * n_experts + e], sizes, piece)

    @pl.when(w == 0)
    def _():
        stage_ref[...] = jnp.zeros(stage_ref.shape, BF16)
        segments(0, 0, True)

    @pl.when(w + 1 < nwin)
    def _():
        segments(w + 1, 1 - slot, True)

    segments(w, slot, False)

    d1 = dcol_ref[:, 0:1]
    d2 = dcol_ref[:, 1:2]
    r = lax.broadcasted_iota(jnp.int32, (win, rows), 1).astype(F32)
    onehot = jnp.logical_or(r == d1, r == d2).astype(BF16)
    y = jnp.dot(onehot, stage_ref[slot], preferred_element_type=F32)
    o_ref[...] = _rms_norm(x_ref[...] + y, fw_ref[...])


def _combine(seg, off, cnt_pad, x2, dcol, fw, ys, *, win, n_experts):
    T, D = x2.shape
    rows = _round_up(TOP_K * win + n_experts * (BF16_ROWS - 1), BF16_ROWS)
    kern = functools.partial(_combine_kernel, win=win, n_experts=n_experts)
    return pl.pallas_call(
        kern,
        grid_spec=pltpu.PrefetchScalarGridSpec(
            num_scalar_prefetch=3,
            grid=(T // win,),
            in_specs=[
                pl.BlockSpec((win, D), lambda w, *_: (w, 0)),
                pl.BlockSpec((win, LANES), lambda w, *_: (w, 0)),
                pl.BlockSpec((1, D), lambda w, *_: (0, 0)),
                pl.BlockSpec(memory_space=pl.ANY),
            ],
            out_specs=pl.BlockSpec((win, D), lambda w, *_: (w, 0)),
            scratch_shapes=[pltpu.VMEM((2, rows, D), BF16),
                            pltpu.SemaphoreType.DMA((2, n_experts))],
        ),
        out_shape=jax.ShapeDtypeStruct((T, D), F32),
        compiler_params=pltpu.CompilerParams(
            dimension_semantics=("arbitrary",), vmem_limit_bytes=VMEM_LIMIT_BYTES),
        name="moe_combine",
    )(seg, off, cnt_pad, x2, dcol, fw.reshape(1, D).astype(F32), ys)


def _moe(x2, nw, router, wg, wu, wd, fw, *, win=MOE_WINDOW, tm=MOE_TILE):
    T, D = x2.shape
    E = router.shape[-1]
    nwin = T // win
    hn, dcol, drow, cnt = _route(x2, nw, router, win=win)

    cnt = cnt.reshape(nwin, LANES)[:, SUBLANES:SUBLANES + E]
    cnt_pad = _round_up(cnt, BF16_ROWS)
    seg = jnp.cumsum(cnt_pad, axis=1) - cnt_pad
    total = jnp.sum(cnt_pad, axis=0)
    region = _round_up(total, tm)
    r_end = jnp.cumsum(region)
    r_start = r_end - region
    off = r_start[None, :] + jnp.cumsum(cnt_pad, axis=0) - cnt_pad
    max_rows = TOP_K * T + nwin * E * (BF16_ROWS - 1)
    ntiles_max = max_rows // tm + E
    cap = ntiles_max * tm
    ti = jnp.arange(ntiles_max)
    t_end = r_end // tm
    tile_e = jnp.minimum(jnp.sum(ti[:, None] >= t_end[None, :], axis=1), E - 1).astype(jnp.int32)
    mine = tile_e[:, None] == jnp.arange(E)[None, :]
    live = jnp.sum(jnp.where(mine, total[None, :] - (ti[:, None] * tm - r_start[None, :]), 0), axis=1)
    live = jnp.clip(live, 0, tm)
    tile_v = jnp.where(ti < t_end[-1], 1 + (live > tm // 2), 0).astype(jnp.int32)
    tail = jnp.stack([r_end[-1], (cap - r_end[-1]) // (tm // 2)])
    fill = jnp.concatenate([r_start + total, region - total, tail])
    flat = lambda a: a.reshape(-1).astype(jnp.int32)
    seg, off, cnt_pad, fill = flat(seg), flat(off), flat(cnt_pad), flat(fill)

    xs = _compact(seg, off, cnt_pad, fill, hn, drow, win=win, n_experts=E, cap=cap, tm=tm)
    ys = _experts(tile_e, tile_v, xs, wg, wu, wd, tm=tm)
    return _combine(seg, off, cnt_pad, x2, dcol, fw, ys, win=win, n_experts=E)


def kernel(x, norm_mix, w_in, conv_w, conv_b, conv_ng, conv_nb, sgu_ng, sgu_nb, sgu_w, sgu_b, w_out,
           norm_ffn, ffn_wg, ffn_wu, ffn_wd, moe_router, moe_wg, moe_wu, moe_wd, norm_final):
    B, S, D = x.shape
    depth = norm_mix.shape[0]
    assert depth == 2, "trunk is one dense layer followed by one MoE layer"
    mix = functools.partial(_mixer, ts=MIXER_TILE, sub=MIXER_SUB_TILE)
    mixer_params = (norm_mix, w_in, conv_w, conv_b, conv_ng, conv_nb, sgu_ng, sgu_nb, sgu_w, sgu_b, w_out)
    x, ffn_wg_b, ffn_wu_b, ffn_wd_b, w_in1_b, w_out1_b, moe_wg_b, moe_wu_b = mix(
        x, 0, *mixer_params,
        cast=((ffn_wg, 0), (ffn_wu, 0), (ffn_wd, 0), (w_in, 1), (w_out, 1), (moe_wg, 0), (moe_wu, 0)))
    x, moe_wd_b = mix(x, 1, *mixer_params, proj=(w_in1_b, w_out1_b),
                      ffn=(0, norm_ffn, ffn_wg_b, ffn_wu_b, ffn_wd_b), cast=((moe_wd, 0),))
    y2 = _moe(x.reshape(B * S, D), norm_ffn[1], moe_router[0], moe_wg_b.reshape(moe_wg[0].shape),
              moe_wu_b.reshape(moe_wu[0].shape), moe_wd_b.reshape(moe_wd[0].shape), norm_final)
    return y2.reshape(B, S, D)
```

```python
import functools

import jax
import jax.numpy as jnp
from jax import lax
from jax.experimental import pallas as pl
from jax.experimental.pallas import tpu as pltpu

F32 = jnp.float32
BF16 = jnp.bfloat16

EPS = 1e-6
CONV_WIDTH = 31
CONV_GROUPS = 8
SGU_HEADS = 4
CHUNK = 128
TOP_K = 2

SUBLANES = 8
LANES = 128
BF16_ROWS = 16
CONV_HALO = 32
VMEM_LIMIT_BYTES = 58 * 1024 * 1024
MIXER_TILE = 512
MIXER_SUB_TILE = 256
_MIX_STAGES = 3
_MIX_PIECES = SUBLANES

MOE_WINDOW = 512
MOE_TILE = 512
MOE_TILE_PARTS = 4


def _rms_norm(x, g):
    ms = jnp.mean(x * x, axis=-1, keepdims=True)
    return x * lax.rsqrt(ms + EPS) * g


def _gelu(x):
    return 0.5 * x * (1.0 + lax.erf(x * (2.0 ** -0.5)))


def _sigmoid(x):
    return 0.5 * jnp.tanh(0.5 * x) + 0.5


def _silu(x):
    h = 0.5 * x
    return h * jnp.tanh(h) + h


def _cast_plan(rows, nsteps):
    nblk = nsteps
    while rows % nblk or (rows // nblk) % BF16_ROWS:
        nblk //= 2
        assert nblk >= 1, rows
    return nblk, rows // nblk


def _cast_specs(items, nsteps, step_of):
    arrs, in_specs, out_specs, shapes = [], [], [], []
    for a, layer in items:
        a = a.reshape(a.shape[0], -1, a.shape[-1])
        nblk, blk = _cast_plan(a.shape[1], nsteps)
        blk_of = lambda *g, nblk=nblk: jnp.minimum(step_of(*g), nblk - 1)
        arrs.append(a)
        in_specs.append(pl.BlockSpec((None, blk, a.shape[2]),
                                     lambda *g, layer=layer, blk_of=blk_of: (layer, blk_of(*g), 0)))
        out_specs.append(pl.BlockSpec((blk, a.shape[2]), lambda *g, blk_of=blk_of: (blk_of(*g), 0)))
        shapes.append(jax.ShapeDtypeStruct(a.shape[1:], BF16))
    return arrs, in_specs, out_specs, shapes


def _cast_blocks(src_refs, dst_refs):
    for src, dst in zip(src_refs, dst_refs):
        dst[...] = src[...].astype(BF16)


_MIXER_INPUTS = 13
_FFN_INPUTS = 5
_FFN_COLS = 256


def _mixer_kernel(*refs, layer, ts, sub, cc, sc, n_cast, ffn_layer):
    (x_ref, nw_all, win_in_ref, cw_ref, cb_all, cng_all, cnb_all, sng_all, snb_all,
     sw_ref, sb_ref, wout_in_ref, gavg_ref) = refs[:_MIXER_INPUTS]
    nw_ref, cb_ref, cng_ref, cnb_ref, sng_ref, snb_ref = (
        r.at[layer:layer + 1] for r in (nw_all, cb_all, cng_all, cnb_all, sng_all, snb_all))
    n_in = _MIXER_INPUTS
    if ffn_layer is not None:
        xn_ref, nwf_all, wg_ref, wu_ref, wd_ref = refs[n_in:n_in + _FFN_INPUTS]
        nwf_ref = nwf_all.at[ffn_layer:ffn_layer + 1]
        n_in += _FFN_INPUTS
    cast_in = refs[n_in:n_in + n_cast]
    o_ref = refs[n_in + n_cast]
    cast_out = refs[n_in + n_cast + 1:n_in + 2 * n_cast + 1]
    scratch = refs[n_in + 2 * n_cast + 1:]
    cbuf_ref = scratch[0]
    s = pl.program_id(1)
    _cast_blocks(cast_in, cast_out)

    if win_in_ref.dtype == BF16:
        win_ref, wout_ref = win_in_ref, wout_in_ref
    else:
        win_ref, wout_ref = scratch[1:3]

        @pl.when(jnp.logical_and(pl.program_id(0) == 0, s == 0))
        def _():
            win_ref[...] = win_in_ref[...].astype(BF16)
            wout_ref[...] = wout_in_ref[...].astype(BF16)

    @pl.when(s == 0)
    def _():
        cbuf_ref[0:CONV_HALO, :] = jnp.zeros((CONV_HALO, cc), F32)

    row = lax.broadcasted_iota(jnp.int32, (CHUNK, CHUNK), 0)
    col = lax.broadcasted_iota(jnp.int32, (CHUNK, CHUNK), 1)
    tril = row >= col
    hd = sc // SGU_HEADS
    nq = -(-CONV_WIDTH // SUBLANES)

    nsub = ts // sub
    d_in = 2 * cc + 2 * sc
    cbw = d_in // _MIX_PIECES

    def ffn_stage(x):
        hf = _rms_norm(x, nwf_ref[...]).astype(BF16)
        ff = wg_ref.shape[-1]
        fcols = [(c0, min(ff, c0 + _FFN_COLS)) for c0 in range(0, ff, _FFN_COLS)]
        per = -(-len(fcols) // 3)
        gu = []
        for w_ref in (wg_ref, wu_ref):
            blocks = []
            for i, (c0, c1) in enumerate(fcols):
                blocks.append(jnp.dot(hf, w_ref[:, c0:c1], preferred_element_type=F32))
                if i % per == per - 1 or i == len(fcols) - 1:
                    yield
            gu.append(jnp.concatenate(blocks, axis=1))
        act = (_silu(gu[0]) * gu[1]).astype(BF16)
        dm = wd_ref.shape[-1]
        yblk = []
        for i in range(4):
            c0, c1 = i * dm // 4, (i + 1) * dm // 4
            yblk.append(jnp.dot(act, wd_ref[:, c0:c1], preferred_element_type=F32))
            if i % 2 == 1 and i < 3:
                yield
        return x + jnp.concatenate(yblk, axis=1)

    if ffn_layer is not None:
        xcarry_ref = scratch[-1]

        @pl.when(jnp.logical_and(pl.program_id(0) == 0, s == 0))
        def _():
            stage = ffn_stage(x_ref[0, 0:sub, :])
            try:
                while True:
                    next(stage)
            except StopIteration as done:
                xcarry_ref[...] = done.value

    def next_first_sub_tile():
        xcarry_ref[...] = yield from ffn_stage(xn_ref[0])
        yield

    def sub_tile(j):
        r0 = j * sub
        if ffn_layer is None:
            x = x_ref[0, r0:r0 + sub, :]
        elif j == 0:
            x = xcarry_ref[...]
        else:
            x = yield from ffn_stage(x_ref[0, r0:r0 + sub, :])
            yield
        h = _rms_norm(x, nw_ref[...]).astype(BF16)
        zblk = []
        for k in range(_MIX_PIECES):
            zblk.append(jnp.dot(h, win_ref[:, k * cbw:(k + 1) * cbw], preferred_element_type=F32))
            yield
        z = jnp.concatenate(zblk, axis=1)
        a = z[:, :cc]
        gate = z[:, cc:2 * cc]
        u = z[:, 2 * cc:2 * cc + sc]
        v = z[:, 2 * cc + sc:]

        base = CONV_HALO + r0
        cbuf_ref[base:base + sub, :] = a * _sigmoid(gate)
        acc = jnp.broadcast_to(cb_ref[...], (sub, cc))
        for b in range(SUBLANES):
            zb = None
            for q in range(nq):
                d = SUBLANES * q + b
                if d >= CONV_WIDTH:
                    continue
                start = base - SUBLANES * (q + 1)
                term = (cw_ref[CONV_WIDTH - 1 - d]
                        * cbuf_ref[start:start + sub + SUBLANES, :])
                zb = term if zb is None else zb + term
            acc = acc + zb[SUBLANES - b:SUBLANES - b + sub, :]
            yield

        gavg = gavg_ref[...]
        acc_hi = acc.astype(BF16)
        acc_lo = (acc - acc_hi.astype(F32)).astype(BF16)
        mu = (jnp.dot(acc_hi, gavg, preferred_element_type=F32)
              + jnp.dot(acc_lo, gavg, preferred_element_type=F32))
        yield
        dev = acc - mu
        var = jnp.dot((dev * dev).astype(BF16), gavg, preferred_element_type=F32)
        yield
        cn = dev * lax.rsqrt(var + EPS) * cng_ref[...] + cnb_ref[...]
        c_out = _silu(cn).astype(BF16)
        yield

        u = _gelu(u)
        v = _gelu(v)
        yield
        g_cols = []
        for hh in range(SGU_HEADS):
            vh = v[:, hh * hd:(hh + 1) * hd]
            mu_h = jnp.mean(vh, axis=-1, keepdims=True)
            dh = vh - mu_h
            var_h = jnp.mean(dh * dh, axis=-1, keepdims=True)
            vn = (dh * lax.rsqrt(var_h + EPS) * sng_ref[:, hh * hd:(hh + 1) * hd]
                  + snb_ref[:, hh * hd:(hh + 1) * hd]).astype(BF16)
            ws = jnp.where(tril, sw_ref[hh], 0.0).astype(BF16)
            rows = []
            for ci in range(sub // CHUNK):
                sp = jnp.dot(ws, vn[ci * CHUNK:(ci + 1) * CHUNK, :], preferred_element_type=F32)
                rows.append(sp + sb_ref[hh])
            sp_h = rows[0] if len(rows) == 1 else jnp.concatenate(rows, axis=0)
            g_cols.append((u[:, hh * hd:(hh + 1) * hd] * sp_h).astype(BF16))
            if hh % 2 == 1:
                yield
        g_out = jnp.concatenate(g_cols, axis=1)

        yc = jnp.dot(c_out, wout_ref[0:cc, :], preferred_element_type=F32)
        yield
        yg = jnp.dot(g_out, wout_ref[cc:cc + sc, :], preferred_element_type=F32)
        o_ref[0, r0:r0 + sub, :] = x + (yc + yg)
        yield

    P = _MIX_PIECES
    if ffn_layer is None:
        plan = [(sub_tile(j), P * j, P * _MIX_STAGES) for j in range(nsub)]
    else:
        plan = [(sub_tile(0), 0, P * _MIX_STAGES)]
        plan += [(sub_tile(j), P * (2 * j - 1), P * (_MIX_STAGES + 1)) for j in range(1, nsub)]
        plan += [(next_first_sub_tile(), P * (2 * nsub - 1), P)]
    for tick in range(max(t0 + n for _, t0, n in plan)):
        for gen, t0, n in reversed(plan):
            if 0 <= tick - t0 < n:
                next(gen)

    cbuf_ref[0:CONV_HALO, :] = cbuf_ref[ts:ts + CONV_HALO, :]


def _mixer(x, layer, norm_mix, w_in, conv_w, conv_b, conv_ng, conv_nb, sgu_ng, sgu_nb, sgu_w, sgu_b,
           w_out, *, ts, sub, cast=(), proj=None, ffn=None):
    B, S, D = x.shape
    L = conv_w.shape[0]
    cc = conv_w.shape[-1]
    sc = sgu_ng.shape[-1]
    d_in = w_in.shape[-1]
    hd = sc // SGU_HEADS
    assert S % ts == 0 and ts % sub == 0 and sub % CHUNK == 0 and d_in == 2 * cc + 2 * sc
    ns = S // ts
    gs = cc // CONV_GROUPS
    gid = jnp.arange(cc) // gs
    gavg = jnp.where(gid[:, None] == gid[None, :], 1.0 / gs, 0.0).astype(BF16)
    sb_full = jnp.broadcast_to(sgu_b[:, :, :, None], (L, SGU_HEADS, CHUNK, hd))
    whole = lambda p: pl.BlockSpec(p.shape, lambda b, s: (0,) * p.ndim)
    layer_block = lambda *shape: pl.BlockSpec((None,) + shape, lambda b, s: (layer,) + (0,) * len(shape))
    cast_arrs, cast_in_specs, cast_out_specs, cast_shapes = _cast_specs(
        cast, B * ns, lambda b, s: b * ns + s)
    kern = functools.partial(_mixer_kernel, layer=layer, ts=ts, sub=sub, cc=cc, sc=sc, n_cast=len(cast),
                             ffn_layer=None if ffn is None else ffn[0])
    scratch = [pltpu.VMEM((CONV_HALO + ts, cc), F32)]
    if proj is None:
        w_in_spec, w_out_spec = layer_block(D, d_in), layer_block(cc + sc, D)
        scratch += [pltpu.VMEM((D, d_in), BF16), pltpu.VMEM((cc + sc, D), BF16)]
    else:
        w_in, w_out = proj
        assert w_in.dtype == BF16 and w_out.dtype == BF16
        w_in_spec, w_out_spec = whole(w_in), whole(w_out)
    ffn_args, ffn_specs = (), []
    if ffn is not None:
        def next_first(b, s):
            step = jnp.minimum(b * ns + s + 1, B * ns - 1)
            return (step // ns, (step % ns) * (ts // sub), 0)

        ffn_args = (x,) + tuple(ffn[1:])
        ffn_specs = [pl.BlockSpec((1, sub, D), next_first)] + [whole(a) for a in ffn[1:]]
        scratch += [pltpu.VMEM((sub, D), F32)]
    in_specs = [
        pl.BlockSpec((1, ts, D), lambda b, s: (b, s, 0)),
        whole(norm_mix),
        w_in_spec,
        layer_block(CONV_WIDTH, 1, cc),
        whole(conv_b), whole(conv_ng), whole(conv_nb),
        whole(sgu_ng), whole(sgu_nb),
        layer_block(SGU_HEADS, CHUNK, CHUNK),
        layer_block(SGU_HEADS, CHUNK, hd),
        w_out_spec,
        whole(gavg),
    ]
    assert len(in_specs) == _MIXER_INPUTS and len(ffn_args) in (0, _FFN_INPUTS)
    in_specs += ffn_specs
    return pl.pallas_call(
        kern,
        grid=(B, ns),
        in_specs=in_specs + cast_in_specs,
        out_specs=[pl.BlockSpec((1, ts, D), lambda b, s: (b, s, 0))] + cast_out_specs,
        out_shape=[jax.ShapeDtypeStruct((B, S, D), F32)] + cast_shapes,
        scratch_shapes=scratch,
        compiler_params=pltpu.CompilerParams(
            dimension_semantics=("arbitrary", "arbitrary"),
            vmem_limit_bytes=VMEM_LIMIT_BYTES),
        name="mixer",
    )(x, norm_mix, w_in, conv_w, conv_b, conv_ng, conv_nb, sgu_ng, sgu_nb, sgu_w, sb_full, w_out,
      gavg, *ffn_args, *cast_arrs)


def _swiglu_tile(h, wg_ref, wu_ref, wd_ref):
    g = jnp.dot(h, wg_ref[...], preferred_element_type=F32)
    u = jnp.dot(h, wu_ref[...], preferred_element_type=F32)
    a = (_silu(g) * u).astype(BF16)
    return jnp.dot(a, wd_ref[...], preferred_element_type=F32)


def _round_up(n, m):
    return (n + m - 1) // m * m


def _piece_sizes(largest):
    sizes = []
    s = largest
    while s >= BF16_ROWS:
        sizes.append(s)
        s //= 2
    assert sizes and sizes[-1] == BF16_ROWS
    return sizes


def _for_each_piece(count, sizes, fn):
    for s in sizes:
        offset = jnp.bitwise_and(count, -2 * s)

        @pl.when(jnp.bitwise_and(count, s) != 0)
        def _():
            fn(offset, s)


def _route_kernel(x_ref, nw_ref, rt_ref, hn_ref, dcol_ref, drow_ref, cnt_ref, *, win, n_experts):
    for k in range(x_ref.shape[0] // win):
        rows = slice(k * win, (k + 1) * win)
        _route_window(x_ref.at[rows], nw_ref, rt_ref, hn_ref.at[rows],
                      dcol_ref.at[rows], drow_ref.at[k], cnt_ref.at[k], n_experts)


def _route_window(x_ref, nw_ref, rt_ref, hn_ref, dcol_ref, drow_ref, cnt_ref, n_experts):
    x = x_ref[...]
    h = _rms_norm(x, nw_ref[...]).astype(BF16)
    hn_ref[...] = h
    logits = jnp.dot(h, rt_ref[...], preferred_element_type=F32)
    lane = lax.broadcasted_iota(jnp.int32, logits.shape, 1)
    lt = jnp.transpose(jnp.where(lane < n_experts, logits, -jnp.inf))[0:SUBLANES, :]
    win = lt.shape[1]
    sub = lax.broadcasted_iota(jnp.int32, lt.shape, 0)
    m1 = jnp.max(lt, axis=0, keepdims=True)
    i1 = jnp.min(jnp.where(lt == m1, sub, SUBLANES), axis=0, keepdims=True)
    rest = jnp.where(sub == i1, -jnp.inf, lt)
    m2 = jnp.max(rest, axis=0, keepdims=True)
    i2 = jnp.min(jnp.where(rest == m2, sub, SUBLANES), axis=0, keepdims=True)
    e2 = jnp.exp(m2 - m1)
    den = 1.0 + e2
    g1 = 1.0 / den
    g2 = e2 / den
    sel1 = sub == i1
    sel2 = sub == i2
    self = jnp.logical_or(sel1, sel2).astype(F32)
    tok = lax.broadcasted_iota(jnp.int32, lt.shape, 1)
    inc = self
    step = 1
    while step < win:
        inc = inc + jnp.where(tok >= step, pltpu.roll(inc, step, axis=1), 0.0)
        step *= 2
    rank = inc - self
    cnt = jnp.sum(self, axis=1, keepdims=True)
    cnt_pad = jnp.floor((cnt + (BF16_ROWS - 1.0)) * (1.0 / BF16_ROWS)) * BF16_ROWS
    seg = jnp.zeros_like(cnt_pad)
    for e in range(n_experts - 1):
        seg = seg + jnp.where(sub[:, 0:1] > e, cnt_pad[e:e + 1, :], 0.0)
    pos = seg + rank
    dest1 = jnp.sum(jnp.where(sel1, pos, 0.0), axis=0, keepdims=True)
    dest2 = jnp.sum(jnp.where(sel2, pos, 0.0), axis=0, keepdims=True)
    info = jnp.where(sub == 0, dest1, jnp.where(sub == 1, dest2,
                     jnp.where(sub == 2, g1, jnp.where(sub == 3, g2, -1.0))))
    drow_ref[...] = info
    cols = jnp.concatenate([info, jnp.broadcast_to(cnt, (SUBLANES, win)),
                            jnp.full((LANES - 2 * SUBLANES, win), -1.0, F32)], axis=0)
    cols = jnp.transpose(cols)
    dcol_ref[...] = cols
    cnt_ref[...] = cols[0:1, :].astype(jnp.int32)


def _route(x2, nw, router, *, win, per_step=4):
    T, D = x2.shape
    E = router.shape[-1]
    assert T % (win * per_step) == 0 and E <= SUBLANES
    nwin = T // win
    blk = win * per_step
    rt = jnp.zeros((D, LANES), F32).at[:, :E].set(router).astype(BF16)
    kern = functools.partial(_route_kernel, win=win, n_experts=E)
    return pl.pallas_call(
        kern,
        grid=(nwin // per_step,),
        in_specs=[
            pl.BlockSpec((blk, D), lambda w: (w, 0)),
            pl.BlockSpec((1, D), lambda w: (0, 0)),
            pl.BlockSpec((D, LANES), lambda w: (0, 0)),
        ],
        out_specs=[
            pl.BlockSpec((blk, D), lambda w: (w, 0)),
            pl.BlockSpec((blk, LANES), lambda w: (w, 0)),
            pl.BlockSpec((per_step, SUBLANES, win), lambda w: (w, 0, 0)),
            pl.BlockSpec((per_step, 1, LANES), lambda w: (w, 0, 0)),
        ],
        out_shape=[
            jax.ShapeDtypeStruct((T, D), BF16),
            jax.ShapeDtypeStruct((T, LANES), F32),
            jax.ShapeDtypeStruct((nwin, SUBLANES, win), F32),
            jax.ShapeDtypeStruct((nwin, 1, LANES), jnp.int32),
        ],
        compiler_params=pltpu.CompilerParams(
            dimension_semantics=("arbitrary",), vmem_limit_bytes=VMEM_LIMIT_BYTES),
        name="moe_route",
    )(x2, nw.reshape(1, D).astype(F32), rt)


def _compact_kernel(seg_ref, off_ref, cnt_ref, fill_ref, hn_ref, drow_ref, xs_ref,
                    stage_ref, zero_ref, sem, *, win, n_experts, rows, tm):
    w = pl.program_id(0)
    nwin = pl.num_programs(0)
    slot = w % 2
    d = hn_ref.shape[-1]
    sizes = _piece_sizes(win)

    def segments(ww, sl, start):
        for e in range(n_experts):
            src0 = seg_ref[ww * n_experts + e]
            dst0 = off_ref[ww * n_experts + e]

            def piece(offset, size):
                cp = pltpu.make_async_copy(
                    stage_ref.at[sl, pl.ds(pl.multiple_of(src0 + offset, BF16_ROWS), size)],
                    xs_ref.at[pl.ds(pl.multiple_of(dst0 + offset, BF16_ROWS), size)],
                    sem.at[sl, e])
                cp.start() if start else cp.wait()

            _for_each_piece(cnt_ref[ww * n_experts + e], sizes, piece)

    def zero_rows(dst, size, e, start):
        cp = pltpu.make_async_copy(zero_ref.at[pl.ds(0, size)],
                                   xs_ref.at[pl.ds(pl.multiple_of(dst, BF16_ROWS), size)],
                                   sem.at[2, e])
        cp.start() if start else cp.wait()

    @pl.when(w == 0)
    def _():
        zero_ref[...] = jnp.zeros(zero_ref.shape, BF16)

    d1 = drow_ref[0, 0:1, :]
    d2 = drow_ref[0, 1:2, :]
    g1 = drow_ref[0, 2:3, :]
    g2 = drow_ref[0, 3:4, :]
    r = lax.broadcasted_iota(jnp.int32, (rows, win), 0).astype(F32)
    m1 = r == d1
    m2 = r == d2
    onehot = jnp.logical_or(m1, m2).astype(BF16)
    xg = jnp.dot(onehot, hn_ref[...], preferred_element_type=F32)
    stage_ref[slot, :, 0:d] = xg.astype(BF16)
    gate = jnp.sum(jnp.where(m1, g1, 0.0) + jnp.where(m2, g2, 0.0), axis=-1, keepdims=True)
    hi = gate.astype(BF16).astype(F32)
    mid = (gate - hi).astype(BF16).astype(F32)
    lo = gate - hi - mid
    lane = lax.broadcasted_iota(jnp.int32, (rows, LANES), 1)
    terms = jnp.where(lane == 0, hi, jnp.where(lane == 1, mid, jnp.where(lane == 2, lo, 0.0)))
    stage_ref[slot, :, d:d + LANES] = terms.astype(BF16)

    segments(w, slot, True)

    @pl.when(w > 0)
    def _():
        segments(w - 1, 1 - slot, False)

    @pl.when(w == nwin - 1)
    def _():
        segments(w, slot, False)
        gap_sizes = _piece_sizes(tm // 2)
        for start in (True, False):
            for e in range(n_experts):
                _for_each_piece(fill_ref[n_experts + e], gap_sizes,
                                lambda offset, size, e=e: zero_rows(fill_ref[e] + offset, size, e, start))
        tail0 = fill_ref[2 * n_experts]
        ntail = fill_ref[2 * n_experts + 1]
        tail_rows = zero_ref.shape[0]

        def tail_start(j, c):
            zero_rows(tail0 + j * tail_rows, tail_rows, 0, True)
            return c

        def tail_wait(j, c):
            zero_rows(tail0 + j * tail_rows, tail_rows, 0, False)
            return c

        lax.fori_loop(0, ntail, tail_start, 0)
        lax.fori_loop(0, ntail, tail_wait, 0)


def _compact(seg, off, cnt_pad, fill, hn, drow, *, win, n_experts, cap, tm):
    T, D = hn.shape
    nwin = T // win
    rows = _round_up(TOP_K * win + n_experts * (BF16_ROWS - 1), BF16_ROWS)
    kern = functools.partial(_compact_kernel, win=win, n_experts=n_experts, rows=rows, tm=tm)
    return pl.pallas_call(
        kern,
        grid_spec=pltpu.PrefetchScalarGridSpec(
            num_scalar_prefetch=4,
            grid=(nwin,),
            in_specs=[
                pl.BlockSpec((win, D), lambda w, *_: (w, 0)),
                pl.BlockSpec((1, SUBLANES, win), lambda w, *_: (w, 0, 0)),
            ],
            out_specs=pl.BlockSpec(memory_space=pl.ANY),
            scratch_shapes=[pltpu.VMEM((2, rows, D + LANES), BF16),
                            pltpu.VMEM((tm // 2, D + LANES), BF16),
                            pltpu.SemaphoreType.DMA((3, n_experts))],
        ),
        out_shape=jax.ShapeDtypeStruct((cap, D + LANES), BF16),
        compiler_params=pltpu.CompilerParams(
            dimension_semantics=("arbitrary",), vmem_limit_bytes=VMEM_LIMIT_BYTES),
        name="moe_compact",
    )(seg, off, cnt_pad, fill, hn, drow)


def _expert_kernel(te_ref, tv_ref, xs_ref, wg_ref, wu_ref, wd_ref, ys_ref):
    i = pl.program_id(0)
    tm, d = ys_ref.shape
    part = tm // MOE_TILE_PARTS

    for k in range(MOE_TILE_PARTS + 1):
        @pl.when(tv_ref[i] == k)
        def _(k=k):
            n = k * part
            if n:
                y = _swiglu_tile(xs_ref[0:n, 0:d], wg_ref.at[0], wu_ref.at[0], wd_ref.at[0])
                gate = jnp.sum(xs_ref[0:n, d:d + LANES].astype(F32), axis=-1, keepdims=True)
                ys_ref[0:n, :] = (gate * y).astype(BF16)
            if n < tm:
                ys_ref[n:tm, :] = jnp.zeros((tm - n, d), BF16)


def _experts(tile_e, tile_v, xs, wg, wu, wd, *, tm):
    cap = xs.shape[0]
    _, D, FF = wg.shape
    assert cap == tile_e.shape[0] * tm and xs.shape[1] == D + LANES and wg.dtype == BF16
    return pl.pallas_call(
        _expert_kernel,
        grid_spec=pltpu.PrefetchScalarGridSpec(
            num_scalar_prefetch=2,
            grid=(cap // tm,),
            in_specs=[
                pl.BlockSpec((tm, D + LANES), lambda i, te, tv: (i, 0)),
                pl.BlockSpec((1, D, FF), lambda i, te, tv: (te[i], 0, 0)),
                pl.BlockSpec((1, D, FF), lambda i, te, tv: (te[i], 0, 0)),
                pl.BlockSpec((1, FF, D), lambda i, te, tv: (te[i], 0, 0)),
            ],
            out_specs=pl.BlockSpec((tm, D), lambda i, te, tv: (i, 0)),
        ),
        out_shape=jax.ShapeDtypeStruct((cap, D), BF16),
        compiler_params=pltpu.CompilerParams(
            dimension_semantics=("arbitrary",), vmem_limit_bytes=VMEM_LIMIT_BYTES),
        name="moe_experts",
    )(tile_e, tile_v, xs, wg, wu, wd)


def _combine_kernel(seg_ref, off_ref, cnt_ref, x_ref, dcol_ref, fw_ref, ys_ref, o_ref,
                    stage_ref, sem, *, win, n_experts):
    w = pl.program_id(0)
    nwin = pl.num_programs(0)
    slot = w % 2
    rows = stage_ref.shape[1]
    sizes = _piece_sizes(win)

    def segments(ww, sl, start):
        for e in range(n_experts):
            src0 = off_ref[ww * n_experts + e]
            dst0 = seg_ref[ww * n_experts + e]

            def piece(offset, size):
                cp = pltpu.make_async_copy(
                    ys_ref.at[pl.ds(pl.multiple_of(src0 + offset, BF16_ROWS), size)],
                    stage_ref.at[sl, pl.ds(pl.multiple_of(dst0 + offset, BF16_ROWS), size)],
                    sem.at[sl, e])
                cp.start() if start else cp.wait()

            _for_each_piece(cnt_ref[ww * n_experts + e], sizes, piece)

    @pl.when(w == 0)
    def _():
        stage_ref[...] = jnp.zeros(stage_ref.shape, BF16)
        segments(0, 0, True)

    @pl.when(w + 1 < nwin)
    def _():
        segments(w + 1, 1 - slot, True)

    segments(w, slot, False)

    d1 = dcol_ref[:, 0:1]
    d2 = dcol_ref[:, 1:2]
    r = lax.broadcasted_iota(jnp.int32, (win, rows), 1).astype(F32)
    onehot = jnp.logical_or(r == d1, r == d2).astype(BF16)
    y = jnp.dot(onehot, stage_ref[slot], preferred_element_type=F32)
    o_ref[...] = _rms_norm(x_ref[...] + y, fw_ref[...])


def _combine(seg, off, cnt_pad, x2, dcol, fw, ys, *, win, n_experts):
    T, D = x2.shape
    rows = _round_up(TOP_K * win + n_experts * (BF16_ROWS - 1), BF16_ROWS)
    kern = functools.partial(_combine_kernel, win=win, n_experts=n_experts)
    return pl.pallas_call(
        kern,
        grid_spec=pltpu.PrefetchScalarGridSpec(
            num_scalar_prefetch=3,
            grid=(T // win,),
            in_specs=[
                pl.BlockSpec((win, D), lambda w, *_: (w, 0)),
                pl.BlockSpec((win, LANES), lambda w, *_: (w, 0)),
                pl.BlockSpec((1, D), lambda w, *_: (0, 0)),
                pl.BlockSpec(memory_space=pl.ANY),
            ],
            out_specs=pl.BlockSpec((win, D), lambda w, *_: (w, 0)),
            scratch_shapes=[pltpu.VMEM((2, rows, D), BF16),
                            pltpu.SemaphoreType.DMA((2, n_experts))],
        ),
        out_shape=jax.ShapeDtypeStruct((T, D), F32),
        compiler_params=pltpu.CompilerParams(
            dimension_semantics=("arbitrary",), vmem_limit_bytes=VMEM_LIMIT_BYTES),
        name="moe_combine",
    )(seg, off, cnt_pad, x2, dcol, fw.reshape(1, D).astype(F32), ys)


def _moe(x2, nw, router, wg, wu, wd, fw, *, win=MOE_WINDOW, tm=MOE_TILE):
    T, D = x2.shape
    E = router.shape[-1]
    nwin = T // win
    hn, dcol, drow, cnt = _route(x2, nw, router, win=win)

    cnt = cnt.reshape(nwin, LANES)[:, SUBLANES:SUBLANES + E]
    cnt_pad = _round_up(cnt, BF16_ROWS)
    seg = jnp.cumsum(cnt_pad, axis=1) - cnt_pad
    total = jnp.sum(cnt_pad, axis=0)
    region = _round_up(total, tm)
    r_end = jnp.cumsum(region)
    r_start = r_end - region
    off = r_start[None, :] + jnp.cumsum(cnt_pad, axis=0) - cnt_pad
    max_rows = TOP_K * T + nwin * E * (BF16_ROWS - 1)
    ntiles_max = max_rows // tm + E
    cap = ntiles_max * tm
    ti = jnp.arange(ntiles_max)
    t_end = r_end // tm
    tile_e = jnp.minimum(jnp.sum(ti[:, None] >= t_end[None, :], axis=1), E - 1).astype(jnp.int32)
    mine = tile_e[:, None] == jnp.arange(E)[None, :]
    live = jnp.sum(jnp.where(mine, total[None, :] - (ti[:, None] * tm - r_start[None, :]), 0), axis=1)
    live = jnp.clip(live, 0, tm)
    part = tm // MOE_TILE_PARTS
    tile_v = jnp.where(ti < t_end[-1], (live + part - 1) // part, 0).astype(jnp.int32)
    tail = jnp.stack([r_end[-1], (cap - r_end[-1]) // (tm // 2)])
    fill = jnp.concatenate([r_start + total, region - total, tail])
    flat = lambda a: a.reshape(-1).astype(jnp.int32)
    seg, off, cnt_pad, fill = flat(seg), flat(off), flat(cnt_pad), flat(fill)

    xs = _compact(seg, off, cnt_pad, fill, hn, drow, win=win, n_experts=E, cap=cap, tm=tm)
    ys = _experts(tile_e, tile_v, xs, wg, wu, wd, tm=tm)
    return _combine(seg, off, cnt_pad, x2, dcol, fw, ys, win=win, n_experts=E)


def kernel(x, norm_mix, w_in, conv_w, conv_b, conv_ng, conv_nb, sgu_ng, sgu_nb, sgu_w, sgu_b, w_out,
           norm_ffn, ffn_wg, ffn_wu, ffn_wd, moe_router, moe_wg, moe_wu, moe_wd, norm_final):
    B, S, D = x.shape
    depth = norm_mix.shape[0]
    assert depth == 2, "trunk is one dense layer followed by one MoE layer"
    mix = functools.partial(_mixer, ts=MIXER_TILE, sub=MIXER_SUB_TILE)
    mixer_params = (norm_mix, w_in, conv_w, conv_b, conv_ng, conv_nb, sgu_ng, sgu_nb, sgu_w, sgu_b, w_out)
    x, ffn_wg_b, ffn_wu_b, ffn_wd_b, w_in1_b, w_out1_b, moe_wg_b, moe_wu_b = mix(
        x, 0, *mixer_params,
        cast=((ffn_wg, 0), (ffn_wu, 0), (ffn_wd, 0), (w_in, 1), (w_out, 1), (moe_wg, 0), (moe_wu, 0)))
    x, moe_wd_b = mix(x, 1, *mixer_params, proj=(w_in1_b, w_out1_b),
                      ffn=(0, norm_ffn, ffn_wg_b, ffn_wu_b, ffn_wd_b), cast=((moe_wd, 0),))
    y2 = _moe(x.reshape(B * S, D), norm_ffn[1], moe_router[0], moe_wg_b.reshape(moe_wg[0].shape),
              moe_wu_b.reshape(moe_wu[0].shape), moe_wd_b.reshape(moe_wd[0].shape), norm_final)
    return y2.reshape(B, S, D)
```

```python
import functools

import jax
import jax.numpy as jnp
from jax import lax
from jax.experimental import pallas as pl
from jax.experimental.pallas import tpu as pltpu

F32 = jnp.float32
BF16 = jnp.bfloat16

EPS = 1e-6
CONV_WIDTH = 31
CONV_GROUPS = 8
SGU_HEADS = 4
CHUNK = 128
TOP_K = 2

SUBLANES = 8
LANES = 128
BF16_ROWS = 16
CONV_HALO = 32
VMEM_LIMIT_BYTES = 60 * 1024 * 1024
MIXER_TILE = 512
MIXER_SUB_TILE = 256
_MIX_STAGES = 3
_MIX_PIECES = SUBLANES

MOE_WINDOW = 512
MOE_TILE = 512
MOE_TILE_PARTS = 4


def _rms_norm(x, g):
    ms = jnp.mean(x * x, axis=-1, keepdims=True)
    return x * lax.rsqrt(ms + EPS) * g


def _gelu(x):
    return 0.5 * x * (1.0 + lax.erf(x * (2.0 ** -0.5)))


def _sigmoid(x):
    return 0.5 * jnp.tanh(0.5 * x) + 0.5


def _silu(x):
    h = 0.5 * x
    return h * jnp.tanh(h) + h


def _cast_plan(rows, nsteps):
    nblk = nsteps
    while rows % nblk or (rows // nblk) % BF16_ROWS:
        nblk //= 2
        assert nblk >= 1, rows
    return nblk, rows // nblk


def _cast_specs(items, nsteps, step_of):
    arrs, in_specs, out_specs, shapes = [], [], [], []
    for a, layer in items:
        a = a.reshape(a.shape[0], -1, a.shape[-1])
        nblk, blk = _cast_plan(a.shape[1], nsteps)
        blk_of = lambda *g, nblk=nblk: jnp.minimum(step_of(*g), nblk - 1)
        arrs.append(a)
        in_specs.append(pl.BlockSpec((None, blk, a.shape[2]),
                                     lambda *g, layer=layer, blk_of=blk_of: (layer, blk_of(*g), 0)))
        out_specs.append(pl.BlockSpec((blk, a.shape[2]), lambda *g, blk_of=blk_of: (blk_of(*g), 0)))
        shapes.append(jax.ShapeDtypeStruct(a.shape[1:], BF16))
    return arrs, in_specs, out_specs, shapes


def _cast_blocks(src_refs, dst_refs):
    for src, dst in zip(src_refs, dst_refs):
        dst[...] = src[...].astype(BF16)


_MIXER_INPUTS = 13
_FFN_INPUTS = 5
_FFN_COLS = 256


def _mixer_kernel(*refs, layer, ts, sub, cc, sc, n_cast, ffn_layer):
    (x_ref, nw_all, win_in_ref, cw_ref, cb_all, cng_all, cnb_all, sng_all, snb_all,
     sw_ref, sb_ref, wout_in_ref, gavg_ref) = refs[:_MIXER_INPUTS]
    nw_ref, cb_ref, cng_ref, cnb_ref, sng_ref, snb_ref = (
        r.at[layer:layer + 1] for r in (nw_all, cb_all, cng_all, cnb_all, sng_all, snb_all))
    n_in = _MIXER_INPUTS
    if ffn_layer is not None:
        xn_ref, nwf_all, wg_ref, wu_ref, wd_ref = refs[n_in:n_in + _FFN_INPUTS]
        nwf_ref = nwf_all.at[ffn_layer:ffn_layer + 1]
        n_in += _FFN_INPUTS
    cast_in = refs[n_in:n_in + n_cast]
    o_ref = refs[n_in + n_cast]
    n_out = 1
    if ffn_layer is not None:
        hn_ref = refs[n_in + n_cast + 1]
        nwo_ref = nwf_all.at[layer:layer + 1]
        n_out = 2
    cast_out = refs[n_in + n_cast + n_out:n_in + 2 * n_cast + n_out]
    scratch = refs[n_in + 2 * n_cast + n_out:]
    cbuf_ref = scratch[0]
    s = pl.program_id(1)
    _cast_blocks(cast_in, cast_out)

    if win_in_ref.dtype == BF16:
        win_ref, wout_ref = win_in_ref, wout_in_ref
    else:
        win_ref, wout_ref = scratch[1:3]

        @pl.when(jnp.logical_and(pl.program_id(0) == 0, s == 0))
        def _():
            win_ref[...] = win_in_ref[...].astype(BF16)
            wout_ref[...] = wout_in_ref[...].astype(BF16)

    @pl.when(s == 0)
    def _():
        cbuf_ref[0:CONV_HALO, :] = jnp.zeros((CONV_HALO, cc), F32)

    row = lax.broadcasted_iota(jnp.int32, (CHUNK, CHUNK), 0)
    col = lax.broadcasted_iota(jnp.int32, (CHUNK, CHUNK), 1)
    tril = row >= col
    hd = sc // SGU_HEADS
    nq = -(-CONV_WIDTH // SUBLANES)

    nsub = ts // sub
    d_in = 2 * cc + 2 * sc
    cbw = d_in // _MIX_PIECES

    def ffn_stage(x):
        hf = _rms_norm(x, nwf_ref[...]).astype(BF16)
        ff = wg_ref.shape[-1]
        fcols = [(c0, min(ff, c0 + _FFN_COLS)) for c0 in range(0, ff, _FFN_COLS)]
        per = -(-len(fcols) // 3)
        gu = []
        for w_ref in (wg_ref, wu_ref):
            blocks = []
            for i, (c0, c1) in enumerate(fcols):
                blocks.append(jnp.dot(hf, w_ref[:, c0:c1], preferred_element_type=F32))
                if i % per == per - 1 or i == len(fcols) - 1:
                    yield
            gu.append(jnp.concatenate(blocks, axis=1))
        act = (_silu(gu[0]) * gu[1]).astype(BF16)
        dm = wd_ref.shape[-1]
        yblk = []
        for i in range(4):
            c0, c1 = i * dm // 4, (i + 1) * dm // 4
            yblk.append(jnp.dot(act, wd_ref[:, c0:c1], preferred_element_type=F32))
            if i % 2 == 1 and i < 3:
                yield
        return x + jnp.concatenate(yblk, axis=1)

    if ffn_layer is not None:
        xcarry_ref = scratch[-1]

        @pl.when(jnp.logical_and(pl.program_id(0) == 0, s == 0))
        def _():
            stage = ffn_stage(x_ref[0, 0:sub, :])
            try:
                while True:
                    next(stage)
            except StopIteration as done:
                xcarry_ref[...] = done.value

    def next_first_sub_tile():
        xcarry_ref[...] = yield from ffn_stage(xn_ref[0])
        yield

    def sub_tile(j):
        r0 = j * sub
        if ffn_layer is None:
            x = x_ref[0, r0:r0 + sub, :]
        elif j == 0:
            x = xcarry_ref[...]
        else:
            x = yield from ffn_stage(x_ref[0, r0:r0 + sub, :])
            yield
        h = _rms_norm(x, nw_ref[...]).astype(BF16)
        zblk = []
        for k in range(_MIX_PIECES):
            zblk.append(jnp.dot(h, win_ref[:, k * cbw:(k + 1) * cbw], preferred_element_type=F32))
            yield
        z = jnp.concatenate(zblk, axis=1)
        a = z[:, :cc]
        gate = z[:, cc:2 * cc]
        u = z[:, 2 * cc:2 * cc + sc]
        v = z[:, 2 * cc + sc:]

        base = CONV_HALO + r0
        cbuf_ref[base:base + sub, :] = a * _sigmoid(gate)
        acc = jnp.broadcast_to(cb_ref[...], (sub, cc))
        for b in range(SUBLANES):
            zb = None
            for q in range(nq):
                d = SUBLANES * q + b
                if d >= CONV_WIDTH:
                    continue
                start = base - SUBLANES * (q + 1)
                term = (cw_ref[CONV_WIDTH - 1 - d]
                        * cbuf_ref[start:start + sub + SUBLANES, :])
                zb = term if zb is None else zb + term
            acc = acc + zb[SUBLANES - b:SUBLANES - b + sub, :]
            yield

        gavg = gavg_ref[...]
        acc_hi = acc.astype(BF16)
        acc_lo = (acc - acc_hi.astype(F32)).astype(BF16)
        mu = (jnp.dot(acc_hi, gavg, preferred_element_type=F32)
              + jnp.dot(acc_lo, gavg, preferred_element_type=F32))
        yield
        dev = acc - mu
        var = jnp.dot((dev * dev).astype(BF16), gavg, preferred_element_type=F32)
        yield
        cn = dev * lax.rsqrt(var + EPS) * cng_ref[...] + cnb_ref[...]
        c_out = _silu(cn).astype(BF16)
        yield

        u = _gelu(u)
        v = _gelu(v)
        yield
        g_cols = []
        for hh in range(SGU_HEADS):
            vh = v[:, hh * hd:(hh + 1) * hd]
            mu_h = jnp.mean(vh, axis=-1, keepdims=True)
            dh = vh - mu_h
            var_h = jnp.mean(dh * dh, axis=-1, keepdims=True)
            vn = (dh * lax.rsqrt(var_h + EPS) * sng_ref[:, hh * hd:(hh + 1) * hd]
                  + snb_ref[:, hh * hd:(hh + 1) * hd]).astype(BF16)
            ws = jnp.where(tril, sw_ref[hh], 0.0).astype(BF16)
            rows = []
            for ci in range(sub // CHUNK):
                sp = jnp.dot(ws, vn[ci * CHUNK:(ci + 1) * CHUNK, :], preferred_element_type=F32)
                rows.append(sp + sb_ref[hh])
            sp_h = rows[0] if len(rows) == 1 else jnp.concatenate(rows, axis=0)
            g_cols.append((u[:, hh * hd:(hh + 1) * hd] * sp_h).astype(BF16))
            if hh % 2 == 1:
                yield
        g_out = jnp.concatenate(g_cols, axis=1)

        yc = jnp.dot(c_out, wout_ref[0:cc, :], preferred_element_type=F32)
        yield
        yg = jnp.dot(g_out, wout_ref[cc:cc + sc, :], preferred_element_type=F32)
        out = x + (yc + yg)
        o_ref[0, r0:r0 + sub, :] = out
        if ffn_layer is not None:
            hn_ref[0, r0:r0 + sub, :] = _rms_norm(out, nwo_ref[...]).astype(BF16)
        yield

    P = _MIX_PIECES
    if ffn_layer is None:
        plan = [(sub_tile(j), P * j, P * _MIX_STAGES) for j in range(nsub)]
    else:
        plan = [(sub_tile(0), 0, P * _MIX_STAGES)]
        plan += [(sub_tile(j), P * (2 * j - 1), P * (_MIX_STAGES + 1)) for j in range(1, nsub)]
        plan += [(next_first_sub_tile(), P * (2 * nsub - 1), P)]
    for tick in range(max(t0 + n for _, t0, n in plan)):
        for gen, t0, n in reversed(plan):
            if 0 <= tick - t0 < n:
                next(gen)

    cbuf_ref[0:CONV_HALO, :] = cbuf_ref[ts:ts + CONV_HALO, :]


def _mixer(x, layer, norm_mix, w_in, conv_w, conv_b, conv_ng, conv_nb, sgu_ng, sgu_nb, sgu_w, sgu_b,
           w_out, *, ts, sub, cast=(), proj=None, ffn=None):
    B, S, D = x.shape
    L = conv_w.shape[0]
    cc = conv_w.shape[-1]
    sc = sgu_ng.shape[-1]
    d_in = w_in.shape[-1]
    hd = sc // SGU_HEADS
    assert S % ts == 0 and ts % sub == 0 and sub % CHUNK == 0 and d_in == 2 * cc + 2 * sc
    ns = S // ts
    gs = cc // CONV_GROUPS
    gid = jnp.arange(cc) // gs
    gavg = jnp.where(gid[:, None] == gid[None, :], 1.0 / gs, 0.0).astype(BF16)
    sb_full = jnp.broadcast_to(sgu_b[:, :, :, None], (L, SGU_HEADS, CHUNK, hd))
    whole = lambda p: pl.BlockSpec(p.shape, lambda b, s: (0,) * p.ndim)
    layer_block = lambda *shape: pl.BlockSpec((None,) + shape, lambda b, s: (layer,) + (0,) * len(shape))
    cast_arrs, cast_in_specs, cast_out_specs, cast_shapes = _cast_specs(
        cast, B * ns, lambda b, s: b * ns + s)
    kern = functools.partial(_mixer_kernel, layer=layer, ts=ts, sub=sub, cc=cc, sc=sc, n_cast=len(cast),
                             ffn_layer=None if ffn is None else ffn[0])
    scratch = [pltpu.VMEM((CONV_HALO + ts, cc), F32)]
    if proj is None:
        w_in_spec, w_out_spec = layer_block(D, d_in), layer_block(cc + sc, D)
        scratch += [pltpu.VMEM((D, d_in), BF16), pltpu.VMEM((cc + sc, D), BF16)]
    else:
        w_in, w_out = proj
        assert w_in.dtype == BF16 and w_out.dtype == BF16
        w_in_spec, w_out_spec = whole(w_in), whole(w_out)
    ffn_args, ffn_specs, n_act = (), [], 1
    if ffn is not None:
        n_act = 2
        def next_first(b, s):
            step = jnp.minimum(b * ns + s + 1, B * ns - 1)
            return (step // ns, (step % ns) * (ts // sub), 0)

        ffn_args = (x,) + tuple(ffn[1:])
        ffn_specs = [pl.BlockSpec((1, sub, D), next_first)] + [whole(a) for a in ffn[1:]]
        scratch += [pltpu.VMEM((sub, D), F32)]
    in_specs = [
        pl.BlockSpec((1, ts, D), lambda b, s: (b, s, 0)),
        whole(norm_mix),
        w_in_spec,
        layer_block(CONV_WIDTH, 1, cc),
        whole(conv_b), whole(conv_ng), whole(conv_nb),
        whole(sgu_ng), whole(sgu_nb),
        layer_block(SGU_HEADS, CHUNK, CHUNK),
        layer_block(SGU_HEADS, CHUNK, hd),
        w_out_spec,
        whole(gavg),
    ]
    assert len(in_specs) == _MIXER_INPUTS and len(ffn_args) in (0, _FFN_INPUTS)
    in_specs += ffn_specs
    return pl.pallas_call(
        kern,
        grid=(B, ns),
        in_specs=in_specs + cast_in_specs,
        out_specs=[pl.BlockSpec((1, ts, D), lambda b, s: (b, s, 0))] * n_act + cast_out_specs,
        out_shape=([jax.ShapeDtypeStruct((B, S, D), F32), jax.ShapeDtypeStruct((B, S, D), BF16)][:n_act]
                   + cast_shapes),
        scratch_shapes=scratch,
        compiler_params=pltpu.CompilerParams(
            dimension_semantics=("arbitrary", "arbitrary"),
            vmem_limit_bytes=VMEM_LIMIT_BYTES),
        name="mixer",
    )(x, norm_mix, w_in, conv_w, conv_b, conv_ng, conv_nb, sgu_ng, sgu_nb, sgu_w, sb_full, w_out,
      gavg, *ffn_args, *cast_arrs)


def _swiglu_tile(h, wg_ref, wu_ref, wd_ref):
    g = jnp.dot(h, wg_ref[...], preferred_element_type=F32)
    u = jnp.dot(h, wu_ref[...], preferred_element_type=F32)
    a = (_silu(g) * u).astype(BF16)
    return jnp.dot(a, wd_ref[...], preferred_element_type=F32)


def _round_up(n, m):
    return (n + m - 1) // m * m


def _piece_sizes(largest):
    sizes = []
    s = largest
    while s >= BF16_ROWS:
        sizes.append(s)
        s //= 2
    assert sizes and sizes[-1] == BF16_ROWS
    return sizes


def _for_each_piece(count, sizes, fn):
    for s in sizes:
        offset = jnp.bitwise_and(count, -2 * s)

        @pl.when(jnp.bitwise_and(count, s) != 0)
        def _():
            fn(offset, s)


def _route_kernel(hn_ref, rt_ref, dcol_ref, drow_ref, cnt_ref, *, win, n_experts):
    for k in range(hn_ref.shape[0] // win):
        rows = slice(k * win, (k + 1) * win)
        _route_window(hn_ref.at[rows], rt_ref, dcol_ref.at[rows], drow_ref.at[k], cnt_ref.at[k],
                      n_experts)


def _route_window(hn_ref, rt_ref, dcol_ref, drow_ref, cnt_ref, n_experts):
    logits = jnp.dot(hn_ref[...], rt_ref[...], preferred_element_type=F32)
    lane = lax.broadcasted_iota(jnp.int32, logits.shape, 1)
    lt = jnp.transpose(jnp.where(lane < n_experts, logits, -jnp.inf))[0:SUBLANES, :]
    win = lt.shape[1]
    sub = lax.broadcasted_iota(jnp.int32, lt.shape, 0)
    m1 = jnp.max(lt, axis=0, keepdims=True)
    i1 = jnp.min(jnp.where(lt == m1, sub, SUBLANES), axis=0, keepdims=True)
    rest = jnp.where(sub == i1, -jnp.inf, lt)
    m2 = jnp.max(rest, axis=0, keepdims=True)
    i2 = jnp.min(jnp.where(rest == m2, sub, SUBLANES), axis=0, keepdims=True)
    e2 = jnp.exp(m2 - m1)
    den = 1.0 + e2
    g1 = 1.0 / den
    g2 = e2 / den
    sel1 = sub == i1
    sel2 = sub == i2
    self = jnp.logical_or(sel1, sel2).astype(F32)
    tok = lax.broadcasted_iota(jnp.int32, lt.shape, 1)
    inc = self
    step = 1
    while step < win:
        inc = inc + jnp.where(tok >= step, pltpu.roll(inc, step, axis=1), 0.0)
        step *= 2
    rank = inc - self
    cnt = jnp.sum(self, axis=1, keepdims=True)
    cnt_pad = jnp.floor((cnt + (BF16_ROWS - 1.0)) * (1.0 / BF16_ROWS)) * BF16_ROWS
    seg = jnp.zeros_like(cnt_pad)
    for e in range(n_experts - 1):
        seg = seg + jnp.where(sub[:, 0:1] > e, cnt_pad[e:e + 1, :], 0.0)
    pos = seg + rank
    dest1 = jnp.sum(jnp.where(sel1, pos, 0.0), axis=0, keepdims=True)
    dest2 = jnp.sum(jnp.where(sel2, pos, 0.0), axis=0, keepdims=True)
    info = jnp.where(sub == 0, dest1, jnp.where(sub == 1, dest2,
                     jnp.where(sub == 2, g1, jnp.where(sub == 3, g2, -1.0))))
    drow_ref[...] = info
    cols = jnp.concatenate([info, jnp.broadcast_to(cnt, (SUBLANES, win)),
                            jnp.full((LANES - 2 * SUBLANES, win), -1.0, F32)], axis=0)
    cols = jnp.transpose(cols)
    dcol_ref[...] = cols
    cnt_ref[...] = cols[0:1, :].astype(jnp.int32)


def _route(hn, router, *, win):
    T, D = hn.shape
    E = router.shape[-1]
    assert T % win == 0 and E <= SUBLANES
    nwin = T // win
    per_step = max(p for p in (8, 4, 2, 1) if nwin % p == 0)
    blk = win * per_step
    rt = jnp.zeros((D, LANES), F32).at[:, :E].set(router).astype(BF16)
    kern = functools.partial(_route_kernel, win=win, n_experts=E)
    return pl.pallas_call(
        kern,
        grid=(nwin // per_step,),
        in_specs=[
            pl.BlockSpec((blk, D), lambda w: (w, 0)),
            pl.BlockSpec((D, LANES), lambda w: (0, 0)),
        ],
        out_specs=[
            pl.BlockSpec((blk, LANES), lambda w: (w, 0)),
            pl.BlockSpec((per_step, SUBLANES, win), lambda w: (w, 0, 0)),
            pl.BlockSpec((per_step, 1, LANES), lambda w: (w, 0, 0)),
        ],
        out_shape=[
            jax.ShapeDtypeStruct((T, LANES), F32),
            jax.ShapeDtypeStruct((nwin, SUBLANES, win), F32),
            jax.ShapeDtypeStruct((nwin, 1, LANES), jnp.int32),
        ],
        compiler_params=pltpu.CompilerParams(
            dimension_semantics=("arbitrary",), vmem_limit_bytes=VMEM_LIMIT_BYTES),
        name="moe_route",
    )(hn, rt)


def _compact_kernel(seg_ref, off_ref, cnt_ref, fill_ref, hn_ref, drow_ref, xs_ref,
                    stage_ref, zero_ref, sem, *, win, n_experts, rows, tm):
    w = pl.program_id(0)
    nwin = pl.num_programs(0)
    slot = w % 2
    d = hn_ref.shape[-1]
    sizes = _piece_sizes(win)

    def segments(ww, sl, start):
        for e in range(n_experts):
            src0 = seg_ref[ww * n_experts + e]
            dst0 = off_ref[ww * n_experts + e]

            def piece(offset, size):
                cp = pltpu.make_async_copy(
                    stage_ref.at[sl, pl.ds(pl.multiple_of(src0 + offset, BF16_ROWS), size)],
                    xs_ref.at[pl.ds(pl.multiple_of(dst0 + offset, BF16_ROWS), size)],
                    sem.at[sl, e])
                cp.start() if start else cp.wait()

            _for_each_piece(cnt_ref[ww * n_experts + e], sizes, piece)

    def zero_rows(dst, size, e, start):
        cp = pltpu.make_async_copy(zero_ref.at[pl.ds(0, size)],
                                   xs_ref.at[pl.ds(pl.multiple_of(dst, BF16_ROWS), size)],
                                   sem.at[2, e])
        cp.start() if start else cp.wait()

    @pl.when(w == 0)
    def _():
        zero_ref[...] = jnp.zeros(zero_ref.shape, BF16)

    d1 = drow_ref[0, 0:1, :]
    d2 = drow_ref[0, 1:2, :]
    g1 = drow_ref[0, 2:3, :]
    g2 = drow_ref[0, 3:4, :]
    r = lax.broadcasted_iota(jnp.int32, (rows, win), 0).astype(F32)
    m1 = r == d1
    m2 = r == d2
    onehot = jnp.logical_or(m1, m2).astype(BF16)
    xg = jnp.dot(onehot, hn_ref[...], preferred_element_type=F32)
    stage_ref[slot, :, 0:d] = xg.astype(BF16)
    gate = jnp.sum(jnp.where(m1, g1, 0.0) + jnp.where(m2, g2, 0.0), axis=-1, keepdims=True)
    hi = gate.astype(BF16).astype(F32)
    mid = (gate - hi).astype(BF16).astype(F32)
    lo = gate - hi - mid
    lane = lax.broadcasted_iota(jnp.int32, (rows, LANES), 1)
    terms = jnp.where(lane == 0, hi, jnp.where(lane == 1, mid, jnp.where(lane == 2, lo, 0.0)))
    stage_ref[slot, :, d:d + LANES] = terms.astype(BF16)

    segments(w, slot, True)

    @pl.when(w > 0)
    def _():
        segments(w - 1, 1 - slot, False)

    @pl.when(w == nwin - 1)
    def _():
        segments(w, slot, False)
        gap_sizes = _piece_sizes(tm // 2)
        for start in (True, False):
            for e in range(n_experts):
                _for_each_piece(fill_ref[n_experts + e], gap_sizes,
                                lambda offset, size, e=e: zero_rows(fill_ref[e] + offset, size, e, start))
        tail0 = fill_ref[2 * n_experts]
        ntail = fill_ref[2 * n_experts + 1]
        tail_rows = zero_ref.shape[0]

        def tail_start(j, c):
            zero_rows(tail0 + j * tail_rows, tail_rows, 0, True)
            return c

        def tail_wait(j, c):
            zero_rows(tail0 + j * tail_rows, tail_rows, 0, False)
            return c

        lax.fori_loop(0, ntail, tail_start, 0)
        lax.fori_loop(0, ntail, tail_wait, 0)


def _compact(seg, off, cnt_pad, fill, hn, drow, *, win, n_experts, cap, tm):
    T, D = hn.shape
    nwin = T // win
    rows = _round_up(TOP_K * win + n_experts * (BF16_ROWS - 1), BF16_ROWS)
    kern = functools.partial(_compact_kernel, win=win, n_experts=n_experts, rows=rows, tm=tm)
    return pl.pallas_call(
        kern,
        grid_spec=pltpu.PrefetchScalarGridSpec(
            num_scalar_prefetch=4,
            grid=(nwin,),
            in_specs=[
                pl.BlockSpec((win, D), lambda w, *_: (w, 0)),
                pl.BlockSpec((1, SUBLANES, win), lambda w, *_: (w, 0, 0)),
            ],
            out_specs=pl.BlockSpec(memory_space=pl.ANY),
            scratch_shapes=[pltpu.VMEM((2, rows, D + LANES), BF16),
                            pltpu.VMEM((tm // 2, D + LANES), BF16),
                            pltpu.SemaphoreType.DMA((3, n_experts))],
        ),
        out_shape=jax.ShapeDtypeStruct((cap, D + LANES), BF16),
        compiler_params=pltpu.CompilerParams(
            dimension_semantics=("arbitrary",), vmem_limit_bytes=VMEM_LIMIT_BYTES),
        name="moe_compact",
    )(seg, off, cnt_pad, fill, hn, drow)


def _expert_kernel(te_ref, tv_ref, xs_ref, wg_ref, wu_ref, wd_ref, ys_ref):
    i = pl.program_id(0)
    tm, d = ys_ref.shape
    part = tm // MOE_TILE_PARTS

    for k in range(MOE_TILE_PARTS + 1):
        @pl.when(tv_ref[i] == k)
        def _(k=k):
            n = k * part
            if n:
                y = _swiglu_tile(xs_ref[0:n, 0:d], wg_ref.at[0], wu_ref.at[0], wd_ref.at[0])
                gate = jnp.sum(xs_ref[0:n, d:d + LANES].astype(F32), axis=-1, keepdims=True)
                ys_ref[0:n, :] = (gate * y).astype(BF16)
            if n < tm:
                ys_ref[n:tm, :] = jnp.zeros((tm - n, d), BF16)


def _experts(tile_e, tile_v, xs, wg, wu, wd, *, tm):
    cap = xs.shape[0]
    _, D, FF = wg.shape
    assert cap == tile_e.shape[0] * tm and xs.shape[1] == D + LANES and wg.dtype == BF16
    return pl.pallas_call(
        _expert_kernel,
        grid_spec=pltpu.PrefetchScalarGridSpec(
            num_scalar_prefetch=2,
            grid=(cap // tm,),
            in_specs=[
                pl.BlockSpec((tm, D + LANES), lambda i, te, tv: (i, 0)),
                pl.BlockSpec((1, D, FF), lambda i, te, tv: (te[i], 0, 0)),
                pl.BlockSpec((1, D, FF), lambda i, te, tv: (te[i], 0, 0)),
                pl.BlockSpec((1, FF, D), lambda i, te, tv: (te[i], 0, 0)),
            ],
            out_specs=pl.BlockSpec((tm, D), lambda i, te, tv: (i, 0)),
        ),
        out_shape=jax.ShapeDtypeStruct((cap, D), BF16),
        compiler_params=pltpu.CompilerParams(
            dimension_semantics=("arbitrary",), vmem_limit_bytes=VMEM_LIMIT_BYTES),
        name="moe_experts",
    )(tile_e, tile_v, xs, wg, wu, wd)


def _combine_kernel(seg_ref, off_ref, cnt_ref, x_ref, dcol_ref, fw_ref, ys_ref, o_ref,
                    stage_ref, sem, *, win, n_experts):
    w = pl.program_id(0)
    nwin = pl.num_programs(0)
    slot = w % 2
    rows = stage_ref.shape[1]
    sizes = _piece_sizes(win)

    def segments(ww, sl, start):
        for e in range(n_experts):
            src0 = off_ref[ww * n_experts + e]
            dst0 = seg_ref[ww * n_experts + e]

            def piece(offset, size):
                cp = pltpu.make_async_copy(
                    ys_ref.at[pl.ds(pl.multiple_of(src0 + offset, BF16_ROWS), size)],
                    stage_ref.at[sl, pl.ds(pl.multiple_of(dst0 + offset, BF16_ROWS), size)],
                    sem.at[sl, e])
                cp.start() if start else cp.wait()

            _for_each_piece(cnt_ref[ww * n_experts + e], sizes, piece)

    @pl.when(w == 0)
    def _():
        stage_ref[...] = jnp.zeros(stage_ref.shape, BF16)
        segments(0, 0, True)

    @pl.when(w + 1 < nwin)
    def _():
        segments(w + 1, 1 - slot, True)

    segments(w, slot, False)

    d1 = dcol_ref[:, 0:1]
    d2 = dcol_ref[:, 1:2]
    r = lax.broadcasted_iota(jnp.int32, (win, rows), 1).astype(F32)
    onehot = jnp.logical_or(r == d1, r == d2).astype(BF16)
    y = jnp.dot(onehot, stage_ref[slot], preferred_element_type=F32)
    o_ref[...] = _rms_norm(x_ref[...] + y, fw_ref[...])


def _combine(seg, off, cnt_pad, x2, dcol, fw, ys, *, win, n_experts):
    T, D = x2.shape
    rows = _round_up(TOP_K * win + n_experts * (BF16_ROWS - 1), BF16_ROWS)
    kern = functools.partial(_combine_kernel, win=win, n_experts=n_experts)
    return pl.pallas_call(
        kern,
        grid_spec=pltpu.PrefetchScalarGridSpec(
            num_scalar_prefetch=3,
            grid=(T // win,),
            in_specs=[
                pl.BlockSpec((win, D), lambda w, *_: (w, 0)),
                pl.BlockSpec((win, LANES), lambda w, *_: (w, 0)),
                pl.BlockSpec((1, D), lambda w, *_: (0, 0)),
                pl.BlockSpec(memory_space=pl.ANY),
            ],
            out_specs=pl.BlockSpec((win, D), lambda w, *_: (w, 0)),
            scratch_shapes=[pltpu.VMEM((2, rows, D), BF16),
                            pltpu.SemaphoreType.DMA((2, n_experts))],
        ),
        out_shape=jax.ShapeDtypeStruct((T, D), F32),
        compiler_params=pltpu.CompilerParams(
            dimension_semantics=("arbitrary",), vmem_limit_bytes=VMEM_LIMIT_BYTES),
        name="moe_combine",
    )(seg, off, cnt_pad, x2, dcol, fw.reshape(1, D).astype(F32), ys)


def _moe(x2, hn, router, wg, wu, wd, fw, *, win=MOE_WINDOW, tm=MOE_TILE):
    T, D = x2.shape
    E = router.shape[-1]
    nwin = T // win
    dcol, drow, cnt = _route(hn, router, win=win)

    cnt = cnt.reshape(nwin, LANES)[:, SUBLANES:SUBLANES + E]
    cnt_pad = _round_up(cnt, BF16_ROWS)
    seg = jnp.cumsum(cnt_pad, axis=1) - cnt_pad
    total = jnp.sum(cnt_pad, axis=0)
    region = _round_up(total, tm)
    r_end = jnp.cumsum(region)
    r_start = r_end - region
    off = r_start[None, :] + jnp.cumsum(cnt_pad, axis=0) - cnt_pad
    max_rows = TOP_K * T + nwin * E * (BF16_ROWS - 1)
    ntiles_max = max_rows // tm + E
    cap = ntiles_max * tm
    ti = jnp.arange(ntiles_max)
    t_end = r_end // tm
    tile_e = jnp.minimum(jnp.sum(ti[:, None] >= t_end[None, :], axis=1), E - 1).astype(jnp.int32)
    mine = tile_e[:, None] == jnp.arange(E)[None, :]
    live = jnp.sum(jnp.where(mine, total[None, :] - (ti[:, None] * tm - r_start[None, :]), 0), axis=1)
    live = jnp.clip(live, 0, tm)
    part = tm // MOE_TILE_PARTS
    tile_v = jnp.where(ti < t_end[-1], (live + part - 1) // part, 0).astype(jnp.int32)
    tail = jnp.stack([r_end[-1], (cap - r_end[-1]) // (tm // 2)])
    fill = jnp.concatenate([r_start + total, region - total, tail])
    flat = lambda a: a.reshape(-1).astype(jnp.int32)
    seg, off, cnt_pad, fill = flat(seg), flat(off), flat(cnt_pad), flat(fill)

    xs = _compact(seg, off, cnt_pad, fill, hn, drow, win=win, n_experts=E, cap=cap, tm=tm)
    ys = _experts(tile_e, tile_v, xs, wg, wu, wd, tm=tm)
    return _combine(seg, off, cnt_pad, x2, dcol, fw, ys, win=win, n_experts=E)


def kernel(x, norm_mix, w_in, conv_w, conv_b, conv_ng, conv_nb, sgu_ng, sgu_nb, sgu_w, sgu_b, w_out,
           norm_ffn, ffn_wg, ffn_wu, ffn_wd, moe_router, moe_wg, moe_wu, moe_wd, norm_final):
    B, S, D = x.shape
    depth = norm_mix.shape[0]
    assert depth == 2, "trunk is one dense layer followed by one MoE layer"
    mix = functools.partial(_mixer, ts=MIXER_TILE, sub=MIXER_SUB_TILE)
    mixer_params = (norm_mix, w_in, conv_w, conv_b, conv_ng, conv_nb, sgu_ng, sgu_nb, sgu_w, sgu_b, w_out)
    x, ffn_wg_b, ffn_wu_b, ffn_wd_b, w_in1_b, w_out1_b, moe_wg_b, moe_wu_b = mix(
        x, 0, *mixer_params,
        cast=((ffn_wg, 0), (ffn_wu, 0), (ffn_wd, 0), (w_in, 1), (w_out, 1), (moe_wg, 0), (moe_wu, 0)))
    x, hn, moe_wd_b = mix(x, 1, *mixer_params, proj=(w_in1_b, w_out1_b),
                          ffn=(0, norm_ffn, ffn_wg_b, ffn_wu_b, ffn_wd_b), cast=((moe_wd, 0),))
    y2 = _moe(x.reshape(B * S, D), hn.reshape(B * S, D), moe_router[0], moe_wg_b.reshape(moe_wg[0].shape),
              moe_wu_b.reshape(moe_wu[0].shape), moe_wd_b.reshape(moe_wd[0].shape), norm_final)
    return y2.reshape(B, S, D)
```

```python
import functools

import jax
import jax.numpy as jnp
from jax import lax
from jax.experimental import pallas as pl
from jax.experimental.pallas import tpu as pltpu

F32 = jnp.float32
BF16 = jnp.bfloat16

EPS = 1e-6
CONV_WIDTH = 31
CONV_GROUPS = 8
SGU_HEADS = 4
CHUNK = 128
TOP_K = 2

SUBLANES = 8
LANES = 128
BF16_ROWS = 16
CONV_HALO = 32
VMEM_LIMIT_BYTES = 60 * 1024 * 1024
MIXER_TILE = 512
MIXER_SUB_TILE = 256
_MIX_STAGES = 3
_MIX_PIECES = SUBLANES

MOE_WINDOW = 512
MOE_TILE = 512
MOE_TILE_PARTS = 4


def _rms_norm(x, g):
    ms = jnp.mean(x * x, axis=-1, keepdims=True)
    return x * lax.rsqrt(ms + EPS) * g


def _gelu(x):
    return 0.5 * x * (1.0 + lax.erf(x * (2.0 ** -0.5)))


def _sigmoid(x):
    return 0.5 * jnp.tanh(0.5 * x) + 0.5


def _silu(x):
    h = 0.5 * x
    return h * jnp.tanh(h) + h


def _cast_plan(rows, nsteps):
    nblk = nsteps
    while rows % nblk or (rows // nblk) % BF16_ROWS:
        nblk //= 2
        assert nblk >= 1, rows
    return nblk, rows // nblk


def _cast_specs(items, nsteps, step_of):
    arrs, in_specs, out_specs, shapes = [], [], [], []
    for a, layer in items:
        a = a.reshape(a.shape[0], -1, a.shape[-1])
        nblk, blk = _cast_plan(a.shape[1], nsteps)
        blk_of = lambda *g, nblk=nblk: jnp.minimum(step_of(*g), nblk - 1)
        arrs.append(a)
        in_specs.append(pl.BlockSpec((None, blk, a.shape[2]),
                                     lambda *g, layer=layer, blk_of=blk_of: (layer, blk_of(*g), 0)))
        out_specs.append(pl.BlockSpec((blk, a.shape[2]), lambda *g, blk_of=blk_of: (blk_of(*g), 0)))
        shapes.append(jax.ShapeDtypeStruct(a.shape[1:], BF16))
    return arrs, in_specs, out_specs, shapes


def _cast_blocks(src_refs, dst_refs):
    for src, dst in zip(src_refs, dst_refs):
        dst[...] = src[...].astype(BF16)


_MIXER_INPUTS = 13
_FFN_INPUTS = 5
_FFN_COLS = 256


def _mixer_kernel(*refs, layer, ts, sub, cc, sc, n_cast, ffn_layer):
    (x_ref, nw_all, win_in_ref, cw_ref, cb_all, cng_all, cnb_all, sng_all, snb_all,
     sw_ref, sb_ref, wout_in_ref, gavg_ref) = refs[:_MIXER_INPUTS]
    nw_ref, cb_ref, cng_ref, cnb_ref, sng_ref, snb_ref = (
        r.at[layer:layer + 1] for r in (nw_all, cb_all, cng_all, cnb_all, sng_all, snb_all))
    n_in = _MIXER_INPUTS
    if ffn_layer is not None:
        xn_ref, nwf_all, wg_ref, wu_ref, wd_ref = refs[n_in:n_in + _FFN_INPUTS]
        nwf_ref = nwf_all.at[ffn_layer:ffn_layer + 1]
        n_in += _FFN_INPUTS
    cast_in = refs[n_in:n_in + n_cast]
    o_ref = refs[n_in + n_cast]
    n_out = 1
    if ffn_layer is not None:
        hn_ref = refs[n_in + n_cast + 1]
        nwo_ref = nwf_all.at[layer:layer + 1]
        n_out = 2
    cast_out = refs[n_in + n_cast + n_out:n_in + 2 * n_cast + n_out]
    scratch = refs[n_in + 2 * n_cast + n_out:]
    cbuf_ref = scratch[0]
    s = pl.program_id(1)
    _cast_blocks(cast_in, cast_out)

    if win_in_ref.dtype == BF16:
        win_ref, wout_ref = win_in_ref, wout_in_ref
    else:
        win_ref, wout_ref = scratch[1:3]

        @pl.when(jnp.logical_and(pl.program_id(0) == 0, s == 0))
        def _():
            win_ref[...] = win_in_ref[...].astype(BF16)
            wout_ref[...] = wout_in_ref[...].astype(BF16)

    @pl.when(s == 0)
    def _():
        cbuf_ref[0:CONV_HALO, :] = jnp.zeros((CONV_HALO, cc), F32)

    row = lax.broadcasted_iota(jnp.int32, (CHUNK, CHUNK), 0)
    col = lax.broadcasted_iota(jnp.int32, (CHUNK, CHUNK), 1)
    tril = row >= col
    hd = sc // SGU_HEADS
    nq = -(-CONV_WIDTH // SUBLANES)

    nsub = ts // sub
    d_in = 2 * cc + 2 * sc
    cbw = d_in // _MIX_PIECES

    def ffn_stage(x):
        hf = _rms_norm(x, nwf_ref[...]).astype(BF16)
        ff = wg_ref.shape[-1]
        fcols = [(c0, min(ff, c0 + _FFN_COLS)) for c0 in range(0, ff, _FFN_COLS)]
        per = -(-len(fcols) // 3)
        gu = []
        for w_ref in (wg_ref, wu_ref):
            blocks = []
            for i, (c0, c1) in enumerate(fcols):
                blocks.append(jnp.dot(hf, w_ref[:, c0:c1], preferred_element_type=F32))
                if i % per == per - 1 or i == len(fcols) - 1:
                    yield
            gu.append(jnp.concatenate(blocks, axis=1))
        act = (_silu(gu[0]) * gu[1]).astype(BF16)
        dm = wd_ref.shape[-1]
        yblk = []
        for i in range(4):
            c0, c1 = i * dm // 4, (i + 1) * dm // 4
            yblk.append(jnp.dot(act, wd_ref[:, c0:c1], preferred_element_type=F32))
            if i % 2 == 1 and i < 3:
                yield
        return x + jnp.concatenate(yblk, axis=1)

    if ffn_layer is not None:
        xcarry_ref = scratch[-1]

        @pl.when(jnp.logical_and(pl.program_id(0) == 0, s == 0))
        def _():
            stage = ffn_stage(x_ref[0, 0:sub, :])
            try:
                while True:
                    next(stage)
            except StopIteration as done:
                xcarry_ref[...] = done.value

    def next_first_sub_tile():
        xcarry_ref[...] = yield from ffn_stage(xn_ref[0])
        yield

    def sub_tile(j):
        r0 = j * sub
        if ffn_layer is None:
            x = x_ref[0, r0:r0 + sub, :]
        elif j == 0:
            x = xcarry_ref[...]
        else:
            x = yield from ffn_stage(x_ref[0, r0:r0 + sub, :])
            yield
        h = _rms_norm(x, nw_ref[...]).astype(BF16)
        zblk = []
        for k in range(_MIX_PIECES):
            zblk.append(jnp.dot(h, win_ref[:, k * cbw:(k + 1) * cbw], preferred_element_type=F32))
            yield
        z = jnp.concatenate(zblk, axis=1)
        a = z[:, :cc]
        gate = z[:, cc:2 * cc]
        u = z[:, 2 * cc:2 * cc + sc]
        v = z[:, 2 * cc + sc:]

        base = CONV_HALO + r0
        cbuf_ref[base:base + sub, :] = a * _sigmoid(gate)
        acc = jnp.broadcast_to(cb_ref[...], (sub, cc))
        for b in range(SUBLANES):
            zb = None
            for q in range(nq):
                d = SUBLANES * q + b
                if d >= CONV_WIDTH:
                    continue
                start = base - SUBLANES * (q + 1)
                term = (cw_ref[CONV_WIDTH - 1 - d]
                        * cbuf_ref[start:start + sub + SUBLANES, :])
                zb = term if zb is None else zb + term
            acc = acc + zb[SUBLANES - b:SUBLANES - b + sub, :]
            yield

        gavg = gavg_ref[...]
        acc_hi = acc.astype(BF16)
        acc_lo = (acc - acc_hi.astype(F32)).astype(BF16)
        mu = (jnp.dot(acc_hi, gavg, preferred_element_type=F32)
              + jnp.dot(acc_lo, gavg, preferred_element_type=F32))
        yield
        dev = acc - mu
        var = jnp.dot((dev * dev).astype(BF16), gavg, preferred_element_type=F32)
        yield
        cn = dev * lax.rsqrt(var + EPS) * cng_ref[...] + cnb_ref[...]
        c_out = _silu(cn).astype(BF16)
        yield

        u = _gelu(u)
        v = _gelu(v)
        yield
        g_cols = []
        for hh in range(SGU_HEADS):
            vh = v[:, hh * hd:(hh + 1) * hd]
            mu_h = jnp.mean(vh, axis=-1, keepdims=True)
            dh = vh - mu_h
            var_h = jnp.mean(dh * dh, axis=-1, keepdims=True)
            vn = (dh * lax.rsqrt(var_h + EPS) * sng_ref[:, hh * hd:(hh + 1) * hd]
                  + snb_ref[:, hh * hd:(hh + 1) * hd]).astype(BF16)
            ws = jnp.where(tril, sw_ref[hh], 0.0).astype(BF16)
            rows = []
            for ci in range(sub // CHUNK):
                sp = jnp.dot(ws, vn[ci * CHUNK:(ci + 1) * CHUNK, :], preferred_element_type=F32)
                rows.append(sp + sb_ref[hh])
            sp_h = rows[0] if len(rows) == 1 else jnp.concatenate(rows, axis=0)
            g_cols.append((u[:, hh * hd:(hh + 1) * hd] * sp_h).astype(BF16))
            if hh % 2 == 1:
                yield
        g_out = jnp.concatenate(g_cols, axis=1)

        yc = jnp.dot(c_out, wout_ref[0:cc, :], preferred_element_type=F32)
        yield
        yg = jnp.dot(g_out, wout_ref[cc:cc + sc, :], preferred_element_type=F32)
        out = x + (yc + yg)
        o_ref[0, r0:r0 + sub, :] = out
        if ffn_layer is not None:
            hn_ref[0, r0:r0 + sub, :] = _rms_norm(out, nwo_ref[...]).astype(BF16)
        yield

    P = _MIX_PIECES
    if ffn_layer is None:
        plan = [(sub_tile(j), P * j, P * _MIX_STAGES) for j in range(nsub)]
    else:
        plan = [(sub_tile(0), 0, P * _MIX_STAGES)]
        plan += [(sub_tile(j), P * (2 * j - 1), P * (_MIX_STAGES + 1)) for j in range(1, nsub)]
        plan += [(next_first_sub_tile(), P * (2 * nsub - 1), P)]
    for tick in range(max(t0 + n for _, t0, n in plan)):
        for gen, t0, n in reversed(plan):
            if 0 <= tick - t0 < n:
                next(gen)

    cbuf_ref[0:CONV_HALO, :] = cbuf_ref[ts:ts + CONV_HALO, :]


def _mixer(x, layer, norm_mix, w_in, conv_w, conv_b, conv_ng, conv_nb, sgu_ng, sgu_nb, sgu_w, sgu_b,
           w_out, *, ts, sub, cast=(), proj=None, ffn=None):
    B, S, D = x.shape
    L = conv_w.shape[0]
    cc = conv_w.shape[-1]
    sc = sgu_ng.shape[-1]
    d_in = w_in.shape[-1]
    hd = sc // SGU_HEADS
    assert S % ts == 0 and ts % sub == 0 and sub % CHUNK == 0 and d_in == 2 * cc + 2 * sc
    ns = S // ts
    gs = cc // CONV_GROUPS
    gid = jnp.arange(cc) // gs
    gavg = jnp.where(gid[:, None] == gid[None, :], 1.0 / gs, 0.0).astype(BF16)
    sb_full = jnp.broadcast_to(sgu_b[:, :, :, None], (L, SGU_HEADS, CHUNK, hd))
    whole = lambda p: pl.BlockSpec(p.shape, lambda b, s: (0,) * p.ndim)
    layer_block = lambda *shape: pl.BlockSpec((None,) + shape, lambda b, s: (layer,) + (0,) * len(shape))
    cast_arrs, cast_in_specs, cast_out_specs, cast_shapes = _cast_specs(
        cast, B * ns, lambda b, s: b * ns + s)
    kern = functools.partial(_mixer_kernel, layer=layer, ts=ts, sub=sub, cc=cc, sc=sc, n_cast=len(cast),
                             ffn_layer=None if ffn is None else ffn[0])
    scratch = [pltpu.VMEM((CONV_HALO + ts, cc), F32)]
    if proj is None:
        w_in_spec, w_out_spec = layer_block(D, d_in), layer_block(cc + sc, D)
        scratch += [pltpu.VMEM((D, d_in), BF16), pltpu.VMEM((cc + sc, D), BF16)]
    else:
        w_in, w_out = proj
        assert w_in.dtype == BF16 and w_out.dtype == BF16
        w_in_spec, w_out_spec = whole(w_in), whole(w_out)
    ffn_args, ffn_specs, n_act = (), [], 1
    if ffn is not None:
        n_act = 2
        def next_first(b, s):
            step = jnp.minimum(b * ns + s + 1, B * ns - 1)
            return (step // ns, (step % ns) * (ts // sub), 0)

        ffn_args = (x,) + tuple(ffn[1:])
        ffn_specs = [pl.BlockSpec((1, sub, D), next_first)] + [whole(a) for a in ffn[1:]]
        scratch += [pltpu.VMEM((sub, D), F32)]
    in_specs = [
        pl.BlockSpec((1, ts, D), lambda b, s: (b, s, 0)),
        whole(norm_mix),
        w_in_spec,
        layer_block(CONV_WIDTH, 1, cc),
        whole(conv_b), whole(conv_ng), whole(conv_nb),
        whole(sgu_ng), whole(sgu_nb),
        layer_block(SGU_HEADS, CHUNK, CHUNK),
        layer_block(SGU_HEADS, CHUNK, hd),
        w_out_spec,
        whole(gavg),
    ]
    assert len(in_specs) == _MIXER_INPUTS and len(ffn_args) in (0, _FFN_INPUTS)
    in_specs += ffn_specs
    return pl.pallas_call(
        kern,
        grid=(B, ns),
        in_specs=in_specs + cast_in_specs,
        out_specs=[pl.BlockSpec((1, ts, D), lambda b, s: (b, s, 0))] * n_act + cast_out_specs,
        out_shape=([jax.ShapeDtypeStruct((B, S, D), F32), jax.ShapeDtypeStruct((B, S, D), BF16)][:n_act]
                   + cast_shapes),
        scratch_shapes=scratch,
        compiler_params=pltpu.CompilerParams(
            dimension_semantics=("arbitrary", "arbitrary"),
            vmem_limit_bytes=VMEM_LIMIT_BYTES),
        name="mixer",
    )(x, norm_mix, w_in, conv_w, conv_b, conv_ng, conv_nb, sgu_ng, sgu_nb, sgu_w, sb_full, w_out,
      gavg, *ffn_args, *cast_arrs)


def _swiglu_tile(h, wg_ref, wu_ref, wd_ref):
    g = jnp.dot(h, wg_ref[...], preferred_element_type=F32)
    u = jnp.dot(h, wu_ref[...], preferred_element_type=F32)
    a = (_silu(g) * u).astype(BF16)
    return jnp.dot(a, wd_ref[...], preferred_element_type=F32)


def _round_up(n, m):
    return (n + m - 1) // m * m


def _piece_sizes(largest):
    sizes = []
    s = largest
    while s >= BF16_ROWS:
        sizes.append(s)
        s //= 2
    assert sizes and sizes[-1] == BF16_ROWS
    return sizes


def _for_each_piece(count, sizes, fn):
    for s in sizes:
        @pl.when(jnp.bitwise_and(count, s) != 0)
        def _(s=s):
            fn(jnp.bitwise_and(count, -2 * s), s)


def _route_kernel(hn_ref, rt_ref, dcol_ref, drow_ref, cnt_ref, *, win, n_experts):
    for k in range(hn_ref.shape[0] // win):
        rows = slice(k * win, (k + 1) * win)
        _route_window(hn_ref.at[rows], rt_ref, dcol_ref.at[rows], drow_ref.at[k], cnt_ref.at[k],
                      n_experts)


def _route_window(hn_ref, rt_ref, dcol_ref, drow_ref, cnt_ref, n_experts):
    logits = jnp.dot(hn_ref[...], rt_ref[...], preferred_element_type=F32)
    lane = lax.broadcasted_iota(jnp.int32, logits.shape, 1)
    lt = jnp.transpose(jnp.where(lane < n_experts, logits, -jnp.inf))[0:SUBLANES, :]
    win = lt.shape[1]
    sub = lax.broadcasted_iota(jnp.int32, lt.shape, 0)
    m1 = jnp.max(lt, axis=0, keepdims=True)
    i1 = jnp.min(jnp.where(lt == m1, sub, SUBLANES), axis=0, keepdims=True)
    rest = jnp.where(sub == i1, -jnp.inf, lt)
    m2 = jnp.max(rest, axis=0, keepdims=True)
    i2 = jnp.min(jnp.where(rest == m2, sub, SUBLANES), axis=0, keepdims=True)
    e2 = jnp.exp(m2 - m1)
    den = 1.0 + e2
    g1 = 1.0 / den
    g2 = e2 / den
    sel1 = sub == i1
    sel2 = sub == i2
    self = jnp.logical_or(sel1, sel2).astype(F32)
    tok = lax.broadcasted_iota(jnp.int32, lt.shape, 1)
    inc = self
    step = 1
    while step < win:
        inc = inc + jnp.where(tok >= step, pltpu.roll(inc, step, axis=1), 0.0)
        step *= 2
    rank = inc - self
    cnt = jnp.sum(self, axis=1, keepdims=True)
    cnt_pad = jnp.floor((cnt + (BF16_ROWS - 1.0)) * (1.0 / BF16_ROWS)) * BF16_ROWS
    seg = jnp.zeros_like(cnt_pad)
    for e in range(n_experts - 1):
        seg = seg + jnp.where(sub[:, 0:1] > e, cnt_pad[e:e + 1, :], 0.0)
    pos = seg + rank
    dest1 = jnp.sum(jnp.where(sel1, pos, 0.0), axis=0, keepdims=True)
    dest2 = jnp.sum(jnp.where(sel2, pos, 0.0), axis=0, keepdims=True)
    info = jnp.where(sub == 0, dest1, jnp.where(sub == 1, dest2,
                     jnp.where(sub == 2, g1, jnp.where(sub == 3, g2, -1.0))))
    drow_ref[...] = info
    cols = jnp.concatenate([info, jnp.broadcast_to(cnt, (SUBLANES, win)),
                            jnp.full((LANES - 2 * SUBLANES, win), -1.0, F32)], axis=0)
    cols = jnp.transpose(cols)
    dcol_ref[...] = cols
    cnt_ref[...] = cols[0:1, :].astype(jnp.int32)


def _route(hn, router, *, win):
    T, D = hn.shape
    E = router.shape[-1]
    assert T % win == 0 and E <= SUBLANES
    nwin = T // win
    per_step = max(p for p in (8, 4, 2, 1) if nwin % p == 0)
    blk = win * per_step
    rt = jnp.zeros((D, LANES), F32).at[:, :E].set(router).astype(BF16)
    kern = functools.partial(_route_kernel, win=win, n_experts=E)
    return pl.pallas_call(
        kern,
        grid=(nwin // per_step,),
        in_specs=[
            pl.BlockSpec((blk, D), lambda w: (w, 0)),
            pl.BlockSpec((D, LANES), lambda w: (0, 0)),
        ],
        out_specs=[
            pl.BlockSpec((blk, LANES), lambda w: (w, 0)),
            pl.BlockSpec((per_step, SUBLANES, win), lambda w: (w, 0, 0)),
            pl.BlockSpec((per_step, 1, LANES), lambda w: (w, 0, 0)),
        ],
        out_shape=[
            jax.ShapeDtypeStruct((T, LANES), F32),
            jax.ShapeDtypeStruct((nwin, SUBLANES, win), F32),
            jax.ShapeDtypeStruct((nwin, 1, LANES), jnp.int32),
        ],
        compiler_params=pltpu.CompilerParams(
            dimension_semantics=("arbitrary",), vmem_limit_bytes=VMEM_LIMIT_BYTES),
        name="moe_route",
    )(hn, rt)


def _compact_kernel(seg_ref, off_ref, cnt_ref, fill_ref, hn_ref, drow_ref, xs_ref,
                    stage_ref, zero_ref, sem, *, win, n_experts, rows, tm):
    w = pl.program_id(0)
    nwin = pl.num_programs(0)
    slot = w % 2
    d = hn_ref.shape[-1]
    sizes = _piece_sizes(win)

    def segments(ww, sl, start):
        for e in range(n_experts):
            src0 = seg_ref[ww * n_experts + e]
            dst0 = off_ref[ww * n_experts + e]

            def piece(offset, size):
                cp = pltpu.make_async_copy(
                    stage_ref.at[sl, pl.ds(pl.multiple_of(src0 + offset, BF16_ROWS), size)],
                    xs_ref.at[pl.ds(pl.multiple_of(dst0 + offset, BF16_ROWS), size)],
                    sem.at[sl, e])
                cp.start() if start else cp.wait()

            _for_each_piece(cnt_ref[ww * n_experts + e], sizes, piece)

    def zero_rows(dst, size, e, start):
        cp = pltpu.make_async_copy(zero_ref.at[pl.ds(0, size)],
                                   xs_ref.at[pl.ds(pl.multiple_of(dst, BF16_ROWS), size)],
                                   sem.at[2, e])
        cp.start() if start else cp.wait()

    @pl.when(w == 0)
    def _():
        zero_ref[...] = jnp.zeros(zero_ref.shape, BF16)

    d1 = drow_ref[0, 0:1, :]
    d2 = drow_ref[0, 1:2, :]
    g1 = drow_ref[0, 2:3, :]
    g2 = drow_ref[0, 3:4, :]
    r = lax.broadcasted_iota(jnp.int32, (rows, win), 0).astype(F32)
    m1 = r == d1
    m2 = r == d2
    onehot = jnp.logical_or(m1, m2).astype(BF16)
    xg = jnp.dot(onehot, hn_ref[...], preferred_element_type=F32)
    stage_ref[slot, :, 0:d] = xg.astype(BF16)
    gate = jnp.sum(jnp.where(m1, g1, 0.0) + jnp.where(m2, g2, 0.0), axis=-1, keepdims=True)
    hi = gate.astype(BF16).astype(F32)
    mid = (gate - hi).astype(BF16).astype(F32)
    lo = gate - hi - mid
    lane = lax.broadcasted_iota(jnp.int32, (rows, LANES), 1)
    terms = jnp.where(lane == 0, hi, jnp.where(lane == 1, mid, jnp.where(lane == 2, lo, 0.0)))
    stage_ref[slot, :, d:d + LANES] = terms.astype(BF16)

    segments(w, slot, True)

    @pl.when(w > 0)
    def _():
        segments(w - 1, 1 - slot, False)

    @pl.when(w == nwin - 1)
    def _():
        segments(w, slot, False)
        gap_sizes = _piece_sizes(tm // 2)
        for start in (True, False):
            for e in range(n_experts):
                _for_each_piece(fill_ref[n_experts + e], gap_sizes,
                                lambda offset, size, e=e: zero_rows(fill_ref[e] + offset, size, e, start))
        tail0 = fill_ref[2 * n_experts]
        ntail = fill_ref[2 * n_experts + 1]
        tail_rows = zero_ref.shape[0]

        def tail_start(j, c):
            zero_rows(tail0 + j * tail_rows, tail_rows, 0, True)
            return c

        def tail_wait(j, c):
            zero_rows(tail0 + j * tail_rows, tail_rows, 0, False)
            return c

        lax.fori_loop(0, ntail, tail_start, 0)
        lax.fori_loop(0, ntail, tail_wait, 0)


def _compact(seg, off, cnt_pad, fill, hn, drow, *, win, n_experts, cap, tm):
    T, D = hn.shape
    nwin = T // win
    rows = _round_up(TOP_K * win + n_experts * (BF16_ROWS - 1), BF16_ROWS)
    kern = functools.partial(_compact_kernel, win=win, n_experts=n_experts, rows=rows, tm=tm)
    return pl.pallas_call(
        kern,
        grid_spec=pltpu.PrefetchScalarGridSpec(
            num_scalar_prefetch=4,
            grid=(nwin,),
            in_specs=[
                pl.BlockSpec((win, D), lambda w, *_: (w, 0)),
                pl.BlockSpec((1, SUBLANES, win), lambda w, *_: (w, 0, 0)),
            ],
            out_specs=pl.BlockSpec(memory_space=pl.ANY),
            scratch_shapes=[pltpu.VMEM((2, rows, D + LANES), BF16),
                            pltpu.VMEM((tm // 2, D + LANES), BF16),
                            pltpu.SemaphoreType.DMA((3, n_experts))],
        ),
        out_shape=jax.ShapeDtypeStruct((cap, D + LANES), BF16),
        compiler_params=pltpu.CompilerParams(
            dimension_semantics=("arbitrary",), vmem_limit_bytes=VMEM_LIMIT_BYTES),
        name="moe_compact",
    )(seg, off, cnt_pad, fill, hn, drow)


def _expert_kernel(te_ref, tv_ref, xs_ref, wg_ref, wu_ref, wd_ref, ys_ref):
    i = pl.program_id(0)
    tm, d = ys_ref.shape
    part = tm // MOE_TILE_PARTS

    for k in range(MOE_TILE_PARTS + 1):
        @pl.when(tv_ref[i] == k)
        def _(k=k):
            n = k * part
            if n:
                y = _swiglu_tile(xs_ref[0:n, 0:d], wg_ref.at[0], wu_ref.at[0], wd_ref.at[0])
                gate = jnp.sum(xs_ref[0:n, d:d + LANES].astype(F32), axis=-1, keepdims=True)
                ys_ref[0:n, :] = (gate * y).astype(BF16)
            if n < tm:
                ys_ref[n:tm, :] = jnp.zeros((tm - n, d), BF16)


def _experts(tile_e, tile_v, xs, wg, wu, wd, *, tm):
    cap = xs.shape[0]
    _, D, FF = wg.shape
    assert cap == tile_e.shape[0] * tm and xs.shape[1] == D + LANES and wg.dtype == BF16
    return pl.pallas_call(
        _expert_kernel,
        grid_spec=pltpu.PrefetchScalarGridSpec(
            num_scalar_prefetch=2,
            grid=(cap // tm,),
            in_specs=[
                pl.BlockSpec((tm, D + LANES), lambda i, te, tv: (i, 0)),
                pl.BlockSpec((1, D, FF), lambda i, te, tv: (te[i], 0, 0)),
                pl.BlockSpec((1, D, FF), lambda i, te, tv: (te[i], 0, 0)),
                pl.BlockSpec((1, FF, D), lambda i, te, tv: (te[i], 0, 0)),
            ],
            out_specs=pl.BlockSpec((tm, D), lambda i, te, tv: (i, 0)),
        ),
        out_shape=jax.ShapeDtypeStruct((cap, D), BF16),
        compiler_params=pltpu.CompilerParams(
            dimension_semantics=("arbitrary",), vmem_limit_bytes=VMEM_LIMIT_BYTES),
        name="moe_experts",
    )(tile_e, tile_v, xs, wg, wu, wd)


def _combine_kernel(seg_ref, off_ref, cnt_ref, x_ref, dcol_ref, fw_ref, ys_ref, o_ref,
                    stage_ref, sem, *, win, n_experts):
    w = pl.program_id(0)
    nwin = pl.num_programs(0)
    slot = w % 2
    rows = stage_ref.shape[1]
    sizes = _piece_sizes(win)

    def segments(ww, sl, start):
        for e in range(n_experts):
            src0 = off_ref[ww * n_experts + e]
            dst0 = seg_ref[ww * n_experts + e]

            def piece(offset, size):
                cp = pltpu.make_async_copy(
                    ys_ref.at[pl.ds(pl.multiple_of(src0 + offset, BF16_ROWS), size)],
                    stage_ref.at[sl, pl.ds(pl.multiple_of(dst0 + offset, BF16_ROWS), size)],
                    sem.at[sl, e])
                cp.start() if start else cp.wait()

            _for_each_piece(cnt_ref[ww * n_experts + e], sizes, piece)

    @pl.when(w == 0)
    def _():
        stage_ref[...] = jnp.zeros(stage_ref.shape, BF16)
        segments(0, 0, True)

    @pl.when(w + 1 < nwin)
    def _():
        segments(w + 1, 1 - slot, True)

    segments(w, slot, False)

    d1 = dcol_ref[:, 0:1]
    d2 = dcol_ref[:, 1:2]
    r = lax.broadcasted_iota(jnp.int32, (win, rows), 1).astype(F32)
    onehot = jnp.logical_or(r == d1, r == d2).astype(BF16)
    y = jnp.dot(onehot, stage_ref[slot], preferred_element_type=F32)
    o_ref[...] = _rms_norm(x_ref[...] + y, fw_ref[...])


def _combine(seg, off, cnt_pad, x2, dcol, fw, ys, *, win, n_experts):
    T, D = x2.shape
    rows = _round_up(TOP_K * win + n_experts * (BF16_ROWS - 1), BF16_ROWS)
    kern = functools.partial(_combine_kernel, win=win, n_experts=n_experts)
    return pl.pallas_call(
        kern,
        grid_spec=pltpu.PrefetchScalarGridSpec(
            num_scalar_prefetch=3,
            grid=(T // win,),
            in_specs=[
                pl.BlockSpec((win, D), lambda w, *_: (w, 0)),
                pl.BlockSpec((win, LANES), lambda w, *_: (w, 0)),
                pl.BlockSpec((1, D), lambda w, *_: (0, 0)),
                pl.BlockSpec(memory_space=pl.ANY),
            ],
            out_specs=pl.BlockSpec((win, D), lambda w, *_: (w, 0)),
            scratch_shapes=[pltpu.VMEM((2, rows, D), BF16),
                            pltpu.SemaphoreType.DMA((2, n_experts))],
        ),
        out_shape=jax.ShapeDtypeStruct((T, D), F32),
        compiler_params=pltpu.CompilerParams(
            dimension_semantics=("arbitrary",), vmem_limit_bytes=VMEM_LIMIT_BYTES),
        name="moe_combine",
    )(seg, off, cnt_pad, x2, dcol, fw.reshape(1, D).astype(F32), ys)


def _moe(x2, hn, router, wg, wu, wd, fw, *, win=MOE_WINDOW, tm=MOE_TILE):
    T, D = x2.shape
    E = router.shape[-1]
    nwin = T // win
    dcol, drow, cnt = _route(hn, router, win=win)

    cnt = cnt.reshape(nwin, LANES)[:, SUBLANES:SUBLANES + E]
    cnt_pad = _round_up(cnt, BF16_ROWS)
    seg = jnp.cumsum(cnt_pad, axis=1) - cnt_pad
    total = jnp.sum(cnt_pad, axis=0)
    region = _round_up(total, tm)
    r_end = jnp.cumsum(region)
    r_start = r_end - region
    off = r_start[None, :] + jnp.cumsum(cnt_pad, axis=0) - cnt_pad
    max_rows = TOP_K * T + nwin * E * (BF16_ROWS - 1)
    ntiles_max = max_rows // tm + E
    cap = ntiles_max * tm
    ti = jnp.arange(ntiles_max)
    t_end = r_end // tm
    tile_e = jnp.minimum(jnp.sum(ti[:, None] >= t_end[None, :], axis=1), E - 1).astype(jnp.int32)
    mine = tile_e[:, None] == jnp.arange(E)[None, :]
    live = jnp.sum(jnp.where(mine, total[None, :] - (ti[:, None] * tm - r_start[None, :]), 0), axis=1)
    live = jnp.clip(live, 0, tm)
    part = tm // MOE_TILE_PARTS
    tile_v = jnp.where(ti < t_end[-1], (live + part - 1) // part, 0).astype(jnp.int32)
    tail = jnp.stack([r_end[-1], (cap - r_end[-1]) // (tm // 2)])
    fill = jnp.concatenate([r_start + total, region - total, tail])
    flat = lambda a: a.reshape(-1).astype(jnp.int32)
    seg, off, cnt_pad, fill = flat(seg), flat(off), flat(cnt_pad), flat(fill)

    xs = _compact(seg, off, cnt_pad, fill, hn, drow, win=win, n_experts=E, cap=cap, tm=tm)
    ys = _experts(tile_e, tile_v, xs, wg, wu, wd, tm=tm)
    return _combine(seg, off, cnt_pad, x2, dcol, fw, ys, win=win, n_experts=E)


def kernel(x, norm_mix, w_in, conv_w, conv_b, conv_ng, conv_nb, sgu_ng, sgu_nb, sgu_w, sgu_b, w_out,
           norm_ffn, ffn_wg, ffn_wu, ffn_wd, moe_router, moe_wg, moe_wu, moe_wd, norm_final):
    B, S, D = x.shape
    depth = norm_mix.shape[0]
    assert depth == 2, "trunk is one dense layer followed by one MoE layer"
    mix = functools.partial(_mixer, ts=MIXER_TILE, sub=MIXER_SUB_TILE)
    mixer_params = (norm_mix, w_in, conv_w, conv_b, conv_ng, conv_nb, sgu_ng, sgu_nb, sgu_w, sgu_b, w_out)
    x, ffn_wg_b, ffn_wu_b, ffn_wd_b, w_in1_b, w_out1_b, moe_wg_b, moe_wu_b = mix(
        x, 0, *mixer_params,
        cast=((ffn_wg, 0), (ffn_wu, 0), (ffn_wd, 0), (w_in, 1), (w_out, 1), (moe_wg, 0), (moe_wu, 0)))
    x, hn, moe_wd_b = mix(x, 1, *mixer_params, proj=(w_in1_b, w_out1_b),
                          ffn=(0, norm_ffn, ffn_wg_b, ffn_wu_b, ffn_wd_b), cast=((moe_wd, 0),))
    y2 = _moe(x.reshape(B * S, D), hn.reshape(B * S, D), moe_router[0], moe_wg_b.reshape(moe_wg[0].shape),
              moe_wu_b.reshape(moe_wu[0].shape), moe_wd_b.reshape(moe_wd[0].shape), norm_final)
    return y2.reshape(B, S, D)
```
